```python
import math
import jax
import jax.numpy as jnp
from jax import lax
import numpy as np

D_MODEL = 2048
BATCH = 2
SEQ = 4096
DEPTH = 2
DEC_BATCH = 32
DEC_SEQ = 8
PAST_LEN = 8192
PAGE_SIZE = 128

H_A = 6
DH_A = 64
DV_A = 2 * DH_A
W_A = H_A * DV_A
ATT_SCALE = DH_A ** -0.5
ATT_BLOCK = 128
NEG_INF = -1e30
H_B = 6
DK_B = 64
DV_B = 128
W_B = H_B * DV_B
RET_CHUNK = 128
ROPE_BASE = 10000.0
N_POOL_GROUPS = 4
POOL_GC = 128
POOL_WINDOWS = (2, 4, 8, 16)
W_C = N_POOL_GROUPS * POOL_GC
POOL_BUF = 15
MIX_W = W_A + W_B + W_C
COL_SIZES = (W_A, W_A, W_A, H_B * DK_B, H_B * DK_B, W_B, W_B, W_C)
IN_COLS = 5120
REL_BUCKETS = 32
REL_MAX_DIST = 128
D_FF = 5632
CONV_W = 3
EPS = 1e-6

kernel_name = 'hybrid_diffattn_retnet_pool_step'


def rmsnorm(x, g=None):
    xf = x.astype(jnp.float32)
    y = xf * lax.rsqrt(jnp.mean(xf * xf, axis=-1, keepdims=True) + EPS)
    if g is not None:
        y = y * g.astype(jnp.float32)
    return y.astype(x.dtype)


def split_cols(t):
    parts = []
    off = 0
    for s in COL_SIZES:
        parts.append(t[..., off:off + s])
        off += s
    return parts


def t5_bucket(dist):
    n = jnp.maximum(dist, 0)
    exact = REL_BUCKETS // 2
    nf = jnp.maximum(n, 1).astype(jnp.float32)
    large = exact + (jnp.log(nf / exact) / math.log(REL_MAX_DIST / exact) * (REL_BUCKETS - exact)).astype(jnp.int32)
    return jnp.where(n < exact, n, jnp.minimum(large, REL_BUCKETS - 1))


def rel_bias(table, qpos, kpos):
    b = t5_bucket(qpos[:, None] - kpos[None, :])
    return jnp.transpose(table[b].astype(jnp.float32), (2, 0, 1))


def rope(x, pos):
    half = x.shape[-1] // 2
    inv = ROPE_BASE ** (-jnp.arange(half, dtype=jnp.float32) / half)
    ang = pos.astype(jnp.float32)[:, None] * inv[None, :]
    cos = jnp.cos(ang)[None, :, None, :]
    sin = jnp.sin(ang)[None, :, None, :]
    xf = x.astype(jnp.float32)
    x1, x2 = xf[..., :half], xf[..., half:]
    return jnp.concatenate([x1 * cos - x2 * sin, x1 * sin + x2 * cos], axis=-1).astype(x.dtype)


def diff_weights(s, lam):
    p = jax.nn.softmax(s, axis=-1)
    return p[:, 0] - lam * p[:, 1]


def diff_attn_prompt(q, k, v, lam, table):
    bsz, seq = q.shape[:2]
    nb = seq // ATT_BLOCK
    kpos = jnp.arange(seq)
    qblocks = jnp.moveaxis(q.reshape(bsz, nb, ATT_BLOCK, H_A, 2, DH_A), 1, 0)

    def one_block(args):
        qi, bi = args
        qpos = bi * ATT_BLOCK + jnp.arange(ATT_BLOCK)
        s = jnp.einsum('bqhcd,bkhcd->bchqk', qi, k).astype(jnp.float32) * ATT_SCALE
        s = s + rel_bias(table, qpos, kpos)[None, None]
        s = jnp.where((kpos[None, :] <= qpos[:, None])[None, None, None], s, NEG_INF)
        a = diff_weights(s, lam)
        return jnp.einsum('bhqk,bkhe->bqhe', a.astype(v.dtype), v)

    o = lax.map(one_block, (qblocks, jnp.arange(nb)))
    return jnp.moveaxis(o, 0, 1).reshape(bsz, seq, H_A, DV_A)


def diff_attn_sample(q, k, v, k_past, v_past, lam, table):
    L = q.shape[1]
    P = k_past.shape[1]
    qpos = P + jnp.arange(L)
    s_past = jnp.einsum('bqhcd,bkhcd->bchqk', q, k_past).astype(jnp.float32) * ATT_SCALE
    s_past = s_past + rel_bias(table, qpos, jnp.arange(P))[None, None]
    s_new = jnp.einsum('bqhcd,bkhcd->bchqk', q, k).astype(jnp.float32) * ATT_SCALE
    s_new = s_new + rel_bias(table, qpos, qpos)[None, None]
    s_new = jnp.where((qpos[None, :] <= qpos[:, None])[None, None, None], s_new, NEG_INF)
    a = diff_weights(jnp.concatenate([s_past, s_new], axis=-1), lam).astype(v.dtype)
    return (jnp.einsum('bhqk,bkhe->bqhe', a[..., :P], v_past)
            + jnp.einsum('bhqk,bkhe->bqhe', a[..., P:], v))


def retention(q, k, v, s0):
    bsz, L, H, _ = q.shape
    dv = v.shape[-1]
    C = RET_CHUNK if L % RET_CHUNK == 0 else L
    n = L // C
    lg = jnp.log(1.0 - 2.0 ** (-5.0 - jnp.arange(H, dtype=jnp.float32)))
    i = jnp.arange(C, dtype=jnp.float32)
    diff = i[:, None] - i[None, :]
    dmat = jnp.where(diff >= 0, jnp.exp(jnp.maximum(diff, 0.0)[None] * lg[:, None, None]), 0.0)
    xi = jnp.exp((i[:, None] + 1.0) * lg[None, :])
    zeta = jnp.exp((C - 1.0 - i)[:, None] * lg[None, :])
    cdec = jnp.exp(C * lg)

    def to_chunks(t):
        return jnp.moveaxis(t.astype(jnp.float32).reshape(bsz, n, C, H, t.shape[-1]), 1, 0)

    def step(s, inp):
        qc, kc, vc = inp
        a = jnp.einsum('bihd,bjhd->bhij', qc, kc) * dmat
        o = (jnp.einsum('bhij,bjhe->bihe', a, vc)
             + jnp.einsum('bihd,bhde->bihe', qc, s) * xi[None, :, :, None])
        s = s * cdec[None, :, None, None] + jnp.einsum('bjhd,bjhe->bhde', kc * zeta[None, :, :, None], vc)
        return s, o

    s, o = lax.scan(step, s0.astype(jnp.float32), (to_chunks(q), to_chunks(k), to_chunks(v)))
    o = jnp.moveaxis(o, 0, 1).reshape(bsz, L, H, dv)
    return o.astype(v.dtype), s


def pool_mixer(xc, buf, pos0, pw, pscale):
    bsz, L, _ = xc.shape
    P = buf.shape[1]
    xe = jnp.concatenate([buf.astype(xc.dtype), xc], axis=1)
    cs = jnp.concatenate([jnp.zeros((bsz, 1, W_C), jnp.float32), jnp.cumsum(xe.astype(jnp.float32), axis=1)], axis=1)
    j = P + jnp.arange(L)
    pos = pos0 + jnp.arange(L)
    parts = []
    for g, win in enumerate(POOL_WINDOWS):
        sl = slice(g * POOL_GC, (g + 1) * POOL_GC)
        lo = jnp.maximum(j - win + 1, 0)
        cnt = jnp.minimum(pos + 1, win).astype(jnp.float32)
        parts.append((cs[:, j + 1, sl] - cs[:, lo, sl]) / cnt[None, :, None])
    pooled = (jnp.concatenate(parts, axis=-1) - xc.astype(jnp.float32)).astype(xc.dtype)
    y = jnp.einsum('blgc,gcd->blgd', pooled.reshape(bsz, L, N_POOL_GROUPS, POOL_GC), pw).reshape(bsz, L, W_C) * pscale
    return y, xe[:, -POOL_BUF:]


def trunk_layer(x, l, pos0, past_kv, ret_s0, pool_buf, conv_buf, w):
    bsz, L, _ = x.shape
    pos = pos0 + jnp.arange(L)
    h = rmsnorm(x, w['norm_mix_pre'][l])
    qa, ka, va, qb, kb, vb, gb, xc = split_cols(h @ w['w_in'][l])
    qa = qa.reshape(bsz, L, H_A, 2, DH_A)
    ka = ka.reshape(bsz, L, H_A, 2, DH_A)
    va = va.reshape(bsz, L, H_A, DV_A)
    lam_init = 0.8 - 0.6 * math.exp(-0.3 * l)
    lam = (jnp.exp(jnp.sum(w['lambda_q1'][l].astype(jnp.float32) * w['lambda_k1'][l].astype(jnp.float32)))
           - jnp.exp(jnp.sum(w['lambda_q2'][l].astype(jnp.float32) * w['lambda_k2'][l].astype(jnp.float32)))
           + lam_init)
    if past_kv is None:
        oa = diff_attn_prompt(qa, ka, va, lam, w['rel_bias_table'])
    else:
        oa = diff_attn_sample(qa, ka, va, past_kv[0], past_kv[1], lam, w['rel_bias_table'])
    oa = (rmsnorm(oa, w['subln_a'][l]) * (1.0 - lam_init)).reshape(bsz, L, W_A)
    new_k = ka.reshape(bsz, L, H_A, DV_A)
    qb = rope(qb.reshape(bsz, L, H_B, DK_B), pos)
    kb = rope(kb.reshape(bsz, L, H_B, DK_B), pos) * (DK_B ** -0.5)
    ob, ret_s = retention(qb, kb, vb.reshape(bsz, L, H_B, DV_B), ret_s0)
    ob = rmsnorm(ob).reshape(bsz, L, W_B) * jax.nn.silu(gb)
    oc, pool_new = pool_mixer(xc, pool_buf, pos0, w['pool_w'][l], w['pool_scale'][l])
    mix = jnp.concatenate([oa, ob, oc], axis=-1) @ w['w_out'][l]
    x = x + rmsnorm(mix, w['norm_mix_post'][l])
    h = rmsnorm(x, w['norm_ffn_pre'][l])
    u = h @ w['ffn_w_up'][l]
    ue = jnp.concatenate([conv_buf.astype(u.dtype), u], axis=1)
    cw = w['ffn_conv_w'][l]
    c = ue[:, 0:L] * cw[0]
    for t in range(1, CONV_W):
        c = c + ue[:, t:t + L] * cw[t]
    c = c + w['ffn_conv_b'][l]
    a, g = c[..., :D_FF], c[..., D_FF:]
    f = (a * jax.nn.gelu(g)) @ w['ffn_w_down'][l]
    x = x + rmsnorm(f, w['norm_ffn_post'][l])
    return x, (new_k, va, ret_s.astype(x.dtype), pool_new, ue[:, -(CONV_W - 1):])


def setup_inputs(seed: int = 0) -> dict:
    key = jax.random.key(seed)
    ks = jax.random.split(key, 32)
    f32 = jnp.float32
    n_pages = PAST_LEN // PAGE_SIZE
    n_pool = (DEC_BATCH * n_pages * 5) // 4

    def nrm(k, shape, scale):
        return jax.random.normal(k, shape, f32) * scale

    def gain(k, shape):
        return 1.0 + 0.05 * jax.random.normal(k, shape, f32)

    perm = jax.random.permutation(ks[7], n_pool)
    page_table = perm[:DEC_BATCH * n_pages].reshape(DEC_BATCH, n_pages).astype(jnp.int32)
    return {
        'x_prompt': nrm(ks[0], (BATCH, SEQ, D_MODEL), 1.0),
        'x_sample': nrm(ks[1], (DEC_BATCH, DEC_SEQ, D_MODEL), 1.0),
        'cache_k': nrm(ks[2], (DEPTH, n_pool, PAGE_SIZE, H_A, DV_A), 1.0),
        'cache_v': nrm(ks[3], (DEPTH, n_pool, PAGE_SIZE, H_A, DV_A), 1.0),
        'state_ret': nrm(ks[4], (DEPTH, DEC_BATCH, H_B, DK_B, DV_B), 0.5),
        'state_pool': nrm(ks[5], (DEPTH, DEC_BATCH, POOL_BUF, W_C), 1.0),
        'state_conv': nrm(ks[6], (DEPTH, DEC_BATCH, CONV_W - 1, 2 * D_FF), 1.0),
        'page_table': page_table,
        'rel_bias_table': nrm(ks[8], (REL_BUCKETS, H_A), 0.5),
        'norm_mix_pre': gain(ks[9], (DEPTH, D_MODEL)),
        'w_in': nrm(ks[10], (DEPTH, D_MODEL, IN_COLS), D_MODEL ** -0.5),
        'lambda_q1': nrm(ks[11], (DEPTH, DH_A), 0.1),
        'lambda_k1': nrm(ks[12], (DEPTH, DH_A), 0.1),
        'lambda_q2': nrm(ks[13], (DEPTH, DH_A), 0.1),
        'lambda_k2': nrm(ks[14], (DEPTH, DH_A), 0.1),
        'subln_a': gain(ks[15], (DEPTH, DV_A)),
        'pool_w': nrm(ks[16], (DEPTH, N_POOL_GROUPS, POOL_GC, POOL_GC), POOL_GC ** -0.5),
        'pool_scale': gain(ks[17], (DEPTH, W_C)),
        'w_out': nrm(ks[18], (DEPTH, MIX_W, D_MODEL), MIX_W ** -0.5),
        'norm_mix_post': gain(ks[19], (DEPTH, D_MODEL)),
        'norm_ffn_pre': gain(ks[20], (DEPTH, D_MODEL)),
        'ffn_w_up': nrm(ks[21], (DEPTH, D_MODEL, 2 * D_FF), D_MODEL ** -0.5),
        'ffn_conv_w': nrm(ks[22], (DEPTH, CONV_W, 2 * D_FF), 0.5),
        'ffn_conv_b': nrm(ks[23], (DEPTH, 2 * D_FF), 0.01),
        'ffn_w_down': nrm(ks[24], (DEPTH, D_FF, D_MODEL), D_FF ** -0.5),
        'norm_ffn_post': gain(ks[25], (DEPTH, D_MODEL)),
    }


def reference(x_prompt, x_sample, cache_k, cache_v, state_ret, state_pool, state_conv, page_table,
              rel_bias_table, norm_mix_pre, w_in, lambda_q1, lambda_k1, lambda_q2, lambda_k2, subln_a,
              pool_w, pool_scale, w_out, norm_mix_post, norm_ffn_pre, ffn_w_up, ffn_conv_w, ffn_conv_b,
              ffn_w_down, norm_ffn_post):
    w = {'rel_bias_table': rel_bias_table, 'norm_mix_pre': norm_mix_pre, 'w_in': w_in,
         'lambda_q1': lambda_q1, 'lambda_k1': lambda_k1, 'lambda_q2': lambda_q2, 'lambda_k2': lambda_k2,
         'subln_a': subln_a, 'pool_w': pool_w, 'pool_scale': pool_scale, 'w_out': w_out,
         'norm_mix_post': norm_mix_post, 'norm_ffn_pre': norm_ffn_pre, 'ffn_w_up': ffn_w_up,
         'ffn_conv_w': ffn_conv_w, 'ffn_conv_b': ffn_conv_b, 'ffn_w_down': ffn_w_down,
         'norm_ffn_post': norm_ffn_post}
    bp = x_prompt.shape[0]
    bs = x_sample.shape[0]
    past_len = page_table.shape[1] * cache_k.shape[2]
    xp, xs = x_prompt, x_sample
    kp_l, vp_l, rp_l, pp_l, cp_l = [], [], [], [], []
    ks_l, vs_l, rs_l, ps_l, cs_l = [], [], [], [], []
    for l in range(DEPTH):
        xp, (k_new, v_new, r_new, p_new, c_new) = trunk_layer(
            xp, l, 0, None,
            jnp.zeros((bp, H_B, DK_B, DV_B), jnp.float32),
            jnp.zeros((bp, 0, W_C), xp.dtype),
            jnp.zeros((bp, CONV_W - 1, 2 * D_FF), xp.dtype), w)
        kp_l.append(k_new); vp_l.append(v_new); rp_l.append(r_new); pp_l.append(p_new); cp_l.append(c_new)
        k_past = cache_k[l, page_table].reshape(bs, past_len, H_A, 2, DH_A)
        v_past = cache_v[l, page_table].reshape(bs, past_len, H_A, DV_A)
        xs, (k_new, v_new, r_new, p_new, c_new) = trunk_layer(
            xs, l, past_len, (k_past, v_past), state_ret[l], state_pool[l], state_conv[l], w)
        ks_l.append(k_new); vs_l.append(v_new); rs_l.append(r_new); ps_l.append(p_new); cs_l.append(c_new)
    return (xp, xs,
            jnp.stack(kp_l), jnp.stack(vp_l), jnp.stack(ks_l), jnp.stack(vs_l),
            jnp.stack(rp_l), jnp.stack(rs_l), jnp.stack(pp_l), jnp.stack(ps_l),
            jnp.stack(cp_l), jnp.stack(cs_l))
```

```python
import functools
import math

import numpy as np
import jax
import jax.numpy as jnp
from jax import lax
from jax.experimental import pallas as pl
from jax.experimental.pallas import tpu as pltpu

F32 = jnp.float32
BF16 = jnp.bfloat16

H_A = 6
DH_A = 64
DV_A = 128
W_A = H_A * DV_A
ATT_SCALE = DH_A ** -0.5
NEG_INF = -1e30
H_B = 6
DK_B = 64
DV_B = 128
W_B = H_B * DV_B
ROPE_BASE = 10000.0
N_POOL_GROUPS = 4
POOL_GC = 128
POOL_WINDOWS = (2, 4, 8, 16)
W_C = N_POOL_GROUPS * POOL_GC
POOL_BUF = 15
POOL_HIST = 16
REL_BUCKETS = 32
REL_MAX_DIST = 128
CONV_W = 3
EPS = 1e-6
COL_QA, COL_KA, COL_VA = 0, W_A, 2 * W_A
COL_QB = 3 * W_A
COL_KB = COL_QB + H_B * DK_B
COL_VB = COL_KB + H_B * DK_B
COL_GB = COL_VB + W_B
COL_XC = COL_GB + W_B

LANES = 128
VMEM_LIMIT = 56 * 1024 * 1024


def _params(sem, vmem=VMEM_LIMIT):
    return pltpu.CompilerParams(dimension_semantics=sem, vmem_limit_bytes=vmem)


def _rms(x, g=None):
    y = x * lax.rsqrt(jnp.mean(x * x, axis=-1, keepdims=True) + EPS)
    return y if g is None else y * g


def _pick(n, pref):
    if n <= pref:
        return n
    t = pref
    while n % t:
        t //= 2
    return t


def _norm_kernel(x_ref, g_ref, o_ref):
    o_ref[...] = _rms(x_ref[...], g_ref[...]).astype(o_ref.dtype)


def _norm_call(x, g):
    m, d = x.shape
    tm = _pick(m, 512)
    return pl.pallas_call(
        _norm_kernel,
        grid=(m // tm,),
        in_specs=[pl.BlockSpec((tm, d), lambda i: (i, 0)),
                  pl.BlockSpec((1, d), lambda i: (0, 0))],
        out_specs=pl.BlockSpec((tm, d), lambda i: (i, 0)),
        out_shape=jax.ShapeDtypeStruct((m, d), BF16),
        compiler_params=_params(("arbitrary",)),
        name="rmsnorm",
    )(x, g.reshape(1, d))


def _matmul_kernel(x_ref, w_ref, o_ref):
    o_ref[...] = jnp.dot(x_ref[...], w_ref[...], preferred_element_type=F32)


def _matmul_call(x, w):
    m, k = x.shape
    n = w.shape[1]
    tm = _pick(m, 1024)
    tn = _pick(n, 1024)
    return pl.pallas_call(
        _matmul_kernel,
        grid=(n // tn, m // tm),
        in_specs=[pl.BlockSpec((tm, k), lambda j, i: (i, 0)),
                  pl.BlockSpec((k, tn), lambda j, i: (0, j))],
        out_specs=pl.BlockSpec((tm, tn), lambda j, i: (i, j)),
        out_shape=jax.ShapeDtypeStruct((m, n), F32),
        compiler_params=_params(("arbitrary", "arbitrary")),
        name="in_proj",
    )(x, w)


def _t5_bucket(dist):
    n = jnp.maximum(dist, 0)
    exact = REL_BUCKETS // 2
    nf = jnp.maximum(n, 1).astype(F32)
    large = exact + (jnp.log(nf / exact) / math.log(REL_MAX_DIST / exact)
                     * (REL_BUCKETS - exact)).astype(jnp.int32)
    return jnp.where(n < exact, n, jnp.minimum(large, REL_BUCKETS - 1))


def _far_bucket_is_constant(min_dist):
    exact = REL_BUCKETS // 2
    v = np.float32(min_dist) / np.float32(exact)
    large = exact + int(np.float32(np.log(v)) / np.float32(math.log(REL_MAX_DIST / exact))
                        * (REL_BUCKETS - exact))
    return min_dist >= exact and large >= REL_BUCKETS


def _attn_prompt_kernel(sc_ref, q_ref, k_ref, v_ref, d0_ref, d1_ref, g_ref, o_ref,
                        kb, vb, acc, mrow, *, tile, out_scale):
    h = pl.program_id(1)
    qi = pl.program_id(2)

    @pl.when(qi == 0)
    def _():
        kb[...] = k_ref[...].astype(BF16)
        vb[:, :DV_A] = v_ref[...].astype(BF16)
        vb[:, DV_A:] = jnp.ones((vb.shape[0], DV_A), BF16)

    q = q_ref[...] * ATT_SCALE
    lane = lax.broadcasted_iota(jnp.int32, q.shape, 1)
    qh = (jnp.where(lane < DH_A, q, 0.0).astype(BF16), jnp.where(lane >= DH_A, q, 0.0).astype(BF16))
    acc[...] = jnp.zeros(acc.shape, F32)
    mrow[...] = jnp.full(mrow.shape, NEG_INF, F32)
    cfar = sc_ref[1 + h]

    def step(kj, bias):
        start = pl.multiple_of(kj * tile, tile)
        ks = kb[pl.ds(start, tile), :]
        vs = vb[pl.ds(start, tile), :]
        for c in range(2):
            s = lax.dot_general(qh[c], ks, (((1,), (1,)), ((), ())), preferred_element_type=F32)
            if bias is None:
                m_cur = jnp.max(s, axis=-1, keepdims=True) + cfar
            else:
                s = s + bias
                m_cur = jnp.max(s, axis=-1, keepdims=True)
            m_prev = mrow[c]
            m_new = jnp.maximum(m_prev, m_cur)
            alpha = jnp.exp(m_prev - m_new)
            shift = m_new - cfar if bias is None else m_new
            p = jnp.exp(s - shift).astype(BF16)
            acc[c] = alpha * acc[c] + jnp.dot(p, vs, preferred_element_type=F32)
            mrow[c] = m_new

    def far_body(kj, carry):
        step(kj, None)
        return carry

    lax.fori_loop(0, qi - 1, far_body, 0)

    @pl.when(qi >= 1)
    def _():
        step(qi - 1, d1_ref[...])

    step(qi, d0_ref[...])

    lam = sc_ref[0]
    a1 = acc[0]
    a2 = acc[1]
    o = a1[:, :DV_A] / a1[:, DV_A:] - lam * (a2[:, :DV_A] / a2[:, DV_A:])
    o_ref[...] = (_rms(o, g_ref[...]) * out_scale).astype(o_ref.dtype)


def _attn_prompt_call(proj, bsz, seq, table, lam, subln, out_scale):
    tile = _pick(seq, 512)
    nq = seq // tile
    assert tile % LANES == 0 and _far_bucket_is_constant(tile + 1)
    dist = jnp.arange(2 * tile)
    bv = table[_t5_bucket(dist)].astype(F32).T
    r = jnp.arange(tile)[:, None]
    c = jnp.arange(tile)[None, :]
    d0 = jnp.where((r >= c)[None], bv[:, jnp.maximum(r - c, 0)], NEG_INF)
    d1 = bv[:, tile + r - c]
    scal = jnp.concatenate([jnp.reshape(lam, (1,)).astype(F32),
                            table[REL_BUCKETS - 1].astype(F32)])
    kq, kk, kv = COL_QA // DV_A, COL_KA // DV_A, COL_VA // DV_A
    kern = functools.partial(_attn_prompt_kernel, tile=tile, out_scale=out_scale)
    return pl.pallas_call(
        kern,
        grid=(bsz, H_A, nq),
        in_specs=[
            pl.BlockSpec(memory_space=pltpu.SMEM),
            pl.BlockSpec((tile, DV_A), lambda b, h, i: (b * nq + i, kq + h)),
            pl.BlockSpec((seq, DV_A), lambda b, h, i: (b, kk + h)),
            pl.BlockSpec((seq, DV_A), lambda b, h, i: (b, kv + h)),
            pl.BlockSpec((None, tile, tile), lambda b, h, i: (h, 0, 0)),
            pl.BlockSpec((None, tile, tile), lambda b, h, i: (h, 0, 0)),
            pl.BlockSpec((1, DV_A), lambda b, h, i: (0, 0)),
        ],
        out_specs=pl.BlockSpec((tile, DV_A), lambda b, h, i: (b * nq + i, h)),
        out_shape=jax.ShapeDtypeStruct((bsz * seq, W_A), BF16),
        scratch_shapes=[pltpu.VMEM((seq, DV_A), BF16),
                        pltpu.VMEM((seq, 2 * DV_A), BF16),
                        pltpu.VMEM((2, tile, 2 * DV_A), F32),
                        pltpu.VMEM((2, tile, 1), F32)],
        compiler_params=_params(("arbitrary", "arbitrary", "arbitrary")),
        name="attn_prompt",
    )(scal, proj, proj, proj, d0, d1, subln.reshape(1, DV_A).astype(F32))


def _attn_sample_kernel(pt_ref, sc_ref, qbd_ref, kn_ref, vn_ref, bl_ref, bf_ref, bn_ref, g_ref,
                        *rest, group, n_ksteps, page, lq, out_scale):
    k_refs = rest[:group]
    v_refs = rest[group:2 * group]
    o_ref = rest[2 * group]
    st, oacc, lsum, vnew = rest[2 * group + 1:]
    j = pl.program_id(1)
    past = n_ksteps * group * page
    chunk = group * page

    @pl.when(j < n_ksteps)
    def _():
        for r in range(group):
            row = pl.multiple_of((j * group + r) * page, page)
            s = jnp.zeros((page, LANES), F32)
            for h in range(H_A):
                s = s + jnp.dot(k_refs[r][h].astype(BF16), qbd_ref[h * DV_A:(h + 1) * DV_A, :],
                                preferred_element_type=F32)
            st[pl.ds(row, page), :] = s

    @pl.when(j == n_ksteps)
    def _():
        st[past:past + page, :] = jnp.full((page, LANES), NEG_INF, F32)
        st[past:past + lq, :] = jnp.dot(kn_ref[...].astype(BF16), qbd_ref[...],
                                        preferred_element_type=F32) + bn_ref[...]
        st[past - page:past, :] = st[past - page:past, :] + bl_ref[...]
        bfar = bf_ref[...]
        n_far = (past - page) // page

        def max_body(i, m):
            row = pl.multiple_of(i * page, page)
            return jnp.maximum(m, jnp.max(st[pl.ds(row, page), :], axis=0, keepdims=True))

        m_far = lax.fori_loop(0, n_far, max_body, jnp.full((1, LANES), NEG_INF, F32))
        m_near = jnp.max(st[past - page:past + page, :], axis=0, keepdims=True)
        m = jnp.maximum(m_far + bfar, m_near)

        def exp_body(i, l):
            row = pl.multiple_of(i * page, page)
            p = jnp.exp(st[pl.ds(row, page), :] - (m - bfar))
            st[pl.ds(row, page), :] = p
            return l + jnp.sum(p, axis=0, keepdims=True)

        l = lax.fori_loop(0, n_far, exp_body, jnp.zeros((1, LANES), F32))
        p = jnp.exp(st[past - page:past + page, :] - m)
        st[past - page:past + page, :] = p
        lsum[...] = l + jnp.sum(p, axis=0, keepdims=True)
        oacc[...] = jnp.zeros(oacc.shape, F32)
        vnew[...] = jnp.zeros(vnew.shape, BF16)
        vnew[0:lq, :] = vn_ref[...].astype(BF16)

    @pl.when(j >= n_ksteps)
    def _():
        jj = j - n_ksteps
        tot = [oacc[h] for h in range(H_A)]
        for r in range(group):
            row = pl.multiple_of((jj * group + r) * page, page)
            pt_t = st[pl.ds(row, page), :].T.astype(BF16)
            for h in range(H_A):
                tot[h] = tot[h] + jnp.dot(pt_t, v_refs[r][h].astype(BF16), preferred_element_type=F32)
        for h in range(H_A):
            oacc[h] = tot[h]

    @pl.when(j == 2 * n_ksteps - 1)
    def _():
        pn_t = st[past:past + page, :].T.astype(BF16)
        rr = lax.broadcasted_iota(jnp.int32, (LANES, LANES), 0)
        cc = lax.broadcasted_iota(jnp.int32, (LANES, LANES), 1)
        l_rows = jnp.sum(jnp.where(rr == cc, jnp.broadcast_to(lsum[...], (LANES, LANES)), 0.0),
                         axis=1, keepdims=True)
        lam = sc_ref[0]
        half = LANES // 2
        for h in range(H_A):
            r1 = h * lq
            r2 = half + h * lq
            tot = oacc[h] + jnp.dot(pn_t, vnew[:, h * DV_A:(h + 1) * DV_A], preferred_element_type=F32)
            o1 = tot[r1:r1 + lq] / l_rows[r1:r1 + lq]
            o2 = tot[r2:r2 + lq] / l_rows[r2:r2 + lq]
            o = o1 - lam * o2
            o_ref[:, h * DV_A:(h + 1) * DV_A] = _rms(o, g_ref[...]) * out_scale


def _attn_sample_call(proj, cache_k, cache_v, layer, page_table, table, lam, subln, out_scale, lq):
    bs = page_table.shape[0]
    n_pages = page_table.shape[1]
    n_pool, page = cache_k.shape[1], cache_k.shape[2]
    assert page == LANES and lq == 8 and H_A * lq <= LANES // 2
    assert _far_bucket_is_constant(page + 1)
    group = _pick(n_pages, 8)
    n_ksteps = n_pages // group
    past = n_pages * page
    half = LANES // 2
    ck = jnp.transpose(cache_k, (0, 1, 3, 2, 4))
    cv = jnp.transpose(cache_v, (0, 1, 3, 2, 4))

    q = (proj[:, COL_QA:COL_QA + W_A] * ATT_SCALE).reshape(bs, lq, H_A, 2, DH_A)
    eye_h = jnp.eye(H_A, LANES // 2 // lq, dtype=F32)
    eye_c = jnp.eye(2, dtype=F32)
    qbd = jnp.einsum('bqhcd,hH,cC->bhcdCHq', q, eye_h, eye_c).reshape(bs, W_A, LANES).astype(BF16)

    lane = np.arange(LANES)
    lane_h = np.minimum((lane % half) // lq, H_A - 1)
    lane_q = lane % lq
    valid = jnp.asarray((lane % half) < H_A * lq)
    tab_t = table.astype(F32).T
    kk = np.arange(page)[:, None]
    b_last = jnp.where(valid[None], tab_t[lane_h[None], _t5_bucket(jnp.asarray(page + lane_q[None] - kk))], 0.0)
    jj = np.arange(lq)[:, None]
    dn = lane_q[None] - jj
    b_new = jnp.where(valid[None], tab_t[lane_h[None], _t5_bucket(jnp.asarray(np.maximum(dn, 0)))], 0.0)
    b_new = jnp.where(jnp.asarray(dn >= 0), b_new, NEG_INF)
    b_far = jnp.where(valid, tab_t[lane_h, REL_BUCKETS - 1], 0.0).reshape(1, LANES)
    scal = jnp.reshape(lam, (1,)).astype(F32)
    pt_flat = page_table.reshape(-1).astype(jnp.int32)

    def k_map(r):
        return lambda b, j, pt: (layer, pt[b * n_pages + jnp.minimum(j, n_ksteps - 1) * group + r], 0, 0, 0)

    def v_map(r):
        return lambda b, j, pt: (layer, pt[b * n_pages + jnp.maximum(j - n_ksteps, 0) * group + r], 0, 0, 0)

    kcol, vcol = COL_KA // W_A, COL_VA // W_A
    in_specs = [
        pl.BlockSpec(memory_space=pltpu.SMEM),
        pl.BlockSpec((None, W_A, LANES), lambda b, j, pt: (b, 0, 0)),
        pl.BlockSpec((lq, W_A), lambda b, j, pt: (b, kcol)),
        pl.BlockSpec((lq, W_A), lambda b, j, pt: (b, vcol)),
        pl.BlockSpec((page, LANES), lambda b, j, pt: (0, 0)),
        pl.BlockSpec((1, LANES), lambda b, j, pt: (0, 0)),
        pl.BlockSpec((lq, LANES), lambda b, j, pt: (0, 0)),
        pl.BlockSpec((1, DV_A), lambda b, j, pt: (0, 0)),
    ]
    in_specs += [pl.BlockSpec((None, None, H_A, page, DV_A), k_map(r)) for r in range(group)]
    in_specs += [pl.BlockSpec((None, None, H_A, page, DV_A), v_map(r)) for r in range(group)]
    kern = functools.partial(_attn_sample_kernel, group=group, n_ksteps=n_ksteps, page=page, lq=lq,
                             out_scale=out_scale)
    grid_spec = pltpu.PrefetchScalarGridSpec(
        num_scalar_prefetch=1,
        grid=(bs, 2 * n_ksteps),
        in_specs=in_specs,
        out_specs=pl.BlockSpec((None, lq, W_A), lambda b, j, pt: (b, 0, 0)),
        scratch_shapes=[pltpu.VMEM((past + page, LANES), F32),
                        pltpu.VMEM((H_A, LANES, DV_A), F32),
                        pltpu.VMEM((1, LANES), F32),
                        pltpu.VMEM((page, W_A), BF16)],
    )
    out = pl.pallas_call(
        kern,
        grid_spec=grid_spec,
        out_shape=jax.ShapeDtypeStruct((bs, lq, W_A), F32),
        compiler_params=_params(("arbitrary", "arbitrary")),
        name="attn_sample",
    )(pt_flat, scal, qbd, proj, proj, b_last, b_far, b_new, subln.reshape(1, DV_A).astype(F32),
      *([ck] * group), *([cv] * group))
    return out.reshape(bs * lq, W_A)


def _log_gamma():
    return np.log(1.0 - 2.0 ** (-5.0 - np.arange(H_B, dtype=np.float64)))


def _retention_constants(chunk_math, rows):
    lg = _log_gamma()
    i = np.arange(rows, dtype=np.float64)
    diff = i[:, None] - i[None, :]
    dmat = np.where(diff >= 0, np.exp(np.maximum(diff, 0.0)[None] * lg[:, None, None]), 0.0)
    xi = np.exp((i[:, None] + 1.0) * lg[None, :])
    zeta = np.exp((chunk_math - 1.0 - i)[:, None] * lg[None, :])
    zeta = np.where(i[:, None] < chunk_math, zeta, 0.0)
    cdec = np.exp(chunk_math * lg)
    xi_l = np.repeat(xi, DV_B, axis=1)
    zeta_l = np.repeat(zeta, DK_B, axis=1) * (DK_B ** -0.5)
    cdec_rows = np.repeat(cdec.reshape(H_B // 2, 2), DK_B, axis=1).reshape(H_B // 2, 2 * DK_B, 1)
    return (jnp.asarray(dmat, F32), jnp.asarray(xi_l, F32), jnp.asarray(zeta_l, F32),
            jnp.asarray(cdec_rows, F32))


def _rope_tables(pos):
    half = DK_B // 2
    inv = ROPE_BASE ** (-jnp.arange(half, dtype=F32) / half)
    ang = pos.astype(F32)[:, None] * inv[None, :]
    cos = jnp.cos(ang)
    sin = jnp.sin(ang)
    cos_l = jnp.tile(jnp.concatenate([cos, cos], axis=1), (1, H_B))
    sin_l = jnp.tile(jnp.concatenate([-sin, sin], axis=1), (1, H_B))
    return cos_l, sin_l


def _mixer_kernel(qb_ref, kb_ref, vb_ref, gb_ref, xc_ref, cos_ref, sin_ref, dmat_ref, xi_ref, zeta_ref,
                  cdec_ref, s0_ref, hist_ref, pw_ref, ps_ref, o_ref, sout_ref, state, ext,
                  *, rows_in, rows, pos0):
    ci = pl.program_id(1)
    n_chunks = pl.num_programs(1)

    @pl.when(ci == 0)
    def _():
        state[...] = s0_ref[...]
        ext[0:POOL_HIST, :] = hist_ref[...]

    def pad_rows(x):
        if rows == rows_in:
            return x
        return jnp.concatenate([x, jnp.zeros((rows - rows_in, x.shape[1]), x.dtype)], axis=0)

    lane = lax.broadcasted_iota(jnp.int32, (rows, LANES), 1)
    first_half = (lane % DK_B) < (DK_B // 2)
    low_head = lane < DK_B

    def rope(x, cos, sin):
        swapped = jnp.where(first_half, pltpu.roll(x, LANES - DK_B // 2, 1), pltpu.roll(x, DK_B // 2, 1))
        return x * cos + swapped * sin

    cos = pad_rows(cos_ref[...])
    sin = pad_rows(sin_ref[...])
    qb = pad_rows(qb_ref[...])
    kb = pad_rows(kb_ref[...])
    vb = pad_rows(vb_ref[...])
    gb = pad_rows(gb_ref[...])
    zeta = zeta_ref[...]
    xi = xi_ref[...]
    for pr in range(H_B // 2):
        sl = slice(pr * LANES, (pr + 1) * LANES)
        q2 = rope(qb[:, sl], cos[:, sl], sin[:, sl])
        k2 = rope(kb[:, sl], cos[:, sl], sin[:, sl])
        kz2 = k2 * zeta[:, sl]
        k2 = k2 * (DK_B ** -0.5)
        s_pair = state[pr]
        s_pair_b = s_pair.astype(BF16)
        upd = jnp.zeros((LANES, DV_B), F32)
        k2b = k2.astype(BF16)
        for hh in range(2):
            h = 2 * pr + hh
            mask = low_head if hh == 0 else jnp.logical_not(low_head)
            qm = jnp.where(mask, q2, 0.0).astype(BF16)
            vh = vb[:, h * DV_B:(h + 1) * DV_B].astype(BF16)
            a = lax.dot_general(qm, k2b, (((1,), (1,)), ((), ())), preferred_element_type=F32)
            a = (a * dmat_ref[h]).astype(BF16)
            o = jnp.dot(a, vh, preferred_element_type=F32)
            o = o + jnp.dot(qm, s_pair_b, preferred_element_type=F32) * xi[:, h * DV_B:(h + 1) * DV_B]
            kzm_t = jnp.where(mask, kz2, 0.0).T.astype(BF16)
            upd = upd + jnp.dot(kzm_t, vh, preferred_element_type=F32)
            g = gb[:, h * DV_B:(h + 1) * DV_B]
            y = _rms(o) * (g / (1.0 + jnp.exp(-g)))
            o_ref[:, h * DV_B:(h + 1) * DV_B] = y[0:rows_in].astype(o_ref.dtype)
        state[pr] = s_pair * cdec_ref[pr] + upd

    @pl.when(ci == n_chunks - 1)
    def _():
        for pr in range(H_B // 2):
            sp = state[pr]
            sout_ref[2 * pr] = sp[0:DK_B]
            sout_ref[2 * pr + 1] = sp[DK_B:]

    xc = xc_ref[...]
    ext[POOL_HIST:POOL_HIST + rows_in, :] = xc
    pos = pos0 + ci * rows_in + lax.broadcasted_iota(jnp.int32, (rows_in, 1), 0)
    for gi, win in enumerate(POOL_WINDOWS):
        sl = slice(gi * POOL_GC, (gi + 1) * POOL_GC)
        tot = xc[:, sl]
        for d in range(1, win):
            tot = tot + ext[POOL_HIST - d:POOL_HIST - d + rows_in, sl]
        cnt = jnp.minimum(pos + 1, win).astype(F32)
        pooled = (tot / cnt - xc[:, sl]).astype(BF16)
        y = jnp.dot(pooled, pw_ref[gi], preferred_element_type=F32) * ps_ref[:, sl]
        o_ref[:, W_B + gi * POOL_GC:W_B + (gi + 1) * POOL_GC] = y.astype(o_ref.dtype)
    ext[0:POOL_HIST, :] = ext[rows_in:rows_in + POOL_HIST, :]


def _mixer_call(proj, bsz, seq, pos0, s0, hist, pool_w, pool_scale, out_dtype):
    rows_in = _pick(seq, 256)
    rows = max(rows_in, LANES)
    n_chunks = seq // rows_in
    assert rows_in % 8 == 0 and (rows_in == rows or n_chunks == 1)
    dmat, xi_l, zeta_l, cdec_rows = _retention_constants(rows_in, rows)
    cos_l, sin_l = _rope_tables(pos0 + jnp.arange(seq))
    s0p = s0.astype(F32).reshape(bsz, H_B // 2, 2 * DK_B, DV_B)
    wq, wv, wc = H_B * DK_B, W_B, W_C
    kern = functools.partial(_mixer_kernel, rows_in=rows_in, rows=rows, pos0=pos0)
    row_map = lambda col: (lambda b, c: (b * n_chunks + c, col))
    const2 = lambda b, c: (0, 0)
    const3 = lambda b, c: (0, 0, 0)
    out, sout = pl.pallas_call(
        kern,
        grid=(bsz, n_chunks),
        in_specs=[
            pl.BlockSpec((rows_in, wq), row_map(COL_QB // wq)),
            pl.BlockSpec((rows_in, wq), row_map(COL_KB // wq)),
            pl.BlockSpec((rows_in, wv), row_map(COL_VB // wv)),
            pl.BlockSpec((rows_in, wv), row_map(COL_GB // wv)),
            pl.BlockSpec((rows_in, wc), row_map(COL_XC // wc)),
            pl.BlockSpec((rows_in, wq), lambda b, c: (c, 0)),
            pl.BlockSpec((rows_in, wq), lambda b, c: (c, 0)),
            pl.BlockSpec((H_B, rows, rows), const3),
            pl.BlockSpec((rows, W_B), const2),
            pl.BlockSpec((rows, wq), const2),
            pl.BlockSpec((H_B // 2, 2 * DK_B, 1), const3),
            pl.BlockSpec((None, H_B // 2, 2 * DK_B, DV_B), lambda b, c: (b, 0, 0, 0)),
            pl.BlockSpec((None, POOL_HIST, W_C), lambda b, c: (b, 0, 0)),
            pl.BlockSpec((N_POOL_GROUPS, POOL_GC, POOL_GC), const3),
            pl.BlockSpec((1, W_C), const2),
        ],
        out_specs=[pl.BlockSpec((rows_in, W_B + W_C), lambda b, c: (b * n_chunks + c, 0)),
                   pl.BlockSpec((None, H_B, DK_B, DV_B), lambda b, c: (b, 0, 0, 0))],
        out_shape=[jax.ShapeDtypeStruct((bsz * seq, W_B + W_C), out_dtype),
                   jax.ShapeDtypeStruct((bsz, H_B, DK_B, DV_B), F32)],
        scratch_shapes=[pltpu.VMEM((H_B // 2, 2 * DK_B, DV_B), F32),
                        pltpu.VMEM((POOL_HIST + rows_in, W_C), F32)],
        compiler_params=_params(("arbitrary", "arbitrary")),
        name="mixer_ret_pool",
    )(proj, proj, proj, proj, proj, cos_l, sin_l, dmat, xi_l, zeta_l, cdec_rows, s0p, hist,
      pool_w.astype(BF16), pool_scale.reshape(1, W_C).astype(F32))
    return out, sout


def _outproj_kernel(oa_ref, bc_ref, w_ref, x_ref, gpost_ref, gnext_ref, x1_ref, h_ref):
    wa = oa_ref.shape[1]
    mix = jnp.dot(oa_ref[...].astype(BF16), w_ref[0:wa, :], preferred_element_type=F32)
    mix = mix + jnp.dot(bc_ref[...].astype(BF16), w_ref[wa:, :], preferred_element_type=F32)
    x1 = x_ref[...] + _rms(mix, gpost_ref[...])
    x1_ref[...] = x1
    h_ref[...] = _rms(x1, gnext_ref[...]).astype(h_ref.dtype)


def _outproj_call(oa, bc, w, x, g_post, g_next):
    m, d = x.shape
    tm = _pick(m, 512)
    row = lambda i: (i, 0)
    const = lambda i: (0, 0)
    return pl.pallas_call(
        _outproj_kernel,
        grid=(m // tm,),
        in_specs=[pl.BlockSpec((tm, oa.shape[1]), row),
                  pl.BlockSpec((tm, bc.shape[1]), row),
                  pl.BlockSpec(w.shape, const),
                  pl.BlockSpec((tm, d), row),
                  pl.BlockSpec((1, d), const),
                  pl.BlockSpec((1, d), const)],
        out_specs=[pl.BlockSpec((tm, d), row), pl.BlockSpec((tm, d), row)],
        out_shape=[jax.ShapeDtypeStruct((m, d), F32), jax.ShapeDtypeStruct((m, d), BF16)],
        compiler_params=_params(("arbitrary",)),
        name="out_proj",
    )(oa, bc, w, x, g_post.reshape(1, d).astype(F32), g_next.reshape(1, d).astype(F32))


def _gelu_tanh(g):
    return 0.5 * g * (1.0 + jnp.tanh(math.sqrt(2.0 / math.pi) * (g + 0.044715 * (g * g * g))))


def _ffn_up_prompt_kernel(h_ref, wa_ref, wg_ref, cwa_ref, cwg_ref, cba_ref, cbg_ref, bufa_ref, bufg_ref,
                          act_ref, lasta_ref, lastg_ref, ua, ug, *, tm, tiles_per_seq):
    i = pl.program_id(1)
    pad = 8
    hh = h_ref[...]

    def branch(w_ref, cw_ref, cb_ref, buf_ref, last_ref, u):
        @pl.when(i % tiles_per_seq == 0)
        def _():
            u[pad - (CONV_W - 1):pad, :] = buf_ref[...]

        u[pad:pad + tm, :] = jnp.dot(hh, w_ref[...], preferred_element_type=F32)
        c = u[pad - 2:pad - 2 + tm, :] * cw_ref[0:1, :]
        c = c + u[pad - 1:pad - 1 + tm, :] * cw_ref[1:2, :]
        c = c + u[pad:pad + tm, :] * cw_ref[2:3, :]
        c = c + cb_ref[...]
        tail = u[pad + tm - (CONV_W - 1):pad + tm, :]
        u[pad - (CONV_W - 1):pad, :] = tail
        last_ref[...] = tail
        return c

    a = branch(wa_ref, cwa_ref, cba_ref, bufa_ref, lasta_ref, ua)
    g = branch(wg_ref, cwg_ref, cbg_ref, bufg_ref, lastg_ref, ug)
    act_ref[...] = (a * _gelu_tanh(g)).astype(act_ref.dtype)


def _ffn_up_prompt_call(h, w_up, conv_w, conv_b, conv_buf, bsz, seq):
    m, d = h.shape
    dff = w_up.shape[1] // 2
    tm = _pick(seq, 1024)
    tn = _pick(dff, 512)
    nj = dff // tn
    tiles_per_seq = seq // tm
    kern = functools.partial(_ffn_up_prompt_kernel, tm=tm, tiles_per_seq=tiles_per_seq)
    cb = conv_b.reshape(1, 2 * dff).astype(F32)
    seq_map_a = lambda j, i: (i // tiles_per_seq, 0, j)
    seq_map_g = lambda j, i: (i // tiles_per_seq, 0, j + nj)
    act, last_a, last_g = pl.pallas_call(
        kern,
        grid=(nj, m // tm),
        in_specs=[pl.BlockSpec((tm, d), lambda j, i: (i, 0)),
                  pl.BlockSpec((d, tn), lambda j, i: (0, j)),
                  pl.BlockSpec((d, tn), lambda j, i: (0, j + nj)),
                  pl.BlockSpec((CONV_W, tn), lambda j, i: (0, j)),
                  pl.BlockSpec((CONV_W, tn), lambda j, i: (0, j + nj)),
                  pl.BlockSpec((1, tn), lambda j, i: (0, j)),
                  pl.BlockSpec((1, tn), lambda j, i: (0, j + nj)),
                  pl.BlockSpec((None, CONV_W - 1, tn), seq_map_a),
                  pl.BlockSpec((None, CONV_W - 1, tn), seq_map_g)],
        out_specs=[pl.BlockSpec((tm, tn), lambda j, i: (i, j)),
                   pl.BlockSpec((None, CONV_W - 1, tn), lambda j, i: (i // tiles_per_seq, 0, j)),
                   pl.BlockSpec((None, CONV_W - 1, tn), lambda j, i: (i // tiles_per_seq, 0, j))],
        out_shape=[jax.ShapeDtypeStruct((m, dff), BF16),
                   jax.ShapeDtypeStruct((bsz, CONV_W - 1, dff), F32),
                   jax.ShapeDtypeStruct((bsz, CONV_W - 1, dff), F32)],
        scratch_shapes=[pltpu.VMEM((8 + tm, tn), F32), pltpu.VMEM((8 + tm, tn), F32)],
        compiler_params=_params(("arbitrary", "arbitrary")),
        name="ffn_up_prompt",
    )(h, w_up, w_up, conv_w, conv_w, cb, cb, conv_buf, conv_buf)
    return act, jnp.concatenate([last_a, last_g], axis=-1)


def _ffn_up_sample_kernel(h_ref, wa_ref, wg_ref, cwa_ref, cwg_ref, cba_ref, cbg_ref,
                          b0a_ref, b0g_ref, b1a_ref, b1g_ref, act_ref, ua_ref, ug_ref, sa, sg, *, m, lq):
    pad = 8
    hh = h_ref[...]
    t = lax.broadcasted_iota(jnp.int32, (m, 1), 0) % lq

    def branch(w_ref, cw_ref, cb_ref, b0_ref, b1_ref, u_ref, s):
        u = jnp.dot(hh, w_ref[...], preferred_element_type=F32)
        u_ref[...] = u
        s[0:pad, :] = jnp.zeros((pad, s.shape[1]), F32)
        s[pad:pad + m, :] = u
        um1 = jnp.where(t == 0, b1_ref[...], s[pad - 1:pad - 1 + m, :])
        um2 = jnp.where(t == 0, b0_ref[...], jnp.where(t == 1, b1_ref[...], s[pad - 2:pad - 2 + m, :]))
        return um2 * cw_ref[0:1, :] + um1 * cw_ref[1:2, :] + u * cw_ref[2:3, :] + cb_ref[...]

    a = branch(wa_ref, cwa_ref, cba_ref, b0a_ref, b1a_ref, ua_ref, sa)
    g = branch(wg_ref, cwg_ref, cbg_ref, b0g_ref, b1g_ref, ug_ref, sg)
    act_ref[...] = (a * _gelu_tanh(g)).astype(act_ref.dtype)


def _ffn_up_sample_call(h, w_up, conv_w, conv_b, conv_buf, bsz, lq):
    m, d = h.shape
    dff = w_up.shape[1] // 2
    tn = _pick(dff, 512)
    nj = dff // tn
    assert lq >= CONV_W - 1
    kern = functools.partial(_ffn_up_sample_kernel, m=m, lq=lq)
    cb = conv_b.reshape(1, 2 * dff).astype(F32)
    b0 = jnp.repeat(conv_buf[:, 0], lq, axis=0)
    b1 = jnp.repeat(conv_buf[:, 1], lq, axis=0)
    col_a = lambda j: (0, j)
    col_g = lambda j: (0, j + nj)
    act, u_a, u_g = pl.pallas_call(
        kern,
        grid=(nj,),
        in_specs=[pl.BlockSpec((m, d), lambda j: (0, 0)),
                  pl.BlockSpec((d, tn), col_a), pl.BlockSpec((d, tn), col_g),
                  pl.BlockSpec((CONV_W, tn), col_a), pl.BlockSpec((CONV_W, tn), col_g),
                  pl.BlockSpec((1, tn), col_a), pl.BlockSpec((1, tn), col_g),
                  pl.BlockSpec((m, tn), col_a), pl.BlockSpec((m, tn), col_g),
                  pl.BlockSpec((m, tn), col_a), pl.BlockSpec((m, tn), col_g)],
        out_specs=[pl.BlockSpec((m, tn), col_a), pl.BlockSpec((m, tn), col_a), pl.BlockSpec((m, tn), col_a)],
        out_shape=[jax.ShapeDtypeStruct((m, dff), BF16),
                   jax.ShapeDtypeStruct((m, dff), F32),
                   jax.ShapeDtypeStruct((m, dff), F32)],
        scratch_shapes=[pltpu.VMEM((8 + m, tn), F32), pltpu.VMEM((8 + m, tn), F32)],
        compiler_params=_params(("arbitrary",)),
        name="ffn_up_sample",
    )(h, w_up, w_up, conv_w, conv_w, cb, cb, b0, b0, b1, b1)
    u = jnp.concatenate([u_a, u_g], axis=-1).reshape(bsz, lq, 2 * dff)
    return act, u[:, lq - (CONV_W - 1):]


def _ffn_down_kernel(act_ref, w_ref, x_ref, gpost_ref, gnext_ref, x2_ref, h_ref, acc):
    k = pl.program_id(1)

    @pl.when(k == 0)
    def _():
        acc[...] = jnp.zeros(acc.shape, F32)

    acc[...] += jnp.dot(act_ref[...], w_ref[...], preferred_element_type=F32)

    @pl.when(k == pl.num_programs(1) - 1)
    def _():
        x2 = x_ref[...] + _rms(acc[...], gpost_ref[...])
        x2_ref[...] = x2
        h_ref[...] = _rms(x2, gnext_ref[...]).astype(h_ref.dtype)


def _ffn_down_call(act, w, x, g_post, g_next):
    m, d = x.shape
    dff = act.shape[1]
    tm = _pick(m, 512)
    tk = dff // 4 if (dff % 4 == 0 and (dff // 4) % LANES == 0) else dff
    row = lambda i, k: (i, 0)
    const = lambda i, k: (0, 0)
    return pl.pallas_call(
        _ffn_down_kernel,
        grid=(m // tm, dff // tk),
        in_specs=[pl.BlockSpec((tm, tk), lambda i, k: (i, k)),
                  pl.BlockSpec((tk, d), lambda i, k: (k, 0)),
                  pl.BlockSpec((tm, d), row),
                  pl.BlockSpec((1, d), const),
                  pl.BlockSpec((1, d), const)],
        out_specs=[pl.BlockSpec((tm, d), row), pl.BlockSpec((tm, d), row)],
        out_shape=[jax.ShapeDtypeStruct((m, d), F32), jax.ShapeDtypeStruct((m, d), BF16)],
        scratch_shapes=[pltpu.VMEM((tm, d), F32)],
        compiler_params=_params(("arbitrary", "arbitrary")),
        name="ffn_down",
    )(act, w, x, g_post.reshape(1, d).astype(F32), g_next.reshape(1, d).astype(F32))


def kernel(x_prompt, x_sample, cache_k, cache_v, state_ret, state_pool, state_conv, page_table,
           rel_bias_table, norm_mix_pre, w_in, lambda_q1, lambda_k1, lambda_q2, lambda_k2, subln_a,
           pool_w, pool_scale, w_out, norm_mix_post, norm_ffn_pre, ffn_w_up, ffn_conv_w, ffn_conv_b,
           ffn_w_down, norm_ffn_post):
    bp, seq, d = x_prompt.shape
    bs, lq, _ = x_sample.shape
    depth = w_in.shape[0]
    dff2 = ffn_w_up.shape[2]
    past_len = page_table.shape[1] * cache_k.shape[2]

    xp = x_prompt.reshape(bp * seq, d)
    xs = x_sample.reshape(bs * lq, d)
    hp = _norm_call(xp, norm_mix_pre[0])
    hs = _norm_call(xs, norm_mix_pre[0])
    zeros_hist_p = jnp.zeros((bp, POOL_HIST, W_C), F32)
    zeros_ret_p = jnp.zeros((bp, H_B, DK_B, DV_B), F32)
    zeros_conv_p = jnp.zeros((bp, CONV_W - 1, dff2), F32)

    outs = {k: [] for k in ('kp', 'vp', 'ks', 'vs', 'rp', 'rs', 'pp', 'ps', 'cp', 'cs')}
    for l in range(depth):
        lam_init = 0.8 - 0.6 * math.exp(-0.3 * l)
        lam = (jnp.exp(jnp.sum(lambda_q1[l].astype(F32) * lambda_k1[l].astype(F32)))
               - jnp.exp(jnp.sum(lambda_q2[l].astype(F32) * lambda_k2[l].astype(F32))) + lam_init)
        w_in_b = w_in[l].astype(BF16)
        w_out_b = w_out[l].astype(BF16)
        w_up_b = ffn_w_up[l].astype(BF16)
        w_down_b = ffn_w_down[l].astype(BF16)
        g_next = norm_mix_pre[l + 1] if l + 1 < depth else norm_mix_pre[l]

        proj_p = _matmul_call(hp, w_in_b)
        oa_p = _attn_prompt_call(proj_p, bp, seq, rel_bias_table, lam, subln_a[l], 1.0 - lam_init)
        bc_p, ret_p = _mixer_call(proj_p, bp, seq, 0, zeros_ret_p, zeros_hist_p, pool_w[l], pool_scale[l], BF16)
        x1_p, h2_p = _outproj_call(oa_p, bc_p, w_out_b, xp, norm_mix_post[l], norm_ffn_pre[l])
        act_p, conv_p = _ffn_up_prompt_call(h2_p, w_up_b, ffn_conv_w[l], ffn_conv_b[l], zeros_conv_p, bp, seq)
        xp, hp = _ffn_down_call(act_p, w_down_b, x1_p, norm_ffn_post[l], g_next)
        proj_p3 = proj_p.reshape(bp, seq, -1)
        outs['kp'].append(proj_p3[:, :, COL_KA:COL_KA + W_A].reshape(bp, seq, H_A, DV_A))
        outs['vp'].append(proj_p3[:, :, COL_VA:COL_VA + W_A].reshape(bp, seq, H_A, DV_A))
        outs['rp'].append(ret_p)
        outs['pp'].append(proj_p3[:, seq - POOL_BUF:, COL_XC:COL_XC + W_C])
        outs['cp'].append(conv_p)

        proj_s = _matmul_call(hs, w_in_b)
        oa_s = _attn_sample_call(proj_s, cache_k, cache_v, l, page_table, rel_bias_table, lam, subln_a[l],
                                 1.0 - lam_init, lq)
        hist_s = jnp.concatenate([jnp.zeros((bs, POOL_HIST - POOL_BUF, W_C), F32),
                                  state_pool[l].astype(F32)], axis=1)
        bc_s, ret_s = _mixer_call(proj_s, bs, lq, past_len, state_ret[l], hist_s, pool_w[l], pool_scale[l], F32)
        x1_s, h2_s = _outproj_call(oa_s, bc_s, w_out_b, xs, norm_mix_post[l], norm_ffn_pre[l])
        act_s, conv_s = _ffn_up_sample_call(h2_s, w_up_b, ffn_conv_w[l], ffn_conv_b[l],
                                            state_conv[l].astype(F32), bs, lq)
        xs, hs = _ffn_down_call(act_s, w_down_b, x1_s, norm_ffn_post[l], g_next)
        proj_s3 = proj_s.reshape(bs, lq, -1)
        outs['ks'].append(proj_s3[:, :, COL_KA:COL_KA + W_A].reshape(bs, lq, H_A, DV_A))
        outs['vs'].append(proj_s3[:, :, COL_VA:COL_VA + W_A].reshape(bs, lq, H_A, DV_A))
        outs['rs'].append(ret_s)
        xe_s = jnp.concatenate([state_pool[l].astype(F32), proj_s3[:, :, COL_XC:COL_XC + W_C]], axis=1)
        outs['ps'].append(xe_s[:, -POOL_BUF:])
        outs['cs'].append(conv_s)

    st = lambda k: jnp.stack(outs[k])
    return (xp.reshape(bp, seq, d), xs.reshape(bs, lq, d),
            st('kp'), st('vp'), st('ks'), st('vs'), st('rp'), st('rs'),
            st('pp'), st('ps'), st('cp'), st('cs'))
```

```python
import functools
import math

import numpy as np
import jax
import jax.numpy as jnp
from jax import lax
from jax.experimental import pallas as pl
from jax.experimental.pallas import tpu as pltpu

F32 = jnp.float32
BF16 = jnp.bfloat16

H_A = 6
DH_A = 64
DV_A = 128
W_A = H_A * DV_A
ATT_SCALE = DH_A ** -0.5
NEG_INF = -1e30
H_B = 6
DK_B = 64
DV_B = 128
W_B = H_B * DV_B
ROPE_BASE = 10000.0
N_POOL_GROUPS = 4
POOL_GC = 128
POOL_WINDOWS = (2, 4, 8, 16)
W_C = N_POOL_GROUPS * POOL_GC
POOL_BUF = 15
POOL_HIST = 16
REL_BUCKETS = 32
REL_MAX_DIST = 128
CONV_W = 3
EPS = 1e-6
COL_QA, COL_KA, COL_VA = 0, W_A, 2 * W_A
COL_QB = 3 * W_A
COL_KB = COL_QB + H_B * DK_B
COL_VB = COL_KB + H_B * DK_B
COL_GB = COL_VB + W_B
COL_XC = COL_GB + W_B

LANES = 128
VMEM_LIMIT = 56 * 1024 * 1024


def _params(sem, vmem=VMEM_LIMIT):
    return pltpu.CompilerParams(dimension_semantics=sem, vmem_limit_bytes=vmem)


def _rms(x, g=None):
    y = x * lax.rsqrt(jnp.mean(x * x, axis=-1, keepdims=True) + EPS)
    return y if g is None else y * g


def _pick(n, pref):
    if n <= pref:
        return n
    t = pref
    while n % t:
        t //= 2
    return t


def _norm_kernel(x_ref, g_ref, o_ref):
    o_ref[...] = _rms(x_ref[...], g_ref[...]).astype(o_ref.dtype)


def _norm_call(x, g):
    m, d = x.shape
    tm = _pick(m, 512)
    return pl.pallas_call(
        _norm_kernel,
        grid=(m // tm,),
        in_specs=[pl.BlockSpec((tm, d), lambda i: (i, 0)),
                  pl.BlockSpec((1, d), lambda i: (0, 0))],
        out_specs=pl.BlockSpec((tm, d), lambda i: (i, 0)),
        out_shape=jax.ShapeDtypeStruct((m, d), BF16),
        compiler_params=_params(("arbitrary",)),
        name="rmsnorm",
    )(x, g.reshape(1, d))


def _matmul_kernel(x_ref, w_ref, o_ref):
    o_ref[...] = jnp.dot(x_ref[...], w_ref[...], preferred_element_type=F32)


def _matmul_call(x, w):
    m, k = x.shape
    n = w.shape[1]
    tm = _pick(m, 1024)
    tn = _pick(n, 1024)
    return pl.pallas_call(
        _matmul_kernel,
        grid=(n // tn, m // tm),
        in_specs=[pl.BlockSpec((tm, k), lambda j, i: (i, 0)),
                  pl.BlockSpec((k, tn), lambda j, i: (0, j))],
        out_specs=pl.BlockSpec((tm, tn), lambda j, i: (i, j)),
        out_shape=jax.ShapeDtypeStruct((m, n), F32),
        compiler_params=_params(("arbitrary", "arbitrary")),
        name="in_proj",
    )(x, w)


def _t5_bucket(dist):
    n = jnp.maximum(dist, 0)
    exact = REL_BUCKETS // 2
    nf = jnp.maximum(n, 1).astype(F32)
    large = exact + (jnp.log(nf / exact) / math.log(REL_MAX_DIST / exact)
                     * (REL_BUCKETS - exact)).astype(jnp.int32)
    return jnp.where(n < exact, n, jnp.minimum(large, REL_BUCKETS - 1))


def _far_bucket_is_constant(min_dist):
    exact = REL_BUCKETS // 2
    v = np.float32(min_dist) / np.float32(exact)
    large = exact + int(np.float32(np.log(v)) / np.float32(math.log(REL_MAX_DIST / exact))
                        * (REL_BUCKETS - exact))
    return min_dist >= exact and large >= REL_BUCKETS


def _attn_prompt_kernel(sc_ref, q_ref, k_ref, v_ref, d0_ref, d1_ref, g_ref, o_ref,
                        kb, vb, qs, s_a, s_b, acc, mrow, *, tile, out_scale):
    h = pl.program_id(1)
    qi = pl.program_id(2)

    @pl.when(qi == 0)
    def _():
        kb[...] = k_ref[...].astype(BF16)
        vb[:, :DV_A] = v_ref[...].astype(BF16)
        vb[:, DV_A:] = jnp.ones((vb.shape[0], DV_A), BF16)

    q = q_ref[...] * ATT_SCALE
    lane = lax.broadcasted_iota(jnp.int32, q.shape, 1)
    qs[0:tile, :] = jnp.where(lane < DH_A, q, 0.0).astype(BF16)
    qs[tile:, :] = jnp.where(lane >= DH_A, q, 0.0).astype(BF16)
    acc[...] = jnp.zeros(acc.shape, F32)
    mrow[...] = jnp.full(mrow.shape, NEG_INF, F32)
    cfar = sc_ref[1 + h]
    reps = tile // LANES

    def scores(kj, s_ref):
        start = pl.multiple_of(kj * tile, tile)
        s_ref[...] = lax.dot_general(qs[...], kb[pl.ds(start, tile), :], (((1,), (1,)), ((), ())),
                                     preferred_element_type=F32)

    def softmax_pv(kj, s_ref, bias_ref):
        start = pl.multiple_of(kj * tile, tile)
        vs = vb[pl.ds(start, tile), :]
        s = s_ref[...]
        if bias_ref is None:
            m_cur = jnp.max(s, axis=-1, keepdims=True) + cfar
        else:
            s = s + jnp.concatenate([bias_ref[...], bias_ref[...]], axis=0)
            m_cur = jnp.max(s, axis=-1, keepdims=True)
        m_prev = mrow[...]
        m_new = jnp.maximum(m_prev, m_cur)
        alpha = jnp.exp(m_prev - m_new)
        shift = m_new - cfar if bias_ref is None else m_new
        p = jnp.exp(s - jnp.tile(shift, (1, reps))).astype(BF16)
        acc[...] = jnp.tile(alpha, (1, 2)) * acc[...] + jnp.dot(p, vs, preferred_element_type=F32)
        mrow[...] = m_new

    n_far = qi - 1
    n_pairs = jnp.maximum(n_far, 0) // 2

    @pl.when(qi >= 1)
    def _():
        scores(0, s_a)

    def pair_body(jj, carry):
        j = 2 * jj
        scores(j + 1, s_b)
        softmax_pv(j, s_a, None)
        scores(j + 2, s_a)
        softmax_pv(j + 1, s_b, None)
        return carry

    lax.fori_loop(0, n_pairs, pair_body, 0)

    @pl.when(jnp.logical_and(qi >= 1, n_far % 2 == 1))
    def _():
        scores(qi - 1, s_b)
        softmax_pv(qi - 2, s_a, None)
        scores(qi, s_a)
        softmax_pv(qi - 1, s_b, d1_ref)
        softmax_pv(qi, s_a, d0_ref)

    @pl.when(jnp.logical_and(qi >= 1, n_far % 2 == 0))
    def _():
        scores(qi, s_b)
        softmax_pv(qi - 1, s_a, d1_ref)
        softmax_pv(qi, s_b, d0_ref)

    @pl.when(qi == 0)
    def _():
        scores(0, s_a)
        softmax_pv(0, s_a, d0_ref)

    lam = sc_ref[0]
    a1 = acc[0:tile, :]
    a2 = acc[tile:, :]
    o = a1[:, :DV_A] / a1[:, DV_A:] - lam * (a2[:, :DV_A] / a2[:, DV_A:])
    o_ref[...] = (_rms(o, g_ref[...]) * out_scale).astype(o_ref.dtype)


def _toeplitz(u, t):
    hh = u.shape[0]
    return jnp.tile(u, (1, t))[:, :t * (2 * t - 1)].reshape(hh, t, 2 * t - 1)[:, :, :t]


def _attn_prompt_call(proj, bsz, seq, table, lam, subln, out_scale):
    tile = _pick(seq, 512)
    nq = seq // tile
    assert tile % LANES == 0 and _far_bucket_is_constant(tile + 1)
    bv = table[_t5_bucket(jnp.arange(2 * tile))].astype(F32).T
    neg = jnp.full((H_A, tile), NEG_INF, F32)
    u0 = jnp.concatenate([bv[:, :1], neg, bv[:, 1:tile][:, ::-1]], axis=1)
    u1 = jnp.concatenate([bv[:, tile:tile + 1], bv[:, 1:tile][:, ::-1], neg[:, :1],
                          bv[:, tile + 1:][:, ::-1]], axis=1)
    d0 = _toeplitz(u0, tile)
    d1 = _toeplitz(u1, tile)
    scal = jnp.concatenate([jnp.reshape(lam, (1,)).astype(F32),
                            table[REL_BUCKETS - 1].astype(F32)])
    kq, kk, kv = COL_QA // DV_A, COL_KA // DV_A, COL_VA // DV_A
    kern = functools.partial(_attn_prompt_kernel, tile=tile, out_scale=out_scale)
    return pl.pallas_call(
        kern,
        grid=(bsz, H_A, nq),
        in_specs=[
            pl.BlockSpec(memory_space=pltpu.SMEM),
            pl.BlockSpec((tile, DV_A), lambda b, h, i: (b * nq + i, kq + h)),
            pl.BlockSpec((seq, DV_A), lambda b, h, i: (b, kk + h)),
            pl.BlockSpec((seq, DV_A), lambda b, h, i: (b, kv + h)),
            pl.BlockSpec((None, tile, tile), lambda b, h, i: (h, 0, 0)),
            pl.BlockSpec((None, tile, tile), lambda b, h, i: (h, 0, 0)),
            pl.BlockSpec((1, DV_A), lambda b, h, i: (0, 0)),
        ],
        out_specs=pl.BlockSpec((tile, DV_A), lambda b, h, i: (b * nq + i, h)),
        out_shape=jax.ShapeDtypeStruct((bsz * seq, W_A), BF16),
        scratch_shapes=[pltpu.VMEM((seq, DV_A), BF16),
                        pltpu.VMEM((seq, 2 * DV_A), BF16),
                        pltpu.VMEM((2 * tile, DV_A), BF16),
                        pltpu.VMEM((2 * tile, tile), F32),
                        pltpu.VMEM((2 * tile, tile), F32),
                        pltpu.VMEM((2 * tile, 2 * DV_A), F32),
                        pltpu.VMEM((2 * tile, LANES), F32)],
        compiler_params=_params(("arbitrary", "arbitrary", "arbitrary")),
        name="attn_prompt",
    )(scal, proj, proj, proj, d0, d1, subln.reshape(1, DV_A).astype(F32))


def _attn_sample_kernel(pt_ref, sc_ref, qbd_ref, kn_ref, vn_ref, bl_ref, bf_ref, bn_ref, g_ref,
                        *rest, group, n_ksteps, page, lq, out_scale):
    k_refs = rest[:group]
    v_refs = rest[group:2 * group]
    o_ref = rest[2 * group]
    st, oacc, lsum, vnew = rest[2 * group + 1:]
    j = pl.program_id(1)
    past = n_ksteps * group * page
    chunk = group * page

    @pl.when(j < n_ksteps)
    def _():
        for r in range(group):
            row = pl.multiple_of((j * group + r) * page, page)
            s = jnp.zeros((page, LANES), F32)
            for h in range(H_A):
                s = s + jnp.dot(k_refs[r][h].astype(BF16), qbd_ref[h * DV_A:(h + 1) * DV_A, :],
                                preferred_element_type=F32)
            st[pl.ds(row, page), :] = s

    @pl.when(j == n_ksteps)
    def _():
        st[past:past + page, :] = jnp.full((page, LANES), NEG_INF, F32)
        st[past:past + lq, :] = jnp.dot(kn_ref[...].astype(BF16), qbd_ref[...],
                                        preferred_element_type=F32) + bn_ref[...]
        st[past - page:past, :] = st[past - page:past, :] + bl_ref[...]
        bfar = bf_ref[...]
        n_far = (past - page) // page

        def max_body(i, m):
            row = pl.multiple_of(i * page, page)
            return jnp.maximum(m, jnp.max(st[pl.ds(row, page), :], axis=0, keepdims=True))

        m_far = lax.fori_loop(0, n_far, max_body, jnp.full((1, LANES), NEG_INF, F32))
        m_near = jnp.max(st[past - page:past + page, :], axis=0, keepdims=True)
        m = jnp.maximum(m_far + bfar, m_near)

        def exp_body(i, l):
            row = pl.multiple_of(i * page, page)
            p = jnp.exp(st[pl.ds(row, page), :] - (m - bfar))
            st[pl.ds(row, page), :] = p
            return l + jnp.sum(p, axis=0, keepdims=True)

        l = lax.fori_loop(0, n_far, exp_body, jnp.zeros((1, LANES), F32))
        p = jnp.exp(st[past - page:past + page, :] - m)
        st[past - page:past + page, :] = p
        lsum[...] = l + jnp.sum(p, axis=0, keepdims=True)
        oacc[...] = jnp.zeros(oacc.shape, F32)
        vnew[...] = jnp.zeros(vnew.shape, BF16)
        vnew[0:lq, :] = vn_ref[...].astype(BF16)

    @pl.when(j >= n_ksteps)
    def _():
        jj = j - n_ksteps
        tot = [oacc[h] for h in range(H_A)]
        for r in range(group):
            row = pl.multiple_of((jj * group + r) * page, page)
            pt_t = st[pl.ds(row, page), :].T.astype(BF16)
            for h in range(H_A):
                tot[h] = tot[h] + jnp.dot(pt_t, v_refs[r][h].astype(BF16), preferred_element_type=F32)
        for h in range(H_A):
            oacc[h] = tot[h]

    @pl.when(j == 2 * n_ksteps - 1)
    def _():
        pn_t = st[past:past + page, :].T.astype(BF16)
        rr = lax.broadcasted_iota(jnp.int32, (LANES, LANES), 0)
        cc = lax.broadcasted_iota(jnp.int32, (LANES, LANES), 1)
        l_rows = jnp.sum(jnp.where(rr == cc, jnp.broadcast_to(lsum[...], (LANES, LANES)), 0.0),
                         axis=1, keepdims=True)
        lam = sc_ref[0]
        half = LANES // 2
        for h in range(H_A):
            r1 = h * lq
            r2 = half + h * lq
            tot = oacc[h] + jnp.dot(pn_t, vnew[:, h * DV_A:(h + 1) * DV_A], preferred_element_type=F32)
            o1 = tot[r1:r1 + lq] / l_rows[r1:r1 + lq]
            o2 = tot[r2:r2 + lq] / l_rows[r2:r2 + lq]
            o = o1 - lam * o2
            o_ref[:, h * DV_A:(h + 1) * DV_A] = _rms(o, g_ref[...]) * out_scale


def _attn_sample_call(proj, cache_k, cache_v, layer, page_table, table, lam, subln, out_scale, lq):
    bs = page_table.shape[0]
    n_pages = page_table.shape[1]
    n_pool, page = cache_k.shape[1], cache_k.shape[2]
    assert page == LANES and lq == 8 and H_A * lq <= LANES // 2
    assert _far_bucket_is_constant(page + 1)
    group = _pick(n_pages, 8)
    n_ksteps = n_pages // group
    past = n_pages * page
    half = LANES // 2
    ck = jnp.transpose(cache_k, (0, 1, 3, 2, 4))
    cv = jnp.transpose(cache_v, (0, 1, 3, 2, 4))

    q = (proj[:, COL_QA:COL_QA + W_A] * ATT_SCALE).reshape(bs, lq, H_A, 2, DH_A)
    eye_h = jnp.eye(H_A, LANES // 2 // lq, dtype=F32)
    eye_c = jnp.eye(2, dtype=F32)
    qbd = jnp.einsum('bqhcd,hH,cC->bhcdCHq', q, eye_h, eye_c).reshape(bs, W_A, LANES).astype(BF16)

    lane = np.arange(LANES)
    lane_h = np.minimum((lane % half) // lq, H_A - 1)
    lane_q = lane % lq
    valid = jnp.asarray((lane % half) < H_A * lq)
    tab_t = table.astype(F32).T
    kk = np.arange(page)[:, None]
    b_last = jnp.where(valid[None], tab_t[lane_h[None], _t5_bucket(jnp.asarray(page + lane_q[None] - kk))], 0.0)
    jj = np.arange(lq)[:, None]
    dn = lane_q[None] - jj
    b_new = jnp.where(valid[None], tab_t[lane_h[None], _t5_bucket(jnp.asarray(np.maximum(dn, 0)))], 0.0)
    b_new = jnp.where(jnp.asarray(dn >= 0), b_new, NEG_INF)
    b_far = jnp.where(valid, tab_t[lane_h, REL_BUCKETS - 1], 0.0).reshape(1, LANES)
    scal = jnp.reshape(lam, (1,)).astype(F32)
    pt_flat = page_table.reshape(-1).astype(jnp.int32)

    def k_map(r):
        return lambda b, j, pt: (layer, pt[b * n_pages + jnp.minimum(j, n_ksteps - 1) * group + r], 0, 0, 0)

    def v_map(r):
        return lambda b, j, pt: (layer, pt[b * n_pages + jnp.maximum(j - n_ksteps, 0) * group + r], 0, 0, 0)

    kcol, vcol = COL_KA // W_A, COL_VA // W_A
    in_specs = [
        pl.BlockSpec(memory_space=pltpu.SMEM),
        pl.BlockSpec((None, W_A, LANES), lambda b, j, pt: (b, 0, 0)),
        pl.BlockSpec((lq, W_A), lambda b, j, pt: (b, kcol)),
        pl.BlockSpec((lq, W_A), lambda b, j, pt: (b, vcol)),
        pl.BlockSpec((page, LANES), lambda b, j, pt: (0, 0)),
        pl.BlockSpec((1, LANES), lambda b, j, pt: (0, 0)),
        pl.BlockSpec((lq, LANES), lambda b, j, pt: (0, 0)),
        pl.BlockSpec((1, DV_A), lambda b, j, pt: (0, 0)),
    ]
    in_specs += [pl.BlockSpec((None, None, H_A, page, DV_A), k_map(r)) for r in range(group)]
    in_specs += [pl.BlockSpec((None, None, H_A, page, DV_A), v_map(r)) for r in range(group)]
    kern = functools.partial(_attn_sample_kernel, group=group, n_ksteps=n_ksteps, page=page, lq=lq,
                             out_scale=out_scale)
    grid_spec = pltpu.PrefetchScalarGridSpec(
        num_scalar_prefetch=1,
        grid=(bs, 2 * n_ksteps),
        in_specs=in_specs,
        out_specs=pl.BlockSpec((None, lq, W_A), lambda b, j, pt: (b, 0, 0)),
        scratch_shapes=[pltpu.VMEM((past + page, LANES), F32),
                        pltpu.VMEM((H_A, LANES, DV_A), F32),
                        pltpu.VMEM((1, LANES), F32),
                        pltpu.VMEM((page, W_A), BF16)],
    )
    out = pl.pallas_call(
        kern,
        grid_spec=grid_spec,
        out_shape=jax.ShapeDtypeStruct((bs, lq, W_A), F32),
        compiler_params=_params(("arbitrary", "arbitrary")),
        name="attn_sample",
    )(pt_flat, scal, qbd, proj, proj, b_last, b_far, b_new, subln.reshape(1, DV_A).astype(F32),
      *([ck] * group), *([cv] * group))
    return out.reshape(bs * lq, W_A)


def _log_gamma():
    return np.log(1.0 - 2.0 ** (-5.0 - np.arange(H_B, dtype=np.float64)))


def _retention_constants(chunk_math, rows):
    lg = _log_gamma()
    i = np.arange(rows, dtype=np.float64)
    diff = i[:, None] - i[None, :]
    dmat = np.where(diff >= 0, np.exp(np.maximum(diff, 0.0)[None] * lg[:, None, None]), 0.0)
    xi = np.exp((i[:, None] + 1.0) * lg[None, :])
    zeta = np.exp((chunk_math - 1.0 - i)[:, None] * lg[None, :])
    zeta = np.where(i[:, None] < chunk_math, zeta, 0.0)
    cdec = np.exp(chunk_math * lg)
    xi_l = np.repeat(xi, DV_B, axis=1)
    zeta_l = np.repeat(zeta, DK_B, axis=1) * (DK_B ** -0.5)
    cdec_rows = np.repeat(cdec.reshape(H_B // 2, 2), DK_B, axis=1).reshape(H_B // 2, 2 * DK_B, 1)
    return (jnp.asarray(dmat, F32), jnp.asarray(xi_l, F32), jnp.asarray(zeta_l, F32),
            jnp.asarray(cdec_rows, F32))


def _rope_tables(pos):
    half = DK_B // 2
    inv = ROPE_BASE ** (-jnp.arange(half, dtype=F32) / half)
    ang = pos.astype(F32)[:, None] * inv[None, :]
    cos = jnp.cos(ang)
    sin = jnp.sin(ang)
    cos_l = jnp.tile(jnp.concatenate([cos, cos], axis=1), (1, H_B))
    sin_l = jnp.tile(jnp.concatenate([-sin, sin], axis=1), (1, H_B))
    return cos_l, sin_l


def _mixer_kernel(qb_ref, kb_ref, vb_ref, gb_ref, xc_ref, cos_ref, sin_ref, dmat_ref, xi_ref, zeta_ref,
                  cdec_ref, s0_ref, hist_ref, pw_ref, ps_ref, o_ref, sout_ref, state, ext,
                  *, rows_in, rows, pos0):
    ci = pl.program_id(1)
    n_chunks = pl.num_programs(1)

    @pl.when(ci == 0)
    def _():
        state[...] = s0_ref[...]
        ext[0:POOL_HIST, :] = hist_ref[...]

    def pad_rows(x):
        if rows == rows_in:
            return x
        return jnp.concatenate([x, jnp.zeros((rows - rows_in, x.shape[1]), x.dtype)], axis=0)

    lane = lax.broadcasted_iota(jnp.int32, (rows, LANES), 1)
    first_half = (lane % DK_B) < (DK_B // 2)
    low_head = lane < DK_B

    def rope(x, cos, sin):
        swapped = jnp.where(first_half, pltpu.roll(x, LANES - DK_B // 2, 1), pltpu.roll(x, DK_B // 2, 1))
        return x * cos + swapped * sin

    cos = pad_rows(cos_ref[...])
    sin = pad_rows(sin_ref[...])
    qb = pad_rows(qb_ref[...])
    kb = pad_rows(kb_ref[...])
    vb = pad_rows(vb_ref[...])
    gb = pad_rows(gb_ref[...])
    zeta = zeta_ref[...]
    xi = xi_ref[...]
    for pr in range(H_B // 2):
        sl = slice(pr * LANES, (pr + 1) * LANES)
        q2 = rope(qb[:, sl], cos[:, sl], sin[:, sl])
        k2 = rope(kb[:, sl], cos[:, sl], sin[:, sl])
        kz2 = k2 * zeta[:, sl]
        k2 = k2 * (DK_B ** -0.5)
        s_pair = state[pr]
        s_pair_b = s_pair.astype(BF16)
        upd = jnp.zeros((LANES, DV_B), F32)
        k2b = k2.astype(BF16)
        for hh in range(2):
            h = 2 * pr + hh
            mask = low_head if hh == 0 else jnp.logical_not(low_head)
            qm = jnp.where(mask, q2, 0.0).astype(BF16)
            vh = vb[:, h * DV_B:(h + 1) * DV_B].astype(BF16)
            a = lax.dot_general(qm, k2b, (((1,), (1,)), ((), ())), preferred_element_type=F32)
            a = (a * dmat_ref[h]).astype(BF16)
            o = jnp.dot(a, vh, preferred_element_type=F32)
            o = o + jnp.dot(qm, s_pair_b, preferred_element_type=F32) * xi[:, h * DV_B:(h + 1) * DV_B]
            kzm_t = jnp.where(mask, kz2, 0.0).T.astype(BF16)
            upd = upd + jnp.dot(kzm_t, vh, preferred_element_type=F32)
            g = gb[:, h * DV_B:(h + 1) * DV_B]
            y = _rms(o) * (g / (1.0 + jnp.exp(-g)))
            o_ref[:, h * DV_B:(h + 1) * DV_B] = y[0:rows_in].astype(o_ref.dtype)
        state[pr] = s_pair * cdec_ref[pr] + upd

    @pl.when(ci == n_chunks - 1)
    def _():
        for pr in range(H_B // 2):
            sp = state[pr]
            sout_ref[2 * pr] = sp[0:DK_B]
            sout_ref[2 * pr + 1] = sp[DK_B:]

    xc = xc_ref[...]
    ext[POOL_HIST:POOL_HIST + rows_in, :] = xc
    pos = pos0 + ci * rows_in + lax.broadcasted_iota(jnp.int32, (rows_in, 1), 0)
    for gi, win in enumerate(POOL_WINDOWS):
        sl = slice(gi * POOL_GC, (gi + 1) * POOL_GC)
        tot = xc[:, sl]
        for d in range(1, win):
            tot = tot + ext[POOL_HIST - d:POOL_HIST - d + rows_in, sl]
        cnt = jnp.minimum(pos + 1, win).astype(F32)
        pooled = (tot / cnt - xc[:, sl]).astype(BF16)
        y = jnp.dot(pooled, pw_ref[gi], preferred_element_type=F32) * ps_ref[:, sl]
        o_ref[:, W_B + gi * POOL_GC:W_B + (gi + 1) * POOL_GC] = y.astype(o_ref.dtype)
    ext[0:POOL_HIST, :] = ext[rows_in:rows_in + POOL_HIST, :]


def _mixer_call(proj, bsz, seq, pos0, s0, hist, pool_w, pool_scale, out_dtype):
    rows_in = _pick(seq, 256)
    rows = max(rows_in, LANES)
    n_chunks = seq // rows_in
    assert rows_in % 8 == 0 and (rows_in == rows or n_chunks == 1)
    dmat, xi_l, zeta_l, cdec_rows = _retention_constants(rows_in, rows)
    cos_l, sin_l = _rope_tables(pos0 + jnp.arange(seq))
    s0p = s0.astype(F32).reshape(bsz, H_B // 2, 2 * DK_B, DV_B)
    wq, wv, wc = H_B * DK_B, W_B, W_C
    kern = functools.partial(_mixer_kernel, rows_in=rows_in, rows=rows, pos0=pos0)
    row_map = lambda col: (lambda b, c: (b * n_chunks + c, col))
    const2 = lambda b, c: (0, 0)
    const3 = lambda b, c: (0, 0, 0)
    out, sout = pl.pallas_call(
        kern,
        grid=(bsz, n_chunks),
        in_specs=[
            pl.BlockSpec((rows_in, wq), row_map(COL_QB // wq)),
            pl.BlockSpec((rows_in, wq), row_map(COL_KB // wq)),
            pl.BlockSpec((rows_in, wv), row_map(COL_VB // wv)),
            pl.BlockSpec((rows_in, wv), row_map(COL_GB // wv)),
            pl.BlockSpec((rows_in, wc), row_map(COL_XC // wc)),
            pl.BlockSpec((rows_in, wq), lambda b, c: (c, 0)),
            pl.BlockSpec((rows_in, wq), lambda b, c: (c, 0)),
            pl.BlockSpec((H_B, rows, rows), const3),
            pl.BlockSpec((rows, W_B), const2),
            pl.BlockSpec((rows, wq), const2),
            pl.BlockSpec((H_B // 2, 2 * DK_B, 1), const3),
            pl.BlockSpec((None, H_B // 2, 2 * DK_B, DV_B), lambda b, c: (b, 0, 0, 0)),
            pl.BlockSpec((None, POOL_HIST, W_C), lambda b, c: (b, 0, 0)),
            pl.BlockSpec((N_POOL_GROUPS, POOL_GC, POOL_GC), const3),
            pl.BlockSpec((1, W_C), const2),
        ],
        out_specs=[pl.BlockSpec((rows_in, W_B + W_C), lambda b, c: (b * n_chunks + c, 0)),
                   pl.BlockSpec((None, H_B, DK_B, DV_B), lambda b, c: (b, 0, 0, 0))],
        out_shape=[jax.ShapeDtypeStruct((bsz * seq, W_B + W_C), out_dtype),
                   jax.ShapeDtypeStruct((bsz, H_B, DK_B, DV_B), F32)],
        scratch_shapes=[pltpu.VMEM((H_B // 2, 2 * DK_B, DV_B), F32),
                        pltpu.VMEM((POOL_HIST + rows_in, W_C), F32)],
        compiler_params=_params(("arbitrary", "arbitrary")),
        name="mixer_ret_pool",
    )(proj, proj, proj, proj, proj, cos_l, sin_l, dmat, xi_l, zeta_l, cdec_rows, s0p, hist,
      pool_w.astype(BF16), pool_scale.reshape(1, W_C).astype(F32))
    return out, sout


def _outproj_kernel(oa_ref, bc_ref, w_ref, x_ref, gpost_ref, gnext_ref, x1_ref, h_ref):
    wa = oa_ref.shape[1]
    mix = jnp.dot(oa_ref[...].astype(BF16), w_ref[0:wa, :], preferred_element_type=F32)
    mix = mix + jnp.dot(bc_ref[...].astype(BF16), w_ref[wa:, :], preferred_element_type=F32)
    x1 = x_ref[...] + _rms(mix, gpost_ref[...])
    x1_ref[...] = x1
    h_ref[...] = _rms(x1, gnext_ref[...]).astype(h_ref.dtype)


def _outproj_call(oa, bc, w, x, g_post, g_next):
    m, d = x.shape
    tm = _pick(m, 512)
    row = lambda i: (i, 0)
    const = lambda i: (0, 0)
    return pl.pallas_call(
        _outproj_kernel,
        grid=(m // tm,),
        in_specs=[pl.BlockSpec((tm, oa.shape[1]), row),
                  pl.BlockSpec((tm, bc.shape[1]), row),
                  pl.BlockSpec(w.shape, const),
                  pl.BlockSpec((tm, d), row),
                  pl.BlockSpec((1, d), const),
                  pl.BlockSpec((1, d), const)],
        out_specs=[pl.BlockSpec((tm, d), row), pl.BlockSpec((tm, d), row)],
        out_shape=[jax.ShapeDtypeStruct((m, d), F32), jax.ShapeDtypeStruct((m, d), BF16)],
        compiler_params=_params(("arbitrary",)),
        name="out_proj",
    )(oa, bc, w, x, g_post.reshape(1, d).astype(F32), g_next.reshape(1, d).astype(F32))


def _gelu_tanh(g):
    return 0.5 * g * (1.0 + jnp.tanh(math.sqrt(2.0 / math.pi) * (g + 0.044715 * (g * g * g))))


def _conv_taps(u, tail, cw_ref, cb_ref, cols):
    row8 = lax.broadcasted_iota(jnp.int32, tail.shape, 0)
    r1 = pltpu.roll(u, 1, 0)
    r2 = pltpu.roll(u, 2, 0)
    head1 = jnp.where(row8 < 1, pltpu.roll(tail, 1, 0), r1[0:8])
    head2 = jnp.where(row8 < 2, pltpu.roll(tail, 2, 0), r2[0:8])
    um1 = jnp.concatenate([head1, r1[8:]], axis=0)
    um2 = jnp.concatenate([head2, r2[8:]], axis=0)
    return (um2 * cw_ref[0:1, cols] + um1 * cw_ref[1:2, cols] + u * cw_ref[2:3, cols]) + cb_ref[:, cols]


def _ffn_up_prompt_kernel(h_ref, wa_ref, wg_ref, cwa_ref, cwg_ref, cba_ref, cbg_ref, bufa_ref, bufg_ref,
                          act_ref, lasta_ref, lastg_ref, ta, tg, *, tm, tn, cc, tiles_per_seq):
    i = pl.program_id(1)

    @pl.when(i % tiles_per_seq == 0)
    def _():
        ta[8 - (CONV_W - 1):, :] = bufa_ref[...]
        tg[8 - (CONV_W - 1):, :] = bufg_ref[...]

    hh = h_ref[...]
    for c0 in range(0, tn, cc):
        cols = slice(c0, c0 + cc)
        ua = jnp.dot(hh, wa_ref[:, cols], preferred_element_type=F32)
        ug = jnp.dot(hh, wg_ref[:, cols], preferred_element_type=F32)
        a = _conv_taps(ua, ta[:, cols], cwa_ref, cba_ref, cols)
        g = _conv_taps(ug, tg[:, cols], cwg_ref, cbg_ref, cols)
        act_ref[:, cols] = (a * _gelu_tanh(g)).astype(act_ref.dtype)
        ta[:, cols] = ua[tm - 8:, :]
        tg[:, cols] = ug[tm - 8:, :]
        lasta_ref[:, cols] = ua[tm - (CONV_W - 1):, :]
        lastg_ref[:, cols] = ug[tm - (CONV_W - 1):, :]


def _ffn_up_prompt_call(h, w_up, conv_w, conv_b, conv_buf, bsz, seq):
    m, d = h.shape
    dff = w_up.shape[1] // 2
    tm = _pick(seq, 1024)
    tn = _pick(dff, 512)
    cc = _pick(tn, 256)
    nj = dff // tn
    tiles_per_seq = seq // tm
    kern = functools.partial(_ffn_up_prompt_kernel, tm=tm, tn=tn, cc=cc, tiles_per_seq=tiles_per_seq)
    cb = conv_b.reshape(1, 2 * dff).astype(F32)
    seq_map_a = lambda j, i: (i // tiles_per_seq, 0, j)
    seq_map_g = lambda j, i: (i // tiles_per_seq, 0, j + nj)
    act, last_a, last_g = pl.pallas_call(
        kern,
        grid=(nj, m // tm),
        in_specs=[pl.BlockSpec((tm, d), lambda j, i: (i, 0)),
                  pl.BlockSpec((d, tn), lambda j, i: (0, j)),
                  pl.BlockSpec((d, tn), lambda j, i: (0, j + nj)),
                  pl.BlockSpec((CONV_W, tn), lambda j, i: (0, j)),
                  pl.BlockSpec((CONV_W, tn), lambda j, i: (0, j + nj)),
                  pl.BlockSpec((1, tn), lambda j, i: (0, j)),
                  pl.BlockSpec((1, tn), lambda j, i: (0, j + nj)),
                  pl.BlockSpec((None, CONV_W - 1, tn), seq_map_a),
                  pl.BlockSpec((None, CONV_W - 1, tn), seq_map_g)],
        out_specs=[pl.BlockSpec((tm, tn), lambda j, i: (i, j)),
                   pl.BlockSpec((None, CONV_W - 1, tn), lambda j, i: (i // tiles_per_seq, 0, j)),
                   pl.BlockSpec((None, CONV_W - 1, tn), lambda j, i: (i // tiles_per_seq, 0, j))],
        out_shape=[jax.ShapeDtypeStruct((m, dff), BF16),
                   jax.ShapeDtypeStruct((bsz, CONV_W - 1, dff), F32),
                   jax.ShapeDtypeStruct((bsz, CONV_W - 1, dff), F32)],
        scratch_shapes=[pltpu.VMEM((8, tn), F32), pltpu.VMEM((8, tn), F32)],
        compiler_params=_params(("arbitrary", "arbitrary")),
        name="ffn_up_prompt",
    )(h, w_up, w_up, conv_w, conv_w, cb, cb, conv_buf, conv_buf)
    return act, jnp.concatenate([last_a, last_g], axis=-1)


def _ffn_up_sample_kernel(h_ref, wa_ref, wg_ref, cwa_ref, cwg_ref, cba_ref, cbg_ref,
                          b0a_ref, b0g_ref, b1a_ref, b1g_ref, act_ref, ua_ref, ug_ref, sa, sg, *, m, lq):
    pad = 8
    hh = h_ref[...]
    t = lax.broadcasted_iota(jnp.int32, (m, 1), 0) % lq

    def branch(w_ref, cw_ref, cb_ref, b0_ref, b1_ref, u_ref, s):
        u = jnp.dot(hh, w_ref[...], preferred_element_type=F32)
        u_ref[...] = u
        s[0:pad, :] = jnp.zeros((pad, s.shape[1]), F32)
        s[pad:pad + m, :] = u
        um1 = jnp.where(t == 0, b1_ref[...], s[pad - 1:pad - 1 + m, :])
        um2 = jnp.where(t == 0, b0_ref[...], jnp.where(t == 1, b1_ref[...], s[pad - 2:pad - 2 + m, :]))
        return um2 * cw_ref[0:1, :] + um1 * cw_ref[1:2, :] + u * cw_ref[2:3, :] + cb_ref[...]

    a = branch(wa_ref, cwa_ref, cba_ref, b0a_ref, b1a_ref, ua_ref, sa)
    g = branch(wg_ref, cwg_ref, cbg_ref, b0g_ref, b1g_ref, ug_ref, sg)
    act_ref[...] = (a * _gelu_tanh(g)).astype(act_ref.dtype)


def _ffn_up_sample_call(h, w_up, conv_w, conv_b, conv_buf, bsz, lq):
    m, d = h.shape
    dff = w_up.shape[1] // 2
    tn = _pick(dff, 512)
    nj = dff // tn
    assert lq >= CONV_W - 1
    kern = functools.partial(_ffn_up_sample_kernel, m=m, lq=lq)
    cb = conv_b.reshape(1, 2 * dff).astype(F32)
    b0 = jnp.repeat(conv_buf[:, 0], lq, axis=0)
    b1 = jnp.repeat(conv_buf[:, 1], lq, axis=0)
    col_a = lambda j: (0, j)
    col_g = lambda j: (0, j + nj)
    act, u_a, u_g = pl.pallas_call(
        kern,
        grid=(nj,),
        in_specs=[pl.BlockSpec((m, d), lambda j: (0, 0)),
                  pl.BlockSpec((d, tn), col_a), pl.BlockSpec((d, tn), col_g),
                  pl.BlockSpec((CONV_W, tn), col_a), pl.BlockSpec((CONV_W, tn), col_g),
                  pl.BlockSpec((1, tn), col_a), pl.BlockSpec((1, tn), col_g),
                  pl.BlockSpec((m, tn), col_a), pl.BlockSpec((m, tn), col_g),
                  pl.BlockSpec((m, tn), col_a), pl.BlockSpec((m, tn), col_g)],
        out_specs=[pl.BlockSpec((m, tn), col_a), pl.BlockSpec((m, tn), col_a), pl.BlockSpec((m, tn), col_a)],
        out_shape=[jax.ShapeDtypeStruct((m, dff), BF16),
                   jax.ShapeDtypeStruct((m, dff), F32),
                   jax.ShapeDtypeStruct((m, dff), F32)],
        scratch_shapes=[pltpu.VMEM((8 + m, tn), F32), pltpu.VMEM((8 + m, tn), F32)],
        compiler_params=_params(("arbitrary",)),
        name="ffn_up_sample",
    )(h, w_up, w_up, conv_w, conv_w, cb, cb, b0, b0, b1, b1)
    u = jnp.concatenate([u_a, u_g], axis=-1).reshape(bsz, lq, 2 * dff)
    return act, u[:, lq - (CONV_W - 1):]


def _ffn_down_kernel(act_ref, w_ref, x_ref, gpost_ref, gnext_ref, x2_ref, h_ref, acc):
    k = pl.program_id(1)

    @pl.when(k == 0)
    def _():
        acc[...] = jnp.zeros(acc.shape, F32)

    acc[...] += jnp.dot(act_ref[...], w_ref[...], preferred_element_type=F32)

    @pl.when(k == pl.num_programs(1) - 1)
    def _():
        x2 = x_ref[...] + _rms(acc[...], gpost_ref[...])
        x2_ref[...] = x2
        h_ref[...] = _rms(x2, gnext_ref[...]).astype(h_ref.dtype)


def _ffn_down_call(act, w, x, g_post, g_next):
    m, d = x.shape
    dff = act.shape[1]
    tm = _pick(m, 512)
    tk = dff // 4 if (dff % 4 == 0 and (dff // 4) % LANES == 0) else dff
    row = lambda i, k: (i, 0)
    const = lambda i, k: (0, 0)
    return pl.pallas_call(
        _ffn_down_kernel,
        grid=(m // tm, dff // tk),
        in_specs=[pl.BlockSpec((tm, tk), lambda i, k: (i, k)),
                  pl.BlockSpec((tk, d), lambda i, k: (k, 0)),
                  pl.BlockSpec((tm, d), row),
                  pl.BlockSpec((1, d), const),
                  pl.BlockSpec((1, d), const)],
        out_specs=[pl.BlockSpec((tm, d), row), pl.BlockSpec((tm, d), row)],
        out_shape=[jax.ShapeDtypeStruct((m, d), F32), jax.ShapeDtypeStruct((m, d), BF16)],
        scratch_shapes=[pltpu.VMEM((tm, d), F32)],
        compiler_params=_params(("arbitrary", "arbitrary")),
        name="ffn_down",
    )(act, w, x, g_post.reshape(1, d).astype(F32), g_next.reshape(1, d).astype(F32))


def kernel(x_prompt, x_sample, cache_k, cache_v, state_ret, state_pool, state_conv, page_table,
           rel_bias_table, norm_mix_pre, w_in, lambda_q1, lambda_k1, lambda_q2, lambda_k2, subln_a,
           pool_w, pool_scale, w_out, norm_mix_post, norm_ffn_pre, ffn_w_up, ffn_conv_w, ffn_conv_b,
           ffn_w_down, norm_ffn_post):
    bp, seq, d = x_prompt.shape
    bs, lq, _ = x_sample.shape
    depth = w_in.shape[0]
    dff2 = ffn_w_up.shape[2]
    past_len = page_table.shape[1] * cache_k.shape[2]

    xp = x_prompt.reshape(bp * seq, d)
    xs = x_sample.reshape(bs * lq, d)
    hp = _norm_call(xp, norm_mix_pre[0])
    hs = _norm_call(xs, norm_mix_pre[0])
    zeros_hist_p = jnp.zeros((bp, POOL_HIST, W_C), F32)
    zeros_ret_p = jnp.zeros((bp, H_B, DK_B, DV_B), F32)
    zeros_conv_p = jnp.zeros((bp, CONV_W - 1, dff2), F32)

    outs = {k: [] for k in ('kp', 'vp', 'ks', 'vs', 'rp', 'rs', 'pp', 'ps', 'cp', 'cs')}
    for l in range(depth):
        lam_init = 0.8 - 0.6 * math.exp(-0.3 * l)
        lam = (jnp.exp(jnp.sum(lambda_q1[l].astype(F32) * lambda_k1[l].astype(F32)))
               - jnp.exp(jnp.sum(lambda_q2[l].astype(F32) * lambda_k2[l].astype(F32))) + lam_init)
        w_in_b = w_in[l].astype(BF16)
        w_out_b = w_out[l].astype(BF16)
        w_up_b = ffn_w_up[l].astype(BF16)
        w_down_b = ffn_w_down[l].astype(BF16)
        g_next = norm_mix_pre[l + 1] if l + 1 < depth else norm_mix_pre[l]

        proj_p = _matmul_call(hp, w_in_b)
        oa_p = _attn_prompt_call(proj_p, bp, seq, rel_bias_table, lam, subln_a[l], 1.0 - lam_init)
        bc_p, ret_p = _mixer_call(proj_p, bp, seq, 0, zeros_ret_p, zeros_hist_p, pool_w[l], pool_scale[l], BF16)
        x1_p, h2_p = _outproj_call(oa_p, bc_p, w_out_b, xp, norm_mix_post[l], norm_ffn_pre[l])
        act_p, conv_p = _ffn_up_prompt_call(h2_p, w_up_b, ffn_conv_w[l], ffn_conv_b[l], zeros_conv_p, bp, seq)
        xp, hp = _ffn_down_call(act_p, w_down_b, x1_p, norm_ffn_post[l], g_next)
        proj_p3 = proj_p.reshape(bp, seq, -1)
        outs['kp'].append(proj_p3[:, :, COL_KA:COL_KA + W_A].reshape(bp, seq, H_A, DV_A))
        outs['vp'].append(proj_p3[:, :, COL_VA:COL_VA + W_A].reshape(bp, seq, H_A, DV_A))
        outs['rp'].append(ret_p)
        outs['pp'].append(proj_p3[:, seq - POOL_BUF:, COL_XC:COL_XC + W_C])
        outs['cp'].append(conv_p)

        proj_s = _matmul_call(hs, w_in_b)
        oa_s = _attn_sample_call(proj_s, cache_k, cache_v, l, page_table, rel_bias_table, lam, subln_a[l],
                                 1.0 - lam_init, lq)
        hist_s = jnp.concatenate([jnp.zeros((bs, POOL_HIST - POOL_BUF, W_C), F32),
                                  state_pool[l].astype(F32)], axis=1)
        bc_s, ret_s = _mixer_call(proj_s, bs, lq, past_len, state_ret[l], hist_s, pool_w[l], pool_scale[l], F32)
        x1_s, h2_s = _outproj_call(oa_s, bc_s, w_out_b, xs, norm_mix_post[l], norm_ffn_pre[l])
        act_s, conv_s = _ffn_up_sample_call(h2_s, w_up_b, ffn_conv_w[l], ffn_conv_b[l],
                                            state_conv[l].astype(F32), bs, lq)
        xs, hs = _ffn_down_call(act_s, w_down_b, x1_s, norm_ffn_post[l], g_next)
        proj_s3 = proj_s.reshape(bs, lq, -1)
        outs['ks'].append(proj_s3[:, :, COL_KA:COL_KA + W_A].reshape(bs, lq, H_A, DV_A))
        outs['vs'].append(proj_s3[:, :, COL_VA:COL_VA + W_A].reshape(bs, lq, H_A, DV_A))
        outs['rs'].append(ret_s)
        xe_s = jnp.concatenate([state_pool[l].astype(F32), proj_s3[:, :, COL_XC:COL_XC + W_C]], axis=1)
        outs['ps'].append(xe_s[:, -POOL_BUF:])
        outs['cs'].append(conv_s)

    st = lambda k: jnp.stack(outs[k])
    return (xp.reshape(bp, seq, d), xs.reshape(bs, lq, d),
            st('kp'), st('vp'), st('ks'), st('vs'), st('rp'), st('rs'),
            st('pp'), st('ps'), st('cp'), st('cs'))
```

```python
import functools
import math

import numpy as np
import jax
import jax.numpy as jnp
from jax import lax
from jax.experimental import pallas as pl
from jax.experimental.pallas import tpu as pltpu

F32 = jnp.float32
BF16 = jnp.bfloat16

H_A = 6
DH_A = 64
DV_A = 128
W_A = H_A * DV_A
ATT_SCALE = DH_A ** -0.5
NEG_INF = -1e30
H_B = 6
DK_B = 64
DV_B = 128
W_B = H_B * DV_B
ROPE_BASE = 10000.0
N_POOL_GROUPS = 4
POOL_GC = 128
POOL_WINDOWS = (2, 4, 8, 16)
W_C = N_POOL_GROUPS * POOL_GC
POOL_BUF = 15
POOL_HIST = 16
REL_BUCKETS = 32
REL_MAX_DIST = 128
CONV_W = 3
EPS = 1e-6
COL_QA, COL_KA, COL_VA = 0, W_A, 2 * W_A
COL_QB = 3 * W_A
COL_KB = COL_QB + H_B * DK_B
COL_VB = COL_KB + H_B * DK_B
COL_GB = COL_VB + W_B
COL_XC = COL_GB + W_B

LANES = 128
VMEM_LIMIT = 56 * 1024 * 1024


def _params(sem, vmem=VMEM_LIMIT):
    return pltpu.CompilerParams(dimension_semantics=sem, vmem_limit_bytes=vmem)


def _rms(x, g=None):
    y = x * lax.rsqrt(jnp.mean(x * x, axis=-1, keepdims=True) + EPS)
    return y if g is None else y * g


def _pick(n, pref):
    if n <= pref:
        return n
    t = pref
    while n % t:
        t //= 2
    return t


def _norm_kernel(x_ref, g_ref, o_ref):
    o_ref[...] = _rms(x_ref[...], g_ref[...]).astype(o_ref.dtype)


def _norm_call(x, g):
    m, d = x.shape
    tm = _pick(m, 512)
    return pl.pallas_call(
        _norm_kernel,
        grid=(m // tm,),
        in_specs=[pl.BlockSpec((tm, d), lambda i: (i, 0)),
                  pl.BlockSpec((1, d), lambda i: (0, 0))],
        out_specs=pl.BlockSpec((tm, d), lambda i: (i, 0)),
        out_shape=jax.ShapeDtypeStruct((m, d), BF16),
        compiler_params=_params(("arbitrary",)),
        name="rmsnorm",
    )(x, g.reshape(1, d))


def _matmul_kernel(x_ref, w_ref, o_ref):
    o_ref[...] = jnp.dot(x_ref[...], w_ref[...], preferred_element_type=F32)


def _matmul_call(x, w):
    m, k = x.shape
    n = w.shape[1]
    tm = _pick(m, 1024)
    tn = _pick(n, 1024)
    return pl.pallas_call(
        _matmul_kernel,
        grid=(n // tn, m // tm),
        in_specs=[pl.BlockSpec((tm, k), lambda j, i: (i, 0)),
                  pl.BlockSpec((k, tn), lambda j, i: (0, j))],
        out_specs=pl.BlockSpec((tm, tn), lambda j, i: (i, j)),
        out_shape=jax.ShapeDtypeStruct((m, n), F32),
        compiler_params=_params(("arbitrary", "arbitrary")),
        name="in_proj",
    )(x, w)


def _t5_bucket(dist):
    n = jnp.maximum(dist, 0)
    exact = REL_BUCKETS // 2
    nf = jnp.maximum(n, 1).astype(F32)
    large = exact + (jnp.log(nf / exact) / math.log(REL_MAX_DIST / exact)
                     * (REL_BUCKETS - exact)).astype(jnp.int32)
    return jnp.where(n < exact, n, jnp.minimum(large, REL_BUCKETS - 1))


def _far_bucket_is_constant(min_dist):
    exact = REL_BUCKETS // 2
    v = np.float32(min_dist) / np.float32(exact)
    large = exact + int(np.float32(np.log(v)) / np.float32(math.log(REL_MAX_DIST / exact))
                        * (REL_BUCKETS - exact))
    return min_dist >= exact and large >= REL_BUCKETS


def _attn_prompt_kernel(sc_ref, q_ref, k_ref, v_ref, d0_ref, d1_ref, g_ref, o_ref,
                        kb, vb, qs, s_a, s_b, acc, mrow, *, tile, out_scale):
    h = pl.program_id(1)
    qi = pl.program_id(2)

    @pl.when(qi == 0)
    def _():
        kb[...] = k_ref[...].astype(BF16)
        vb[:, :DV_A] = v_ref[...].astype(BF16)
        vb[:, DV_A:] = jnp.ones((vb.shape[0], DV_A), BF16)

    q = q_ref[...] * ATT_SCALE
    lane = lax.broadcasted_iota(jnp.int32, q.shape, 1)
    qs[0:tile, :] = jnp.where(lane < DH_A, q, 0.0).astype(BF16)
    qs[tile:, :] = jnp.where(lane >= DH_A, q, 0.0).astype(BF16)
    acc[...] = jnp.zeros(acc.shape, F32)
    mrow[...] = jnp.full(mrow.shape, NEG_INF, F32)
    cfar = sc_ref[1 + h]
    reps = tile // LANES

    def scores(kj, s_ref):
        start = pl.multiple_of(kj * tile, tile)
        s_ref[...] = lax.dot_general(qs[...], kb[pl.ds(start, tile), :], (((1,), (1,)), ((), ())),
                                     preferred_element_type=F32)

    def softmax_pv(kj, s_ref, bias_ref):
        start = pl.multiple_of(kj * tile, tile)
        vs = vb[pl.ds(start, tile), :]
        s = s_ref[...]
        if bias_ref is None:
            m_cur = jnp.max(s, axis=-1, keepdims=True) + cfar
        else:
            s = s + jnp.concatenate([bias_ref[...], bias_ref[...]], axis=0)
            m_cur = jnp.max(s, axis=-1, keepdims=True)
        m_prev = mrow[...]
        m_new = jnp.maximum(m_prev, m_cur)
        alpha = jnp.exp(m_prev - m_new)
        shift = m_new - cfar if bias_ref is None else m_new
        p = jnp.exp(s - jnp.tile(shift, (1, reps))).astype(BF16)
        acc[...] = jnp.tile(alpha, (1, 2)) * acc[...] + jnp.dot(p, vs, preferred_element_type=F32)
        mrow[...] = m_new

    n_far = qi - 1
    n_pairs = jnp.maximum(n_far, 0) // 2

    @pl.when(qi >= 1)
    def _():
        scores(0, s_a)

    def pair_body(jj, carry):
        j = 2 * jj
        scores(j + 1, s_b)
        softmax_pv(j, s_a, None)
        scores(j + 2, s_a)
        softmax_pv(j + 1, s_b, None)
        return carry

    lax.fori_loop(0, n_pairs, pair_body, 0)

    @pl.when(jnp.logical_and(qi >= 1, n_far % 2 == 1))
    def _():
        scores(qi - 1, s_b)
        softmax_pv(qi - 2, s_a, None)
        scores(qi, s_a)
        softmax_pv(qi - 1, s_b, d1_ref)
        softmax_pv(qi, s_a, d0_ref)

    @pl.when(jnp.logical_and(qi >= 1, n_far % 2 == 0))
    def _():
        scores(qi, s_b)
        softmax_pv(qi - 1, s_a, d1_ref)
        softmax_pv(qi, s_b, d0_ref)

    @pl.when(qi == 0)
    def _():
        scores(0, s_a)
        softmax_pv(0, s_a, d0_ref)

    lam = sc_ref[0]
    a1 = acc[0:tile, :]
    a2 = acc[tile:, :]
    o = a1[:, :DV_A] / a1[:, DV_A:] - lam * (a2[:, :DV_A] / a2[:, DV_A:])
    o_ref[...] = (_rms(o, g_ref[...]) * out_scale).astype(o_ref.dtype)


def _toeplitz(u, t):
    hh = u.shape[0]
    return jnp.tile(u, (1, t))[:, :t * (2 * t - 1)].reshape(hh, t, 2 * t - 1)[:, :, :t]


def _attn_prompt_call(proj, bsz, seq, table, lam, subln, out_scale):
    tile = _pick(seq, 512)
    nq = seq // tile
    assert tile % LANES == 0 and _far_bucket_is_constant(tile + 1)
    bv = table[_t5_bucket(jnp.arange(2 * tile))].astype(F32).T
    neg = jnp.full((H_A, tile), NEG_INF, F32)
    u0 = jnp.concatenate([bv[:, :1], neg, bv[:, 1:tile][:, ::-1]], axis=1)
    u1 = jnp.concatenate([bv[:, tile:tile + 1], bv[:, 1:tile][:, ::-1], neg[:, :1],
                          bv[:, tile + 1:][:, ::-1]], axis=1)
    d0 = _toeplitz(u0, tile)
    d1 = _toeplitz(u1, tile)
    scal = jnp.concatenate([jnp.reshape(lam, (1,)).astype(F32),
                            table[REL_BUCKETS - 1].astype(F32)])
    kq, kk, kv = COL_QA // DV_A, COL_KA // DV_A, COL_VA // DV_A
    kern = functools.partial(_attn_prompt_kernel, tile=tile, out_scale=out_scale)
    return pl.pallas_call(
        kern,
        grid=(bsz, H_A, nq),
        in_specs=[
            pl.BlockSpec(memory_space=pltpu.SMEM),
            pl.BlockSpec((tile, DV_A), lambda b, h, i: (b * nq + i, kq + h)),
            pl.BlockSpec((seq, DV_A), lambda b, h, i: (b, kk + h)),
            pl.BlockSpec((seq, DV_A), lambda b, h, i: (b, kv + h)),
            pl.BlockSpec((None, tile, tile), lambda b, h, i: (h, 0, 0)),
            pl.BlockSpec((None, tile, tile), lambda b, h, i: (h, 0, 0)),
            pl.BlockSpec((1, DV_A), lambda b, h, i: (0, 0)),
        ],
        out_specs=pl.BlockSpec((tile, DV_A), lambda b, h, i: (b * nq + i, h)),
        out_shape=jax.ShapeDtypeStruct((bsz * seq, W_A), BF16),
        scratch_shapes=[pltpu.VMEM((seq, DV_A), BF16),
                        pltpu.VMEM((seq, 2 * DV_A), BF16),
                        pltpu.VMEM((2 * tile, DV_A), BF16),
                        pltpu.VMEM((2 * tile, tile), F32),
                        pltpu.VMEM((2 * tile, tile), F32),
                        pltpu.VMEM((2 * tile, 2 * DV_A), F32),
                        pltpu.VMEM((2 * tile, LANES), F32)],
        compiler_params=_params(("arbitrary", "arbitrary", "arbitrary")),
        name="attn_prompt",
    )(scal, proj, proj, proj, d0, d1, subln.reshape(1, DV_A).astype(F32))


def _attn_sample_kernel(pt_ref, sc_ref, qbd_ref, kn_ref, vn_ref, bl_ref, bf_ref, bn_ref, g_ref,
                        ck_ref, cv_ref, o_ref, st, oacc, vnew, ring, sems,
                        *, layer, n_seq, n_pages, group, n_slots, page, lq, out_scale):
    b = pl.program_id(0)
    past = n_pages * page
    items_per_seq = 2 * n_pages
    n_items = n_seq * items_per_seq
    n_groups = n_pages // group

    def page_copy(cache_ref, page_idx, slot):
        return pltpu.make_async_copy(cache_ref.at[layer, page_idx], ring.at[slot], sems.at[slot])

    def issue(item, slot):
        seq = item // items_per_seq
        it = item % items_per_seq

        @pl.when(jnp.logical_and(item < n_items, it < n_pages))
        def _():
            page_copy(ck_ref, pt_ref[seq * n_pages + it], slot).start()

        @pl.when(jnp.logical_and(item < n_items, it >= n_pages))
        def _():
            page_copy(cv_ref, pt_ref[seq * n_pages + it - n_pages], slot).start()

    @pl.when(b == 0)
    def _():
        for r in range(n_slots):
            issue(jnp.int32(r), r)

    item0 = b * items_per_seq

    def k_group(grp, carry):
        base = (grp % (n_slots // group)) * group
        for r in range(group):
            slot = base + r
            page_copy(ck_ref, pt_ref[b * n_pages + grp * group + r], slot).wait()
            s = jnp.zeros((page, LANES), F32)
            for h in range(H_A):
                s = s + jnp.dot(ring[slot, h].astype(BF16), qbd_ref[h * DV_A:(h + 1) * DV_A, :],
                                preferred_element_type=F32)
            row = pl.multiple_of((grp * group + r) * page, page)
            st[pl.ds(row, page), :] = s
            issue(item0 + grp * group + r + n_slots, slot)
        return carry

    lax.fori_loop(0, n_groups, k_group, 0)

    st[past:past + page, :] = jnp.full((page, LANES), NEG_INF, F32)
    st[past:past + lq, :] = jnp.dot(kn_ref[...].astype(BF16), qbd_ref[...],
                                    preferred_element_type=F32) + bn_ref[...]
    st[past - page:past, :] = st[past - page:past, :] + bl_ref[...]
    bfar = bf_ref[...]
    n_far = (past - page) // page

    def max_body(i, m):
        row = pl.multiple_of(i * page, page)
        return jnp.maximum(m, jnp.max(st[pl.ds(row, page), :], axis=0, keepdims=True))

    m_far = lax.fori_loop(0, n_far, max_body, jnp.full((1, LANES), NEG_INF, F32))
    m_near = jnp.max(st[past - page:past + page, :], axis=0, keepdims=True)
    m = jnp.maximum(m_far + bfar, m_near)

    def exp_body(i, l):
        row = pl.multiple_of(i * page, page)
        p = jnp.exp(st[pl.ds(row, page), :] - (m - bfar))
        st[pl.ds(row, page), :] = p
        return l + jnp.sum(p, axis=0, keepdims=True)

    l_far = lax.fori_loop(0, n_far, exp_body, jnp.zeros((1, LANES), F32))
    p_near = jnp.exp(st[past - page:past + page, :] - m)
    st[past - page:past + page, :] = p_near
    l_lane = l_far + jnp.sum(p_near, axis=0, keepdims=True)
    for h in range(H_A):
        oacc[h] = jnp.zeros((LANES, DV_A), F32)
    vnew[...] = jnp.zeros(vnew.shape, BF16)
    vnew[0:lq, :] = vn_ref[...].astype(BF16)

    def v_group(grp, carry):
        base = ((n_groups + grp) % (n_slots // group)) * group
        tot = [oacc[h] for h in range(H_A)]
        for r in range(group):
            slot = base + r
            page_copy(cv_ref, pt_ref[b * n_pages + grp * group + r], slot).wait()
            row = pl.multiple_of((grp * group + r) * page, page)
            pt_t = st[pl.ds(row, page), :].T.astype(BF16)
            for h in range(H_A):
                tot[h] = tot[h] + jnp.dot(pt_t, ring[slot, h].astype(BF16), preferred_element_type=F32)
            issue(item0 + n_pages + grp * group + r + n_slots, slot)
        for h in range(H_A):
            oacc[h] = tot[h]
        return carry

    lax.fori_loop(0, n_groups, v_group, 0)

    pn_t = st[past:past + page, :].T.astype(BF16)
    rr = lax.broadcasted_iota(jnp.int32, (LANES, LANES), 0)
    cc = lax.broadcasted_iota(jnp.int32, (LANES, LANES), 1)
    l_rows = jnp.sum(jnp.where(rr == cc, jnp.broadcast_to(l_lane, (LANES, LANES)), 0.0),
                     axis=1, keepdims=True)
    lam = sc_ref[0]
    half = LANES // 2
    for h in range(H_A):
        r1 = h * lq
        r2 = half + h * lq
        tot = oacc[h] + jnp.dot(pn_t, vnew[:, h * DV_A:(h + 1) * DV_A], preferred_element_type=F32)
        o1 = tot[r1:r1 + lq] / l_rows[r1:r1 + lq]
        o2 = tot[r2:r2 + lq] / l_rows[r2:r2 + lq]
        o = o1 - lam * o2
        o_ref[:, h * DV_A:(h + 1) * DV_A] = _rms(o, g_ref[...]) * out_scale


def _lane_bias(t, lq, half):
    rows = t.shape[1]
    x = jnp.transpose(t, (1, 0, 2)).reshape(rows, H_A * lq)
    x = jnp.concatenate([x, jnp.zeros((rows, half - H_A * lq), F32)], axis=1)
    return jnp.concatenate([x, x], axis=1)


def _attn_sample_call(proj, cache_k, cache_v, layer, page_table, table, lam, subln, out_scale, lq):
    bs = page_table.shape[0]
    n_pages = page_table.shape[1]
    page = cache_k.shape[2]
    assert page == LANES and lq == 8 and H_A * lq <= LANES // 2
    assert _far_bucket_is_constant(page + 1)
    group = _pick(n_pages, 8)
    n_slots = 2 * group
    assert n_pages % group == 0 and (2 * n_pages) % n_slots == 0
    past = n_pages * page
    half = LANES // 2
    ck = jnp.transpose(cache_k, (0, 1, 3, 2, 4))
    cv = jnp.transpose(cache_v, (0, 1, 3, 2, 4))

    q = proj[:, COL_QA:COL_QA + W_A] * ATT_SCALE
    qt = jnp.transpose(q.reshape(bs, lq, W_A), (0, 2, 1))
    row = np.arange(W_A)[:, None]
    lane = np.arange(LANES)[None, :]
    keep = ((row // DV_A) == (lane % half) // lq) & (((row % DV_A) // DH_A) == lane // half)
    qbd = jnp.where(jnp.asarray(keep)[None], jnp.tile(qt, (1, 1, LANES // lq)), 0.0).astype(BF16)

    bv = table[_t5_bucket(jnp.arange(2 * page))].astype(F32).T
    u_last = jnp.concatenate([bv[:, page:], bv[:, :page]], axis=1)
    b_last = _lane_bias(_toeplitz(u_last, page)[:, :, :lq], lq, half)
    neg = jnp.full((H_A, page), NEG_INF, F32)
    u_new = jnp.concatenate([bv[:, :page], neg], axis=1)
    b_new = _lane_bias(_toeplitz(u_new, page)[:, :lq, :lq], lq, half)
    lane_h = np.minimum((np.arange(LANES) % half) // lq, H_A - 1)
    valid = jnp.asarray((np.arange(LANES) % half) < H_A * lq)
    b_far = jnp.where(valid, table[REL_BUCKETS - 1].astype(F32)[lane_h], 0.0).reshape(1, LANES)
    scal = jnp.reshape(lam, (1,)).astype(F32)
    pt_flat = page_table.reshape(-1).astype(jnp.int32)

    kcol, vcol = COL_KA // W_A, COL_VA // W_A
    in_specs = [
        pl.BlockSpec(memory_space=pltpu.SMEM),
        pl.BlockSpec((None, W_A, LANES), lambda b, pt: (b, 0, 0)),
        pl.BlockSpec((lq, W_A), lambda b, pt: (b, kcol)),
        pl.BlockSpec((lq, W_A), lambda b, pt: (b, vcol)),
        pl.BlockSpec((page, LANES), lambda b, pt: (0, 0)),
        pl.BlockSpec((1, LANES), lambda b, pt: (0, 0)),
        pl.BlockSpec((lq, LANES), lambda b, pt: (0, 0)),
        pl.BlockSpec((1, DV_A), lambda b, pt: (0, 0)),
        pl.BlockSpec(memory_space=pl.ANY),
        pl.BlockSpec(memory_space=pl.ANY),
    ]
    kern = functools.partial(_attn_sample_kernel, layer=layer, n_seq=bs, n_pages=n_pages, group=group,
                             n_slots=n_slots, page=page, lq=lq, out_scale=out_scale)
    grid_spec = pltpu.PrefetchScalarGridSpec(
        num_scalar_prefetch=1,
        grid=(bs,),
        in_specs=in_specs,
        out_specs=pl.BlockSpec((None, lq, W_A), lambda b, pt: (b, 0, 0)),
        scratch_shapes=[pltpu.VMEM((past + page, LANES), F32),
                        pltpu.VMEM((H_A, LANES, DV_A), F32),
                        pltpu.VMEM((page, W_A), BF16),
                        pltpu.VMEM((n_slots, H_A, page, DV_A), F32),
                        pltpu.SemaphoreType.DMA((n_slots,))],
    )
    out = pl.pallas_call(
        kern,
        grid_spec=grid_spec,
        out_shape=jax.ShapeDtypeStruct((bs, lq, W_A), F32),
        compiler_params=_params(("arbitrary",)),
        name="attn_sample",
    )(pt_flat, scal, qbd, proj, proj, b_last, b_far, b_new, subln.reshape(1, DV_A).astype(F32), ck, cv)
    return out.reshape(bs * lq, W_A)


def _log_gamma():
    return np.log(1.0 - 2.0 ** (-5.0 - np.arange(H_B, dtype=np.float64)))


def _retention_constants(chunk_math, rows):
    lg = _log_gamma()
    i = np.arange(rows, dtype=np.float64)
    diff = i[:, None] - i[None, :]
    dmat = np.where(diff >= 0, np.exp(np.maximum(diff, 0.0)[None] * lg[:, None, None]), 0.0)
    xi = np.exp((i[:, None] + 1.0) * lg[None, :])
    zeta = np.exp((chunk_math - 1.0 - i)[:, None] * lg[None, :])
    zeta = np.where(i[:, None] < chunk_math, zeta, 0.0)
    cdec = np.exp(chunk_math * lg)
    xi_l = np.repeat(xi, DV_B, axis=1)
    zeta_l = np.repeat(zeta, DK_B, axis=1) * (DK_B ** -0.5)
    cdec_rows = np.repeat(cdec.reshape(H_B // 2, 2), DK_B, axis=1).reshape(H_B // 2, 2 * DK_B, 1)
    return (jnp.asarray(dmat, F32), jnp.asarray(xi_l, F32), jnp.asarray(zeta_l, F32),
            jnp.asarray(cdec_rows, F32))


def _rope_tables(pos):
    half = DK_B // 2
    inv = ROPE_BASE ** (-jnp.arange(half, dtype=F32) / half)
    ang = pos.astype(F32)[:, None] * inv[None, :]
    cos = jnp.cos(ang)
    sin = jnp.sin(ang)
    cos_l = jnp.tile(jnp.concatenate([cos, cos], axis=1), (1, H_B))
    sin_l = jnp.tile(jnp.concatenate([-sin, sin], axis=1), (1, H_B))
    return cos_l, sin_l


def _mixer_kernel(qb_ref, kb_ref, vb_ref, gb_ref, xc_ref, cos_ref, sin_ref, dmat_ref, xi_ref, zeta_ref,
                  cdec_ref, s0_ref, hist_ref, pw_ref, ps_ref, o_ref, sout_ref, state, ext,
                  *, rows_in, rows, pos0):
    ci = pl.program_id(1)
    n_chunks = pl.num_programs(1)

    @pl.when(ci == 0)
    def _():
        state[...] = s0_ref[...]
        ext[0:POOL_HIST, :] = hist_ref[...]

    def pad_rows(x):
        if rows == rows_in:
            return x
        return jnp.concatenate([x, jnp.zeros((rows - rows_in, x.shape[1]), x.dtype)], axis=0)

    lane = lax.broadcasted_iota(jnp.int32, (rows, LANES), 1)
    first_half = (lane % DK_B) < (DK_B // 2)
    low_head = lane < DK_B

    def rope(x, cos, sin):
        swapped = jnp.where(first_half, pltpu.roll(x, LANES - DK_B // 2, 1), pltpu.roll(x, DK_B // 2, 1))
        return x * cos + swapped * sin

    cos = pad_rows(cos_ref[...])
    sin = pad_rows(sin_ref[...])
    qb = pad_rows(qb_ref[...])
    kb = pad_rows(kb_ref[...])
    vb = pad_rows(vb_ref[...])
    gb = pad_rows(gb_ref[...])
    zeta = zeta_ref[...]
    xi = xi_ref[...]
    for pr in range(H_B // 2):
        sl = slice(pr * LANES, (pr + 1) * LANES)
        q2 = rope(qb[:, sl], cos[:, sl], sin[:, sl])
        k2 = rope(kb[:, sl], cos[:, sl], sin[:, sl])
        kz2 = k2 * zeta[:, sl]
        k2 = k2 * (DK_B ** -0.5)
        s_pair = state[pr]
        s_pair_b = s_pair.astype(BF16)
        upd = jnp.zeros((LANES, DV_B), F32)
        k2b = k2.astype(BF16)
        for hh in range(2):
            h = 2 * pr + hh
            mask = low_head if hh == 0 else jnp.logical_not(low_head)
            qm = jnp.where(mask, q2, 0.0).astype(BF16)
            vh = vb[:, h * DV_B:(h + 1) * DV_B].astype(BF16)
            a = lax.dot_general(qm, k2b, (((1,), (1,)), ((), ())), preferred_element_type=F32)
            a = (a * dmat_ref[h]).astype(BF16)
            o = jnp.dot(a, vh, preferred_element_type=F32)
            o = o + jnp.dot(qm, s_pair_b, preferred_element_type=F32) * xi[:, h * DV_B:(h + 1) * DV_B]
            kzm_t = jnp.where(mask, kz2, 0.0).T.astype(BF16)
            upd = upd + jnp.dot(kzm_t, vh, preferred_element_type=F32)
            g = gb[:, h * DV_B:(h + 1) * DV_B]
            y = _rms(o) * (g / (1.0 + jnp.exp(-g)))
            o_ref[:, h * DV_B:(h + 1) * DV_B] = y[0:rows_in].astype(o_ref.dtype)
        state[pr] = s_pair * cdec_ref[pr] + upd

    @pl.when(ci == n_chunks - 1)
    def _():
        for pr in range(H_B // 2):
            sp = state[pr]
            sout_ref[2 * pr] = sp[0:DK_B]
            sout_ref[2 * pr + 1] = sp[DK_B:]

    xc = xc_ref[...]
    ext[POOL_HIST:POOL_HIST + rows_in, :] = xc
    pos = pos0 + ci * rows_in + lax.broadcasted_iota(jnp.int32, (rows_in, 1), 0)
    for gi, win in enumerate(POOL_WINDOWS):
        sl = slice(gi * POOL_GC, (gi + 1) * POOL_GC)
        tot = xc[:, sl]
        for d in range(1, win):
            tot = tot + ext[POOL_HIST - d:POOL_HIST - d + rows_in, sl]
        cnt = jnp.minimum(pos + 1, win).astype(F32)
        pooled = (tot / cnt - xc[:, sl]).astype(BF16)
        y = jnp.dot(pooled, pw_ref[gi], preferred_element_type=F32) * ps_ref[:, sl]
        o_ref[:, W_B + gi * POOL_GC:W_B + (gi + 1) * POOL_GC] = y.astype(o_ref.dtype)
    ext[0:POOL_HIST, :] = ext[rows_in:rows_in + POOL_HIST, :]


def _mixer_call(proj, bsz, seq, pos0, s0, hist, pool_w, pool_scale, out_dtype):
    rows_in = _pick(seq, 256)
    rows = max(rows_in, LANES)
    n_chunks = seq // rows_in
    assert rows_in % 8 == 0 and (rows_in == rows or n_chunks == 1)
    dmat, xi_l, zeta_l, cdec_rows = _retention_constants(rows_in, rows)
    cos_l, sin_l = _rope_tables(pos0 + jnp.arange(seq))
    s0p = s0.astype(F32).reshape(bsz, H_B // 2, 2 * DK_B, DV_B)
    wq, wv, wc = H_B * DK_B, W_B, W_C
    kern = functools.partial(_mixer_kernel, rows_in=rows_in, rows=rows, pos0=pos0)
    row_map = lambda col: (lambda b, c: (b * n_chunks + c, col))
    const2 = lambda b, c: (0, 0)
    const3 = lambda b, c: (0, 0, 0)
    out, sout = pl.pallas_call(
        kern,
        grid=(bsz, n_chunks),
        in_specs=[
            pl.BlockSpec((rows_in, wq), row_map(COL_QB // wq)),
            pl.BlockSpec((rows_in, wq), row_map(COL_KB // wq)),
            pl.BlockSpec((rows_in, wv), row_map(COL_VB // wv)),
            pl.BlockSpec((rows_in, wv), row_map(COL_GB // wv)),
            pl.BlockSpec((rows_in, wc), row_map(COL_XC // wc)),
            pl.BlockSpec((rows_in, wq), lambda b, c: (c, 0)),
            pl.BlockSpec((rows_in, wq), lambda b, c: (c, 0)),
            pl.BlockSpec((H_B, rows, rows), const3),
            pl.BlockSpec((rows, W_B), const2),
            pl.BlockSpec((rows, wq), const2),
            pl.BlockSpec((H_B // 2, 2 * DK_B, 1), const3),
            pl.BlockSpec((None, H_B // 2, 2 * DK_B, DV_B), lambda b, c: (b, 0, 0, 0)),
            pl.BlockSpec((None, POOL_HIST, W_C), lambda b, c: (b, 0, 0)),
            pl.BlockSpec((N_POOL_GROUPS, POOL_GC, POOL_GC), const3),
            pl.BlockSpec((1, W_C), const2),
        ],
        out_specs=[pl.BlockSpec((rows_in, W_B + W_C), lambda b, c: (b * n_chunks + c, 0)),
                   pl.BlockSpec((None, H_B, DK_B, DV_B), lambda b, c: (b, 0, 0, 0))],
        out_shape=[jax.ShapeDtypeStruct((bsz * seq, W_B + W_C), out_dtype),
                   jax.ShapeDtypeStruct((bsz, H_B, DK_B, DV_B), F32)],
        scratch_shapes=[pltpu.VMEM((H_B // 2, 2 * DK_B, DV_B), F32),
                        pltpu.VMEM((POOL_HIST + rows_in, W_C), F32)],
        compiler_params=_params(("arbitrary", "arbitrary")),
        name="mixer_ret_pool",
    )(proj, proj, proj, proj, proj, cos_l, sin_l, dmat, xi_l, zeta_l, cdec_rows, s0p, hist,
      pool_w.astype(BF16), pool_scale.reshape(1, W_C).astype(F32))
    return out, sout


def _outproj_kernel(oa_ref, bc_ref, w_ref, x_ref, gpost_ref, gnext_ref, x1_ref, h_ref):
    wa = oa_ref.shape[1]
    mix = jnp.dot(oa_ref[...].astype(BF16), w_ref[0:wa, :], preferred_element_type=F32)
    mix = mix + jnp.dot(bc_ref[...].astype(BF16), w_ref[wa:, :], preferred_element_type=F32)
    x1 = x_ref[...] + _rms(mix, gpost_ref[...])
    x1_ref[...] = x1
    h_ref[...] = _rms(x1, gnext_ref[...]).astype(h_ref.dtype)


def _outproj_call(oa, bc, w, x, g_post, g_next):
    m, d = x.shape
    tm = _pick(m, 512)
    row = lambda i: (i, 0)
    const = lambda i: (0, 0)
    return pl.pallas_call(
        _outproj_kernel,
        grid=(m // tm,),
        in_specs=[pl.BlockSpec((tm, oa.shape[1]), row),
                  pl.BlockSpec((tm, bc.shape[1]), row),
                  pl.BlockSpec(w.shape, const),
                  pl.BlockSpec((tm, d), row),
                  pl.BlockSpec((1, d), const),
                  pl.BlockSpec((1, d), const)],
        out_specs=[pl.BlockSpec((tm, d), row), pl.BlockSpec((tm, d), row)],
        out_shape=[jax.ShapeDtypeStruct((m, d), F32), jax.ShapeDtypeStruct((m, d), BF16)],
        compiler_params=_params(("arbitrary",)),
        name="out_proj",
    )(oa, bc, w, x, g_post.reshape(1, d).astype(F32), g_next.reshape(1, d).astype(F32))


def _gelu_tanh(g):
    return 0.5 * g * (1.0 + jnp.tanh(math.sqrt(2.0 / math.pi) * (g + 0.044715 * (g * g * g))))


def _conv_taps(u, tail, cw_ref, cb_ref, cols):
    row8 = lax.broadcasted_iota(jnp.int32, tail.shape, 0)
    r1 = pltpu.roll(u, 1, 0)
    r2 = pltpu.roll(u, 2, 0)
    head1 = jnp.where(row8 < 1, pltpu.roll(tail, 1, 0), r1[0:8])
    head2 = jnp.where(row8 < 2, pltpu.roll(tail, 2, 0), r2[0:8])
    um1 = jnp.concatenate([head1, r1[8:]], axis=0)
    um2 = jnp.concatenate([head2, r2[8:]], axis=0)
    return (um2 * cw_ref[0:1, cols] + um1 * cw_ref[1:2, cols] + u * cw_ref[2:3, cols]) + cb_ref[:, cols]


def _ffn_up_prompt_kernel(h_ref, wa_ref, wg_ref, cwa_ref, cwg_ref, cba_ref, cbg_ref, bufa_ref, bufg_ref,
                          act_ref, lasta_ref, lastg_ref, ta, tg, *, tm, tn, cc, tiles_per_seq):
    i = pl.program_id(1)

    @pl.when(i % tiles_per_seq == 0)
    def _():
        ta[8 - (CONV_W - 1):, :] = bufa_ref[...]
        tg[8 - (CONV_W - 1):, :] = bufg_ref[...]

    hh = h_ref[...]
    for c0 in range(0, tn, cc):
        cols = slice(c0, c0 + cc)
        ua = jnp.dot(hh, wa_ref[:, cols], preferred_element_type=F32)
        ug = jnp.dot(hh, wg_ref[:, cols], preferred_element_type=F32)
        a = _conv_taps(ua, ta[:, cols], cwa_ref, cba_ref, cols)
        g = _conv_taps(ug, tg[:, cols], cwg_ref, cbg_ref, cols)
        act_ref[:, cols] = (a * _gelu_tanh(g)).astype(act_ref.dtype)
        ta[:, cols] = ua[tm - 8:, :]
        tg[:, cols] = ug[tm - 8:, :]
        lasta_ref[:, cols] = ua[tm - (CONV_W - 1):, :]
        lastg_ref[:, cols] = ug[tm - (CONV_W - 1):, :]


def _ffn_up_prompt_call(h, w_up, conv_w, conv_b, conv_buf, bsz, seq):
    m, d = h.shape
    dff = w_up.shape[1] // 2
    tm = _pick(seq, 1024)
    tn = _pick(dff, 512)
    cc = _pick(tn, 256)
    nj = dff // tn
    tiles_per_seq = seq // tm
    kern = functools.partial(_ffn_up_prompt_kernel, tm=tm, tn=tn, cc=cc, tiles_per_seq=tiles_per_seq)
    cb = conv_b.reshape(1, 2 * dff).astype(F32)
    seq_map_a = lambda j, i: (i // tiles_per_seq, 0, j)
    seq_map_g = lambda j, i: (i // tiles_per_seq, 0, j + nj)
    act, last_a, last_g = pl.pallas_call(
        kern,
        grid=(nj, m // tm),
        in_specs=[pl.BlockSpec((tm, d), lambda j, i: (i, 0)),
                  pl.BlockSpec((d, tn), lambda j, i: (0, j)),
                  pl.BlockSpec((d, tn), lambda j, i: (0, j + nj)),
                  pl.BlockSpec((CONV_W, tn), lambda j, i: (0, j)),
                  pl.BlockSpec((CONV_W, tn), lambda j, i: (0, j + nj)),
                  pl.BlockSpec((1, tn), lambda j, i: (0, j)),
                  pl.BlockSpec((1, tn), lambda j, i: (0, j + nj)),
                  pl.BlockSpec((None, CONV_W - 1, tn), seq_map_a),
                  pl.BlockSpec((None, CONV_W - 1, tn), seq_map_g)],
        out_specs=[pl.BlockSpec((tm, tn), lambda j, i: (i, j)),
                   pl.BlockSpec((None, CONV_W - 1, tn), lambda j, i: (i // tiles_per_seq, 0, j)),
                   pl.BlockSpec((None, CONV_W - 1, tn), lambda j, i: (i // tiles_per_seq, 0, j))],
        out_shape=[jax.ShapeDtypeStruct((m, dff), BF16),
                   jax.ShapeDtypeStruct((bsz, CONV_W - 1, dff), F32),
                   jax.ShapeDtypeStruct((bsz, CONV_W - 1, dff), F32)],
        scratch_shapes=[pltpu.VMEM((8, tn), F32), pltpu.VMEM((8, tn), F32)],
        compiler_params=_params(("arbitrary", "arbitrary")),
        name="ffn_up_prompt",
    )(h, w_up, w_up, conv_w, conv_w, cb, cb, conv_buf, conv_buf)
    return act, jnp.concatenate([last_a, last_g], axis=-1)


def _ffn_up_sample_kernel(h_ref, wa_ref, wg_ref, cwa_ref, cwg_ref, cba_ref, cbg_ref,
                          b0a_ref, b0g_ref, b1a_ref, b1g_ref, act_ref, ua_ref, ug_ref, sa, sg, *, m, lq):
    pad = 8
    hh = h_ref[...]
    t = lax.broadcasted_iota(jnp.int32, (m, 1), 0) % lq

    def branch(w_ref, cw_ref, cb_ref, b0_ref, b1_ref, u_ref, s):
        u = jnp.dot(hh, w_ref[...], preferred_element_type=F32)
        u_ref[...] = u
        s[0:pad, :] = jnp.zeros((pad, s.shape[1]), F32)
        s[pad:pad + m, :] = u
        um1 = jnp.where(t == 0, b1_ref[...], s[pad - 1:pad - 1 + m, :])
        um2 = jnp.where(t == 0, b0_ref[...], jnp.where(t == 1, b1_ref[...], s[pad - 2:pad - 2 + m, :]))
        return um2 * cw_ref[0:1, :] + um1 * cw_ref[1:2, :] + u * cw_ref[2:3, :] + cb_ref[...]

    a = branch(wa_ref, cwa_ref, cba_ref, b0a_ref, b1a_ref, ua_ref, sa)
    g = branch(wg_ref, cwg_ref, cbg_ref, b0g_ref, b1g_ref, ug_ref, sg)
    act_ref[...] = (a * _gelu_tanh(g)).astype(act_ref.dtype)


def _ffn_up_sample_call(h, w_up, conv_w, conv_b, conv_buf, bsz, lq):
    m, d = h.shape
    dff = w_up.shape[1] // 2
    tn = _pick(dff, 512)
    nj = dff // tn
    assert lq >= CONV_W - 1
    kern = functools.partial(_ffn_up_sample_kernel, m=m, lq=lq)
    cb = conv_b.reshape(1, 2 * dff).astype(F32)
    b0 = jnp.repeat(conv_buf[:, 0], lq, axis=0)
    b1 = jnp.repeat(conv_buf[:, 1], lq, axis=0)
    col_a = lambda j: (0, j)
    col_g = lambda j: (0, j + nj)
    act, u_a, u_g = pl.pallas_call(
        kern,
        grid=(nj,),
        in_specs=[pl.BlockSpec((m, d), lambda j: (0, 0)),
                  pl.BlockSpec((d, tn), col_a), pl.BlockSpec((d, tn), col_g),
                  pl.BlockSpec((CONV_W, tn), col_a), pl.BlockSpec((CONV_W, tn), col_g),
                  pl.BlockSpec((1, tn), col_a), pl.BlockSpec((1, tn), col_g),
                  pl.BlockSpec((m, tn), col_a), pl.BlockSpec((m, tn), col_g),
                  pl.BlockSpec((m, tn), col_a), pl.BlockSpec((m, tn), col_g)],
        out_specs=[pl.BlockSpec((m, tn), col_a), pl.BlockSpec((m, tn), col_a), pl.BlockSpec((m, tn), col_a)],
        out_shape=[jax.ShapeDtypeStruct((m, dff), BF16),
                   jax.ShapeDtypeStruct((m, dff), F32),
                   jax.ShapeDtypeStruct((m, dff), F32)],
        scratch_shapes=[pltpu.VMEM((8 + m, tn), F32), pltpu.VMEM((8 + m, tn), F32)],
        compiler_params=_params(("arbitrary",)),
        name="ffn_up_sample",
    )(h, w_up, w_up, conv_w, conv_w, cb, cb, b0, b0, b1, b1)
    u = jnp.concatenate([u_a, u_g], axis=-1).reshape(bsz, lq, 2 * dff)
    return act, u[:, lq - (CONV_W - 1):]


def _ffn_down_kernel(act_ref, w_ref, x_ref, gpost_ref, gnext_ref, x2_ref, h_ref, acc):
    k = pl.program_id(1)

    @pl.when(k == 0)
    def _():
        acc[...] = jnp.zeros(acc.shape, F32)

    acc[...] += jnp.dot(act_ref[...], w_ref[...], preferred_element_type=F32)

    @pl.when(k == pl.num_programs(1) - 1)
    def _():
        x2 = x_ref[...] + _rms(acc[...], gpost_ref[...])
        x2_ref[...] = x2
        h_ref[...] = _rms(x2, gnext_ref[...]).astype(h_ref.dtype)


def _ffn_down_call(act, w, x, g_post, g_next):
    m, d = x.shape
    dff = act.shape[1]
    tm = _pick(m, 512)
    tk = dff // 4 if (dff % 4 == 0 and (dff // 4) % LANES == 0) else dff
    row = lambda i, k: (i, 0)
    const = lambda i, k: (0, 0)
    return pl.pallas_call(
        _ffn_down_kernel,
        grid=(m // tm, dff // tk),
        in_specs=[pl.BlockSpec((tm, tk), lambda i, k: (i, k)),
                  pl.BlockSpec((tk, d), lambda i, k: (k, 0)),
                  pl.BlockSpec((tm, d), row),
                  pl.BlockSpec((1, d), const),
                  pl.BlockSpec((1, d), const)],
        out_specs=[pl.BlockSpec((tm, d), row), pl.BlockSpec((tm, d), row)],
        out_shape=[jax.ShapeDtypeStruct((m, d), F32), jax.ShapeDtypeStruct((m, d), BF16)],
        scratch_shapes=[pltpu.VMEM((tm, d), F32)],
        compiler_params=_params(("arbitrary", "arbitrary")),
        name="ffn_down",
    )(act, w, x, g_post.reshape(1, d).astype(F32), g_next.reshape(1, d).astype(F32))


def kernel(x_prompt, x_sample, cache_k, cache_v, state_ret, state_pool, state_conv, page_table,
           rel_bias_table, norm_mix_pre, w_in, lambda_q1, lambda_k1, lambda_q2, lambda_k2, subln_a,
           pool_w, pool_scale, w_out, norm_mix_post, norm_ffn_pre, ffn_w_up, ffn_conv_w, ffn_conv_b,
           ffn_w_down, norm_ffn_post):
    bp, seq, d = x_prompt.shape
    bs, lq, _ = x_sample.shape
    depth = w_in.shape[0]
    dff2 = ffn_w_up.shape[2]
    past_len = page_table.shape[1] * cache_k.shape[2]

    xp = x_prompt.reshape(bp * seq, d)
    xs = x_sample.reshape(bs * lq, d)
    hp = _norm_call(xp, norm_mix_pre[0])
    hs = _norm_call(xs, norm_mix_pre[0])
    zeros_hist_p = jnp.zeros((bp, POOL_HIST, W_C), F32)
    zeros_ret_p = jnp.zeros((bp, H_B, DK_B, DV_B), F32)
    zeros_conv_p = jnp.zeros((bp, CONV_W - 1, dff2), F32)

    outs = {k: [] for k in ('kp', 'vp', 'ks', 'vs', 'rp', 'rs', 'pp', 'ps', 'cp', 'cs')}
    for l in range(depth):
        lam_init = 0.8 - 0.6 * math.exp(-0.3 * l)
        lam = (jnp.exp(jnp.sum(lambda_q1[l].astype(F32) * lambda_k1[l].astype(F32)))
               - jnp.exp(jnp.sum(lambda_q2[l].astype(F32) * lambda_k2[l].astype(F32))) + lam_init)
        w_in_b = w_in[l].astype(BF16)
        w_out_b = w_out[l].astype(BF16)
        w_up_b = ffn_w_up[l].astype(BF16)
        w_down_b = ffn_w_down[l].astype(BF16)
        g_next = norm_mix_pre[l + 1] if l + 1 < depth else norm_mix_pre[l]

        proj_p = _matmul_call(hp, w_in_b)
        oa_p = _attn_prompt_call(proj_p, bp, seq, rel_bias_table, lam, subln_a[l], 1.0 - lam_init)
        bc_p, ret_p = _mixer_call(proj_p, bp, seq, 0, zeros_ret_p, zeros_hist_p, pool_w[l], pool_scale[l], BF16)
        x1_p, h2_p = _outproj_call(oa_p, bc_p, w_out_b, xp, norm_mix_post[l], norm_ffn_pre[l])
        act_p, conv_p = _ffn_up_prompt_call(h2_p, w_up_b, ffn_conv_w[l], ffn_conv_b[l], zeros_conv_p, bp, seq)
        xp, hp = _ffn_down_call(act_p, w_down_b, x1_p, norm_ffn_post[l], g_next)
        proj_p3 = proj_p.reshape(bp, seq, -1)
        outs['kp'].append(proj_p3[:, :, COL_KA:COL_KA + W_A].reshape(bp, seq, H_A, DV_A))
        outs['vp'].append(proj_p3[:, :, COL_VA:COL_VA + W_A].reshape(bp, seq, H_A, DV_A))
        outs['rp'].append(ret_p)
        outs['pp'].append(proj_p3[:, seq - POOL_BUF:, COL_XC:COL_XC + W_C])
        outs['cp'].append(conv_p)

        proj_s = _matmul_call(hs, w_in_b)
        oa_s = _attn_sample_call(proj_s, cache_k, cache_v, l, page_table, rel_bias_table, lam, subln_a[l],
                                 1.0 - lam_init, lq)
        hist_s = jnp.concatenate([jnp.zeros((bs, POOL_HIST - POOL_BUF, W_C), F32),
                                  state_pool[l].astype(F32)], axis=1)
        bc_s, ret_s = _mixer_call(proj_s, bs, lq, past_len, state_ret[l], hist_s, pool_w[l], pool_scale[l], F32)
        x1_s, h2_s = _outproj_call(oa_s, bc_s, w_out_b, xs, norm_mix_post[l], norm_ffn_pre[l])
        act_s, conv_s = _ffn_up_sample_call(h2_s, w_up_b, ffn_conv_w[l], ffn_conv_b[l],
                                            state_conv[l].astype(F32), bs, lq)
        xs, hs = _ffn_down_call(act_s, w_down_b, x1_s, norm_ffn_post[l], g_next)
        proj_s3 = proj_s.reshape(bs, lq, -1)
        outs['ks'].append(proj_s3[:, :, COL_KA:COL_KA + W_A].reshape(bs, lq, H_A, DV_A))
        outs['vs'].append(proj_s3[:, :, COL_VA:COL_VA + W_A].reshape(bs, lq, H_A, DV_A))
        outs['rs'].append(ret_s)
        xe_s = jnp.concatenate([state_pool[l].astype(F32), proj_s3[:, :, COL_XC:COL_XC + W_C]], axis=1)
        outs['ps'].append(xe_s[:, -POOL_BUF:])
        outs['cs'].append(conv_s)

    st = lambda k: jnp.stack(outs[k])
    return (xp.reshape(bp, seq, d), xs.reshape(bs, lq, d),
            st('kp'), st('vp'), st('ks'), st('vs'), st('rp'), st('rs'),
            st('pp'), st('ps'), st('cp'), st('cs'))
```

```python
import functools
import math

import numpy as np
import jax
import jax.numpy as jnp
from jax import lax
from jax.experimental import pallas as pl
from jax.experimental.pallas import tpu as pltpu

F32 = jnp.float32
BF16 = jnp.bfloat16

H_A = 6
DH_A = 64
DV_A = 128
W_A = H_A * DV_A
ATT_SCALE = DH_A ** -0.5
NEG_INF = -1e30
H_B = 6
DK_B = 64
DV_B = 128
W_B = H_B * DV_B
ROPE_BASE = 10000.0
N_POOL_GROUPS = 4
POOL_GC = 128
POOL_WINDOWS = (2, 4, 8, 16)
W_C = N_POOL_GROUPS * POOL_GC
POOL_BUF = 15
POOL_HIST = 16
REL_BUCKETS = 32
REL_MAX_DIST = 128
CONV_W = 3
EPS = 1e-6
COL_QA, COL_KA, COL_VA = 0, W_A, 2 * W_A
COL_QB = 3 * W_A
COL_KB = COL_QB + H_B * DK_B
COL_VB = COL_KB + H_B * DK_B
COL_GB = COL_VB + W_B
COL_XC = COL_GB + W_B

LANES = 128
VMEM_LIMIT = 56 * 1024 * 1024


def _params(sem, vmem=VMEM_LIMIT):
    return pltpu.CompilerParams(dimension_semantics=sem, vmem_limit_bytes=vmem)


def _rms(x, g=None):
    y = x * lax.rsqrt(jnp.mean(x * x, axis=-1, keepdims=True) + EPS)
    return y if g is None else y * g


def _pick(n, pref):
    if n <= pref:
        return n
    t = pref
    while n % t:
        t //= 2
    return t


def _norm_kernel(x_ref, g_ref, o_ref):
    o_ref[...] = _rms(x_ref[...], g_ref[...]).astype(o_ref.dtype)


def _norm_call(x, g):
    m, d = x.shape
    tm = _pick(m, 512)
    return pl.pallas_call(
        _norm_kernel,
        grid=(m // tm,),
        in_specs=[pl.BlockSpec((tm, d), lambda i: (i, 0)),
                  pl.BlockSpec((1, d), lambda i: (0, 0))],
        out_specs=pl.BlockSpec((tm, d), lambda i: (i, 0)),
        out_shape=jax.ShapeDtypeStruct((m, d), BF16),
        compiler_params=_params(("arbitrary",)),
        name="rmsnorm",
    )(x, g.reshape(1, d))


def _matmul_kernel(x_ref, w_ref, o_ref):
    o_ref[...] = jnp.dot(x_ref[...], w_ref[...], preferred_element_type=F32)


def _matmul_call(x, w, layer):
    m, k = x.shape
    n = w.shape[2]
    tm = _pick(m, 1024)
    tn = _pick(n, 1024)
    return pl.pallas_call(
        _matmul_kernel,
        grid=(n // tn, m // tm),
        in_specs=[pl.BlockSpec((tm, k), lambda j, i: (i, 0)),
                  pl.BlockSpec((None, k, tn), lambda j, i: (layer, 0, j))],
        out_specs=pl.BlockSpec((tm, tn), lambda j, i: (i, j)),
        out_shape=jax.ShapeDtypeStruct((m, n), F32),
        compiler_params=_params(("arbitrary", "arbitrary")),
        name="in_proj",
    )(x, w)


def _t5_bucket(dist):
    n = jnp.maximum(dist, 0)
    exact = REL_BUCKETS // 2
    nf = jnp.maximum(n, 1).astype(F32)
    large = exact + (jnp.log(nf / exact) / math.log(REL_MAX_DIST / exact)
                     * (REL_BUCKETS - exact)).astype(jnp.int32)
    return jnp.where(n < exact, n, jnp.minimum(large, REL_BUCKETS - 1))


def _far_bucket_is_constant(min_dist):
    exact = REL_BUCKETS // 2
    v = np.float32(min_dist) / np.float32(exact)
    large = exact + int(np.float32(np.log(v)) / np.float32(math.log(REL_MAX_DIST / exact))
                        * (REL_BUCKETS - exact))
    return min_dist >= exact and large >= REL_BUCKETS


def _attn_prompt_kernel(sc_ref, q_ref, k_ref, v_ref, d0_ref, d1_ref, g_ref, o_ref,
                        kb, vb, qs, s_a, s_b, acc, mrow, *, tile, out_scale):
    h = pl.program_id(1)
    qi = pl.program_id(2)

    @pl.when(qi == 0)
    def _():
        kb[...] = k_ref[...].astype(BF16)
        vb[:, :DV_A] = v_ref[...].astype(BF16)
        vb[:, DV_A:] = jnp.ones((vb.shape[0], DV_A), BF16)

    q = q_ref[...] * ATT_SCALE
    lane = lax.broadcasted_iota(jnp.int32, q.shape, 1)
    qs[0:tile, :] = jnp.where(lane < DH_A, q, 0.0).astype(BF16)
    qs[tile:, :] = jnp.where(lane >= DH_A, q, 0.0).astype(BF16)
    acc[...] = jnp.zeros(acc.shape, F32)
    mrow[...] = jnp.full(mrow.shape, NEG_INF, F32)
    cfar = sc_ref[1 + h]
    reps = tile // LANES

    def scores(kj, s_ref):
        start = pl.multiple_of(kj * tile, tile)
        s_ref[...] = lax.dot_general(qs[...], kb[pl.ds(start, tile), :], (((1,), (1,)), ((), ())),
                                     preferred_element_type=F32)

    def softmax_pv(kj, s_ref, bias_ref):
        start = pl.multiple_of(kj * tile, tile)
        vs = vb[pl.ds(start, tile), :]
        s = s_ref[...]
        if bias_ref is None:
            m_cur = jnp.max(s, axis=-1, keepdims=True) + cfar
        else:
            s = s + jnp.concatenate([bias_ref[...], bias_ref[...]], axis=0)
            m_cur = jnp.max(s, axis=-1, keepdims=True)
        m_prev = mrow[...]
        m_new = jnp.maximum(m_prev, m_cur)
        alpha = jnp.exp(m_prev - m_new)
        shift = m_new - cfar if bias_ref is None else m_new
        p = jnp.exp(s - jnp.tile(shift, (1, reps))).astype(BF16)
        acc[...] = jnp.tile(alpha, (1, 2)) * acc[...] + jnp.dot(p, vs, preferred_element_type=F32)
        mrow[...] = m_new

    n_far = qi - 1
    n_pairs = jnp.maximum(n_far, 0) // 2

    @pl.when(qi >= 1)
    def _():
        scores(0, s_a)

    def pair_body(jj, carry):
        j = 2 * jj
        scores(j + 1, s_b)
        softmax_pv(j, s_a, None)
        scores(j + 2, s_a)
        softmax_pv(j + 1, s_b, None)
        return carry

    lax.fori_loop(0, n_pairs, pair_body, 0)

    @pl.when(jnp.logical_and(qi >= 1, n_far % 2 == 1))
    def _():
        scores(qi - 1, s_b)
        softmax_pv(qi - 2, s_a, None)
        scores(qi, s_a)
        softmax_pv(qi - 1, s_b, d1_ref)
        softmax_pv(qi, s_a, d0_ref)

    @pl.when(jnp.logical_and(qi >= 1, n_far % 2 == 0))
    def _():
        scores(qi, s_b)
        softmax_pv(qi - 1, s_a, d1_ref)
        softmax_pv(qi, s_b, d0_ref)

    @pl.when(qi == 0)
    def _():
        scores(0, s_a)
        softmax_pv(0, s_a, d0_ref)

    lam = sc_ref[0]
    a1 = acc[0:tile, :]
    a2 = acc[tile:, :]
    o = a1[:, :DV_A] / a1[:, DV_A:] - lam * (a2[:, :DV_A] / a2[:, DV_A:])
    o_ref[...] = (_rms(o, g_ref[...]) * out_scale).astype(o_ref.dtype)


def _toeplitz(u, t):
    hh = u.shape[0]
    return jnp.tile(u, (1, t))[:, :t * (2 * t - 1)].reshape(hh, t, 2 * t - 1)[:, :, :t]


def _attn_prompt_call(proj, bsz, seq, table, lam, subln, out_scale):
    tile = _pick(seq, 512)
    nq = seq // tile
    assert tile % LANES == 0 and _far_bucket_is_constant(tile + 1)
    bv = table[_t5_bucket(jnp.arange(2 * tile))].astype(F32).T
    neg = jnp.full((H_A, tile), NEG_INF, F32)
    u0 = jnp.concatenate([bv[:, :1], neg, bv[:, 1:tile][:, ::-1]], axis=1)
    u1 = jnp.concatenate([bv[:, tile:tile + 1], bv[:, 1:tile][:, ::-1], neg[:, :1],
                          bv[:, tile + 1:][:, ::-1]], axis=1)
    d0 = _toeplitz(u0, tile)
    d1 = _toeplitz(u1, tile)
    scal = jnp.concatenate([jnp.reshape(lam, (1,)).astype(F32),
                            table[REL_BUCKETS - 1].astype(F32)])
    kq, kk, kv = COL_QA // DV_A, COL_KA // DV_A, COL_VA // DV_A
    kern = functools.partial(_attn_prompt_kernel, tile=tile, out_scale=out_scale)
    return pl.pallas_call(
        kern,
        grid=(bsz, H_A, nq),
        in_specs=[
            pl.BlockSpec(memory_space=pltpu.SMEM),
            pl.BlockSpec((tile, DV_A), lambda b, h, i: (b * nq + i, kq + h)),
            pl.BlockSpec((seq, DV_A), lambda b, h, i: (b, kk + h)),
            pl.BlockSpec((seq, DV_A), lambda b, h, i: (b, kv + h)),
            pl.BlockSpec((None, tile, tile), lambda b, h, i: (h, 0, 0)),
            pl.BlockSpec((None, tile, tile), lambda b, h, i: (h, 0, 0)),
            pl.BlockSpec((1, DV_A), lambda b, h, i: (0, 0)),
        ],
        out_specs=pl.BlockSpec((tile, DV_A), lambda b, h, i: (b * nq + i, h)),
        out_shape=jax.ShapeDtypeStruct((bsz * seq, W_A), BF16),
        scratch_shapes=[pltpu.VMEM((seq, DV_A), BF16),
                        pltpu.VMEM((seq, 2 * DV_A), BF16),
                        pltpu.VMEM((2 * tile, DV_A), BF16),
                        pltpu.VMEM((2 * tile, tile), F32),
                        pltpu.VMEM((2 * tile, tile), F32),
                        pltpu.VMEM((2 * tile, 2 * DV_A), F32),
                        pltpu.VMEM((2 * tile, LANES), F32)],
        compiler_params=_params(("arbitrary", "arbitrary", "arbitrary")),
        name="attn_prompt",
    )(scal, proj, proj, proj, d0, d1, subln.reshape(1, DV_A).astype(F32))


def _attn_sample_kernel(pt_ref, sc_ref, qbd_ref, kn_ref, vn_ref, bl_ref, bf_ref, bn_ref, g_ref,
                        ck_ref, cv_ref, o_ref, st, oacc, vnew, ring, sems,
                        *, layer, n_seq, n_pages, group, n_slots, page, lq, out_scale):
    b = pl.program_id(0)
    past = n_pages * page
    items_per_seq = 2 * n_pages
    n_items = n_seq * items_per_seq
    n_groups = n_pages // group

    def page_copy(cache_ref, page_idx, slot):
        return pltpu.make_async_copy(cache_ref.at[layer, page_idx], ring.at[slot], sems.at[slot])

    def issue(item, slot):
        seq = item // items_per_seq
        it = item % items_per_seq

        @pl.when(jnp.logical_and(item < n_items, it < n_pages))
        def _():
            page_copy(ck_ref, pt_ref[seq * n_pages + it], slot).start()

        @pl.when(jnp.logical_and(item < n_items, it >= n_pages))
        def _():
            page_copy(cv_ref, pt_ref[seq * n_pages + it - n_pages], slot).start()

    @pl.when(b == 0)
    def _():
        for r in range(n_slots):
            issue(jnp.int32(r), r)

    item0 = b * items_per_seq

    def group_pages(base):
        return jnp.concatenate(
            [jnp.concatenate([ring[base + r, h].astype(BF16) for h in range(H_A)], axis=1)
             for r in range(group)], axis=0)

    def k_group(grp, carry):
        base = (grp % (n_slots // group)) * group
        for r in range(group):
            page_copy(ck_ref, pt_ref[b * n_pages + grp * group + r], base + r).wait()
        row = pl.multiple_of(grp * group * page, group * page)
        st[pl.ds(row, group * page), :] = jnp.dot(group_pages(base), qbd_ref[...],
                                                   preferred_element_type=F32)
        for r in range(group):
            issue(item0 + grp * group + r + n_slots, base + r)
        return carry

    lax.fori_loop(0, n_groups, k_group, 0)

    st[past:past + page, :] = jnp.full((page, LANES), NEG_INF, F32)
    st[past:past + lq, :] = jnp.dot(kn_ref[...].astype(BF16), qbd_ref[...],
                                    preferred_element_type=F32) + bn_ref[...]
    st[past - page:past, :] = st[past - page:past, :] + bl_ref[...]
    bfar = bf_ref[...]
    n_far = (past - page) // page

    def max_body(i, m):
        row = pl.multiple_of(i * page, page)
        return jnp.maximum(m, jnp.max(st[pl.ds(row, page), :], axis=0, keepdims=True))

    m_far = lax.fori_loop(0, n_far, max_body, jnp.full((1, LANES), NEG_INF, F32))
    m_near = jnp.max(st[past - page:past + page, :], axis=0, keepdims=True)
    m = jnp.maximum(m_far + bfar, m_near)

    def exp_body(i, l):
        row = pl.multiple_of(i * page, page)
        p = jnp.exp(st[pl.ds(row, page), :] - (m - bfar))
        st[pl.ds(row, page), :] = p
        return l + jnp.sum(p, axis=0, keepdims=True)

    l_far = lax.fori_loop(0, n_far, exp_body, jnp.zeros((1, LANES), F32))
    p_near = jnp.exp(st[past - page:past + page, :] - m)
    st[past - page:past + page, :] = p_near
    l_lane = l_far + jnp.sum(p_near, axis=0, keepdims=True)
    oacc[...] = jnp.zeros(oacc.shape, F32)
    vnew[...] = jnp.zeros(vnew.shape, BF16)
    vnew[0:lq, :] = vn_ref[...].astype(BF16)

    def v_group(grp, carry):
        base = ((n_groups + grp) % (n_slots // group)) * group
        for r in range(group):
            page_copy(cv_ref, pt_ref[b * n_pages + grp * group + r], base + r).wait()
        row = pl.multiple_of(grp * group * page, group * page)
        pt_t = st[pl.ds(row, group * page), :].T.astype(BF16)
        oacc[...] += jnp.dot(pt_t, group_pages(base), preferred_element_type=F32)
        for r in range(group):
            issue(item0 + n_pages + grp * group + r + n_slots, base + r)
        return carry

    lax.fori_loop(0, n_groups, v_group, 0)

    pn_t = st[past:past + page, :].T.astype(BF16)
    rr = lax.broadcasted_iota(jnp.int32, (LANES, LANES), 0)
    cc = lax.broadcasted_iota(jnp.int32, (LANES, LANES), 1)
    l_rows = jnp.sum(jnp.where(rr == cc, jnp.broadcast_to(l_lane, (LANES, LANES)), 0.0),
                     axis=1, keepdims=True)
    lam = sc_ref[0]
    half = LANES // 2
    tot_all = oacc[...] + jnp.dot(pn_t, vnew[...], preferred_element_type=F32)
    for h in range(H_A):
        r1 = h * lq
        r2 = half + h * lq
        tot = tot_all[:, h * DV_A:(h + 1) * DV_A]
        o1 = tot[r1:r1 + lq] / l_rows[r1:r1 + lq]
        o2 = tot[r2:r2 + lq] / l_rows[r2:r2 + lq]
        o = o1 - lam * o2
        o_ref[:, h * DV_A:(h + 1) * DV_A] = _rms(o, g_ref[...]) * out_scale


def _lane_bias(t, lq, half):
    rows = t.shape[1]
    x = jnp.transpose(t, (1, 0, 2)).reshape(rows, H_A * lq)
    x = jnp.concatenate([x, jnp.zeros((rows, half - H_A * lq), F32)], axis=1)
    return jnp.concatenate([x, x], axis=1)


def _attn_sample_call(proj, cache_k, cache_v, layer, page_table, table, lam, subln, out_scale, lq):
    bs = page_table.shape[0]
    n_pages = page_table.shape[1]
    page = cache_k.shape[2]
    assert page == LANES and lq == 8 and H_A * lq <= LANES // 2
    assert _far_bucket_is_constant(page + 1)
    group = _pick(n_pages, 8)
    n_slots = 2 * group
    assert n_pages % group == 0 and (2 * n_pages) % n_slots == 0
    past = n_pages * page
    half = LANES // 2
    ck = jnp.transpose(cache_k, (0, 1, 3, 2, 4))
    cv = jnp.transpose(cache_v, (0, 1, 3, 2, 4))

    q = proj[:, COL_QA:COL_QA + W_A] * ATT_SCALE
    qt = jnp.transpose(q.reshape(bs, lq, W_A), (0, 2, 1))
    row = np.arange(W_A)[:, None]
    lane = np.arange(LANES)[None, :]
    keep = ((row // DV_A) == (lane % half) // lq) & (((row % DV_A) // DH_A) == lane // half)
    qbd = jnp.where(jnp.asarray(keep)[None], jnp.tile(qt, (1, 1, LANES // lq)), 0.0).astype(BF16)

    bv = table[_t5_bucket(jnp.arange(2 * page))].astype(F32).T
    u_last = jnp.concatenate([bv[:, page:], bv[:, :page]], axis=1)
    b_last = _lane_bias(_toeplitz(u_last, page)[:, :, :lq], lq, half)
    neg = jnp.full((H_A, page), NEG_INF, F32)
    u_new = jnp.concatenate([bv[:, :page], neg], axis=1)
    b_new = _lane_bias(_toeplitz(u_new, page)[:, :lq, :lq], lq, half)
    lane_h = np.minimum((np.arange(LANES) % half) // lq, H_A - 1)
    valid = jnp.asarray((np.arange(LANES) % half) < H_A * lq)
    b_far = jnp.where(valid, table[REL_BUCKETS - 1].astype(F32)[lane_h], 0.0).reshape(1, LANES)
    scal = jnp.reshape(lam, (1,)).astype(F32)
    pt_flat = page_table.reshape(-1).astype(jnp.int32)

    kcol, vcol = COL_KA // W_A, COL_VA // W_A
    in_specs = [
        pl.BlockSpec(memory_space=pltpu.SMEM),
        pl.BlockSpec((None, W_A, LANES), lambda b, pt: (b, 0, 0)),
        pl.BlockSpec((lq, W_A), lambda b, pt: (b, kcol)),
        pl.BlockSpec((lq, W_A), lambda b, pt: (b, vcol)),
        pl.BlockSpec((page, LANES), lambda b, pt: (0, 0)),
        pl.BlockSpec((1, LANES), lambda b, pt: (0, 0)),
        pl.BlockSpec((lq, LANES), lambda b, pt: (0, 0)),
        pl.BlockSpec((1, DV_A), lambda b, pt: (0, 0)),
        pl.BlockSpec(memory_space=pl.ANY),
        pl.BlockSpec(memory_space=pl.ANY),
    ]
    kern = functools.partial(_attn_sample_kernel, layer=layer, n_seq=bs, n_pages=n_pages, group=group,
                             n_slots=n_slots, page=page, lq=lq, out_scale=out_scale)
    grid_spec = pltpu.PrefetchScalarGridSpec(
        num_scalar_prefetch=1,
        grid=(bs,),
        in_specs=in_specs,
        out_specs=pl.BlockSpec((None, lq, W_A), lambda b, pt: (b, 0, 0)),
        scratch_shapes=[pltpu.VMEM((past + page, LANES), F32),
                        pltpu.VMEM((LANES, W_A), F32),
                        pltpu.VMEM((page, W_A), BF16),
                        pltpu.VMEM((n_slots, H_A, page, DV_A), F32),
                        pltpu.SemaphoreType.DMA((n_slots,))],
    )
    out = pl.pallas_call(
        kern,
        grid_spec=grid_spec,
        out_shape=jax.ShapeDtypeStruct((bs, lq, W_A), F32),
        compiler_params=_params(("arbitrary",)),
        name="attn_sample",
    )(pt_flat, scal, qbd, proj, proj, b_last, b_far, b_new, subln.reshape(1, DV_A).astype(F32), ck, cv)
    return out.reshape(bs * lq, W_A)


def _log_gamma():
    return np.log(1.0 - 2.0 ** (-5.0 - np.arange(H_B, dtype=np.float64)))


def _retention_constants(chunk_math, rows):
    lg = _log_gamma()
    i = np.arange(rows, dtype=np.float64)
    diff = i[:, None] - i[None, :]
    dmat = np.where(diff >= 0, np.exp(np.maximum(diff, 0.0)[None] * lg[:, None, None]), 0.0)
    xi = np.exp((i[:, None] + 1.0) * lg[None, :])
    zeta = np.exp((chunk_math - 1.0 - i)[:, None] * lg[None, :])
    zeta = np.where(i[:, None] < chunk_math, zeta, 0.0)
    cdec = np.exp(chunk_math * lg)
    xi_l = np.repeat(xi, DV_B, axis=1)
    zeta_l = np.repeat(zeta, DK_B, axis=1) * (DK_B ** -0.5)
    cdec_rows = np.repeat(cdec.reshape(H_B // 2, 2), DK_B, axis=1).reshape(H_B // 2, 2 * DK_B, 1)
    return (jnp.asarray(dmat, F32), jnp.asarray(xi_l, F32), jnp.asarray(zeta_l, F32),
            jnp.asarray(cdec_rows, F32))


def _rope_tables(pos):
    half = DK_B // 2
    inv = ROPE_BASE ** (-jnp.arange(half, dtype=F32) / half)
    ang = pos.astype(F32)[:, None] * inv[None, :]
    cos = jnp.cos(ang)
    sin = jnp.sin(ang)
    cos_l = jnp.tile(jnp.concatenate([cos, cos], axis=1), (1, H_B))
    sin_l = jnp.tile(jnp.concatenate([-sin, sin], axis=1), (1, H_B))
    return cos_l, sin_l


def _mixer_kernel(qb_ref, kb_ref, vb_ref, gb_ref, xc_ref, cos_ref, sin_ref, dmat_ref, xi_ref, zeta_ref,
                  cdec_ref, s0_ref, hist_ref, pw_ref, ps_ref, o_ref, sout_ref, state, ext,
                  *, rows_in, rows, pos0):
    ci = pl.program_id(1)
    n_chunks = pl.num_programs(1)

    @pl.when(ci == 0)
    def _():
        state[...] = s0_ref[...]
        ext[0:POOL_HIST, :] = hist_ref[...]

    def pad_rows(x):
        if rows == rows_in:
            return x
        return jnp.concatenate([x, jnp.zeros((rows - rows_in, x.shape[1]), x.dtype)], axis=0)

    lane = lax.broadcasted_iota(jnp.int32, (rows, LANES), 1)
    first_half = (lane % DK_B) < (DK_B // 2)
    low_head = lane < DK_B

    def rope(x, cos, sin):
        swapped = jnp.where(first_half, pltpu.roll(x, LANES - DK_B // 2, 1), pltpu.roll(x, DK_B // 2, 1))
        return x * cos + swapped * sin

    cos = pad_rows(cos_ref[...])
    sin = pad_rows(sin_ref[...])
    qb = pad_rows(qb_ref[...])
    kb = pad_rows(kb_ref[...])
    vb = pad_rows(vb_ref[...])
    gb = pad_rows(gb_ref[...])
    zeta = zeta_ref[...]
    xi = xi_ref[...]
    for pr in range(H_B // 2):
        sl = slice(pr * LANES, (pr + 1) * LANES)
        q2 = rope(qb[:, sl], cos[:, sl], sin[:, sl])
        k2 = rope(kb[:, sl], cos[:, sl], sin[:, sl])
        kz2 = k2 * zeta[:, sl]
        k2 = k2 * (DK_B ** -0.5)
        s_pair = state[pr]
        s_pair_b = s_pair.astype(BF16)
        upd = jnp.zeros((LANES, DV_B), F32)
        k2b = k2.astype(BF16)
        for hh in range(2):
            h = 2 * pr + hh
            mask = low_head if hh == 0 else jnp.logical_not(low_head)
            qm = jnp.where(mask, q2, 0.0).astype(BF16)
            vh = vb[:, h * DV_B:(h + 1) * DV_B].astype(BF16)
            a = lax.dot_general(qm, k2b, (((1,), (1,)), ((), ())), preferred_element_type=F32)
            a = (a * dmat_ref[h]).astype(BF16)
            o = jnp.dot(a, vh, preferred_element_type=F32)
            o = o + jnp.dot(qm, s_pair_b, preferred_element_type=F32) * xi[:, h * DV_B:(h + 1) * DV_B]
            kzm_t = jnp.where(mask, kz2, 0.0).T.astype(BF16)
            upd = upd + jnp.dot(kzm_t, vh, preferred_element_type=F32)
            g = gb[:, h * DV_B:(h + 1) * DV_B]
            y = _rms(o) * (g / (1.0 + jnp.exp(-g)))
            o_ref[:, h * DV_B:(h + 1) * DV_B] = y[0:rows_in].astype(o_ref.dtype)
        state[pr] = s_pair * cdec_ref[pr] + upd

    @pl.when(ci == n_chunks - 1)
    def _():
        for pr in range(H_B // 2):
            sp = state[pr]
            sout_ref[2 * pr] = sp[0:DK_B]
            sout_ref[2 * pr + 1] = sp[DK_B:]

    xc = xc_ref[...]
    ext[POOL_HIST:POOL_HIST + rows_in, :] = xc
    pos = pos0 + ci * rows_in + lax.broadcasted_iota(jnp.int32, (rows_in, 1), 0)
    for gi, win in enumerate(POOL_WINDOWS):
        sl = slice(gi * POOL_GC, (gi + 1) * POOL_GC)
        tot = xc[:, sl]
        for d in range(1, win):
            tot = tot + ext[POOL_HIST - d:POOL_HIST - d + rows_in, sl]
        cnt = jnp.minimum(pos + 1, win).astype(F32)
        pooled = (tot / cnt - xc[:, sl]).astype(BF16)
        y = jnp.dot(pooled, pw_ref[gi], preferred_element_type=F32) * ps_ref[:, sl]
        o_ref[:, W_B + gi * POOL_GC:W_B + (gi + 1) * POOL_GC] = y.astype(o_ref.dtype)
    ext[0:POOL_HIST, :] = ext[rows_in:rows_in + POOL_HIST, :]


def _mixer_call(proj, bsz, seq, pos0, s0, hist, pool_w, pool_scale, out_dtype):
    rows_in = _pick(seq, 256)
    rows = max(rows_in, LANES)
    n_chunks = seq // rows_in
    assert rows_in % 8 == 0 and (rows_in == rows or n_chunks == 1)
    dmat, xi_l, zeta_l, cdec_rows = _retention_constants(rows_in, rows)
    cos_l, sin_l = _rope_tables(pos0 + jnp.arange(seq))
    s0p = s0.astype(F32).reshape(bsz, H_B // 2, 2 * DK_B, DV_B)
    wq, wv, wc = H_B * DK_B, W_B, W_C
    kern = functools.partial(_mixer_kernel, rows_in=rows_in, rows=rows, pos0=pos0)
    row_map = lambda col: (lambda b, c: (b * n_chunks + c, col))
    const2 = lambda b, c: (0, 0)
    const3 = lambda b, c: (0, 0, 0)
    out, sout = pl.pallas_call(
        kern,
        grid=(bsz, n_chunks),
        in_specs=[
            pl.BlockSpec((rows_in, wq), row_map(COL_QB // wq)),
            pl.BlockSpec((rows_in, wq), row_map(COL_KB // wq)),
            pl.BlockSpec((rows_in, wv), row_map(COL_VB // wv)),
            pl.BlockSpec((rows_in, wv), row_map(COL_GB // wv)),
            pl.BlockSpec((rows_in, wc), row_map(COL_XC // wc)),
            pl.BlockSpec((rows_in, wq), lambda b, c: (c, 0)),
            pl.BlockSpec((rows_in, wq), lambda b, c: (c, 0)),
            pl.BlockSpec((H_B, rows, rows), const3),
            pl.BlockSpec((rows, W_B), const2),
            pl.BlockSpec((rows, wq), const2),
            pl.BlockSpec((H_B // 2, 2 * DK_B, 1), const3),
            pl.BlockSpec((None, H_B // 2, 2 * DK_B, DV_B), lambda b, c: (b, 0, 0, 0)),
            pl.BlockSpec((None, POOL_HIST, W_C), lambda b, c: (b, 0, 0)),
            pl.BlockSpec((N_POOL_GROUPS, POOL_GC, POOL_GC), const3),
            pl.BlockSpec((1, W_C), const2),
        ],
        out_specs=[pl.BlockSpec((rows_in, W_B + W_C), lambda b, c: (b * n_chunks + c, 0)),
                   pl.BlockSpec((None, H_B, DK_B, DV_B), lambda b, c: (b, 0, 0, 0))],
        out_shape=[jax.ShapeDtypeStruct((bsz * seq, W_B + W_C), out_dtype),
                   jax.ShapeDtypeStruct((bsz, H_B, DK_B, DV_B), F32)],
        scratch_shapes=[pltpu.VMEM((H_B // 2, 2 * DK_B, DV_B), F32),
                        pltpu.VMEM((POOL_HIST + rows_in, W_C), F32)],
        compiler_params=_params(("arbitrary", "arbitrary")),
        name="mixer_ret_pool",
    )(proj, proj, proj, proj, proj, cos_l, sin_l, dmat, xi_l, zeta_l, cdec_rows, s0p, hist,
      pool_w.astype(BF16), pool_scale.reshape(1, W_C).astype(F32))
    return out, sout


def _outproj_kernel(oa_ref, bc_ref, w_ref, x_ref, gpost_ref, gnext_ref, x1_ref, h_ref):
    wa = oa_ref.shape[1]
    mix = jnp.dot(oa_ref[...].astype(BF16), w_ref[0:wa, :], preferred_element_type=F32)
    mix = mix + jnp.dot(bc_ref[...].astype(BF16), w_ref[wa:, :], preferred_element_type=F32)
    x1 = x_ref[...] + _rms(mix, gpost_ref[...])
    x1_ref[...] = x1
    h_ref[...] = _rms(x1, gnext_ref[...]).astype(h_ref.dtype)


def _outproj_call(oa, bc, w, layer, x, g_post, g_next):
    m, d = x.shape
    tm = _pick(m, 512)
    row = lambda i: (i, 0)
    const = lambda i: (0, 0)
    return pl.pallas_call(
        _outproj_kernel,
        grid=(m // tm,),
        in_specs=[pl.BlockSpec((tm, oa.shape[1]), row),
                  pl.BlockSpec((tm, bc.shape[1]), row),
                  pl.BlockSpec((None,) + w.shape[1:], lambda i: (layer, 0, 0)),
                  pl.BlockSpec((tm, d), row),
                  pl.BlockSpec((1, d), const),
                  pl.BlockSpec((1, d), const)],
        out_specs=[pl.BlockSpec((tm, d), row), pl.BlockSpec((tm, d), row)],
        out_shape=[jax.ShapeDtypeStruct((m, d), F32), jax.ShapeDtypeStruct((m, d), BF16)],
        compiler_params=_params(("arbitrary",)),
        name="out_proj",
    )(oa, bc, w, x, g_post.reshape(1, d).astype(F32), g_next.reshape(1, d).astype(F32))


def _gelu_tanh(g):
    return 0.5 * g * (1.0 + jnp.tanh(math.sqrt(2.0 / math.pi) * (g + 0.044715 * (g * g * g))))


def _conv_taps(u, tail, cw_ref, cb_ref, cols):
    row8 = lax.broadcasted_iota(jnp.int32, tail.shape, 0)
    r1 = pltpu.roll(u, 1, 0)
    r2 = pltpu.roll(u, 2, 0)
    head1 = jnp.where(row8 < 1, pltpu.roll(tail, 1, 0), r1[0:8])
    head2 = jnp.where(row8 < 2, pltpu.roll(tail, 2, 0), r2[0:8])
    um1 = jnp.concatenate([head1, r1[8:]], axis=0)
    um2 = jnp.concatenate([head2, r2[8:]], axis=0)
    return (um2 * cw_ref[0:1, cols] + um1 * cw_ref[1:2, cols] + u * cw_ref[2:3, cols]) + cb_ref[:, cols]


def _ffn_up_prompt_kernel(h_ref, wa_ref, wg_ref, cwa_ref, cwg_ref, cba_ref, cbg_ref, bufa_ref, bufg_ref,
                          act_ref, lasta_ref, lastg_ref, ta, tg, *, tm, tn, cc, tiles_per_seq):
    i = pl.program_id(1)

    @pl.when(i % tiles_per_seq == 0)
    def _():
        ta[8 - (CONV_W - 1):, :] = bufa_ref[...]
        tg[8 - (CONV_W - 1):, :] = bufg_ref[...]

    hh = h_ref[...]
    for c0 in range(0, tn, cc):
        cols = slice(c0, c0 + cc)
        ua = jnp.dot(hh, wa_ref[:, cols], preferred_element_type=F32)
        ug = jnp.dot(hh, wg_ref[:, cols], preferred_element_type=F32)
        a = _conv_taps(ua, ta[:, cols], cwa_ref, cba_ref, cols)
        g = _conv_taps(ug, tg[:, cols], cwg_ref, cbg_ref, cols)
        act_ref[:, cols] = (a * _gelu_tanh(g)).astype(act_ref.dtype)
        ta[:, cols] = ua[tm - 8:, :]
        tg[:, cols] = ug[tm - 8:, :]
        lasta_ref[:, cols] = ua[tm - (CONV_W - 1):, :]
        lastg_ref[:, cols] = ug[tm - (CONV_W - 1):, :]


def _ffn_up_prompt_call(h, w_up, layer, conv_w, conv_b, conv_buf, bsz, seq):
    m, d = h.shape
    dff = w_up.shape[2] // 2
    tm = _pick(seq, 1024)
    tn = _pick(dff, 512)
    cc = _pick(tn, 256)
    nj = dff // tn
    tiles_per_seq = seq // tm
    kern = functools.partial(_ffn_up_prompt_kernel, tm=tm, tn=tn, cc=cc, tiles_per_seq=tiles_per_seq)
    cb = conv_b.reshape(1, 2 * dff).astype(F32)
    seq_map_a = lambda j, i: (i // tiles_per_seq, 0, j)
    seq_map_g = lambda j, i: (i // tiles_per_seq, 0, j + nj)
    act, last_a, last_g = pl.pallas_call(
        kern,
        grid=(nj, m // tm),
        in_specs=[pl.BlockSpec((tm, d), lambda j, i: (i, 0)),
                  pl.BlockSpec((None, d, tn), lambda j, i: (layer, 0, j)),
                  pl.BlockSpec((None, d, tn), lambda j, i: (layer, 0, j + nj)),
                  pl.BlockSpec((CONV_W, tn), lambda j, i: (0, j)),
                  pl.BlockSpec((CONV_W, tn), lambda j, i: (0, j + nj)),
                  pl.BlockSpec((1, tn), lambda j, i: (0, j)),
                  pl.BlockSpec((1, tn), lambda j, i: (0, j + nj)),
                  pl.BlockSpec((None, CONV_W - 1, tn), seq_map_a),
                  pl.BlockSpec((None, CONV_W - 1, tn), seq_map_g)],
        out_specs=[pl.BlockSpec((tm, tn), lambda j, i: (i, j)),
                   pl.BlockSpec((None, CONV_W - 1, tn), lambda j, i: (i // tiles_per_seq, 0, j)),
                   pl.BlockSpec((None, CONV_W - 1, tn), lambda j, i: (i // tiles_per_seq, 0, j))],
        out_shape=[jax.ShapeDtypeStruct((m, dff), BF16),
                   jax.ShapeDtypeStruct((bsz, CONV_W - 1, dff), F32),
                   jax.ShapeDtypeStruct((bsz, CONV_W - 1, dff), F32)],
        scratch_shapes=[pltpu.VMEM((8, tn), F32), pltpu.VMEM((8, tn), F32)],
        compiler_params=_params(("arbitrary", "arbitrary")),
        name="ffn_up_prompt",
    )(h, w_up, w_up, conv_w, conv_w, cb, cb, conv_buf, conv_buf)
    return act, jnp.concatenate([last_a, last_g], axis=-1)


def _ffn_up_sample_kernel(h_ref, wa_ref, wg_ref, cwa_ref, cwg_ref, cba_ref, cbg_ref,
                          b0a_ref, b0g_ref, b1a_ref, b1g_ref, act_ref, ua_ref, ug_ref, sa, sg, *, m, lq):
    pad = 8
    hh = h_ref[...]
    t = lax.broadcasted_iota(jnp.int32, (m, 1), 0) % lq

    def branch(w_ref, cw_ref, cb_ref, b0_ref, b1_ref, u_ref, s):
        u = jnp.dot(hh, w_ref[...], preferred_element_type=F32)
        u_ref[...] = u
        s[0:pad, :] = jnp.zeros((pad, s.shape[1]), F32)
        s[pad:pad + m, :] = u
        um1 = jnp.where(t == 0, b1_ref[...], s[pad - 1:pad - 1 + m, :])
        um2 = jnp.where(t == 0, b0_ref[...], jnp.where(t == 1, b1_ref[...], s[pad - 2:pad - 2 + m, :]))
        return um2 * cw_ref[0:1, :] + um1 * cw_ref[1:2, :] + u * cw_ref[2:3, :] + cb_ref[...]

    a = branch(wa_ref, cwa_ref, cba_ref, b0a_ref, b1a_ref, ua_ref, sa)
    g = branch(wg_ref, cwg_ref, cbg_ref, b0g_ref, b1g_ref, ug_ref, sg)
    act_ref[...] = (a * _gelu_tanh(g)).astype(act_ref.dtype)


def _ffn_up_sample_call(h, w_up, layer, conv_w, conv_b, conv_buf, bsz, lq):
    m, d = h.shape
    dff = w_up.shape[2] // 2
    tn = _pick(dff, 512)
    nj = dff // tn
    assert lq >= CONV_W - 1
    kern = functools.partial(_ffn_up_sample_kernel, m=m, lq=lq)
    cb = conv_b.reshape(1, 2 * dff).astype(F32)
    b0 = jnp.repeat(conv_buf[:, 0], lq, axis=0)
    b1 = jnp.repeat(conv_buf[:, 1], lq, axis=0)
    col_a = lambda j: (0, j)
    col_g = lambda j: (0, j + nj)
    act, u_a, u_g = pl.pallas_call(
        kern,
        grid=(nj,),
        in_specs=[pl.BlockSpec((m, d), lambda j: (0, 0)),
                  pl.BlockSpec((None, d, tn), lambda j: (layer, 0, j)),
                  pl.BlockSpec((None, d, tn), lambda j: (layer, 0, j + nj)),
                  pl.BlockSpec((CONV_W, tn), col_a), pl.BlockSpec((CONV_W, tn), col_g),
                  pl.BlockSpec((1, tn), col_a), pl.BlockSpec((1, tn), col_g),
                  pl.BlockSpec((m, tn), col_a), pl.BlockSpec((m, tn), col_g),
                  pl.BlockSpec((m, tn), col_a), pl.BlockSpec((m, tn), col_g)],
        out_specs=[pl.BlockSpec((m, tn), col_a), pl.BlockSpec((m, tn), col_a), pl.BlockSpec((m, tn), col_a)],
        out_shape=[jax.ShapeDtypeStruct((m, dff), BF16),
                   jax.ShapeDtypeStruct((m, dff), F32),
                   jax.ShapeDtypeStruct((m, dff), F32)],
        scratch_shapes=[pltpu.VMEM((8 + m, tn), F32), pltpu.VMEM((8 + m, tn), F32)],
        compiler_params=_params(("arbitrary",)),
        name="ffn_up_sample",
    )(h, w_up, w_up, conv_w, conv_w, cb, cb, b0, b0, b1, b1)
    u = jnp.concatenate([u_a, u_g], axis=-1).reshape(bsz, lq, 2 * dff)
    return act, u[:, lq - (CONV_W - 1):]


def _ffn_down_kernel(act_ref, w_ref, x_ref, gpost_ref, gnext_ref, x2_ref, h_ref, acc):
    k = pl.program_id(1)

    @pl.when(k == 0)
    def _():
        acc[...] = jnp.zeros(acc.shape, F32)

    acc[...] += jnp.dot(act_ref[...], w_ref[...], preferred_element_type=F32)

    @pl.when(k == pl.num_programs(1) - 1)
    def _():
        x2 = x_ref[...] + _rms(acc[...], gpost_ref[...])
        x2_ref[...] = x2
        h_ref[...] = _rms(x2, gnext_ref[...]).astype(h_ref.dtype)


def _ffn_down_call(act, w, layer, x, g_post, g_next):
    m, d = x.shape
    dff = act.shape[1]
    tm = _pick(m, 512)
    tk = dff // 4 if (dff % 4 == 0 and (dff // 4) % LANES == 0) else dff
    row = lambda i, k: (i, 0)
    const = lambda i, k: (0, 0)
    return pl.pallas_call(
        _ffn_down_kernel,
        grid=(m // tm, dff // tk),
        in_specs=[pl.BlockSpec((tm, tk), lambda i, k: (i, k)),
                  pl.BlockSpec((None, tk, d), lambda i, k: (layer, k, 0)),
                  pl.BlockSpec((tm, d), row),
                  pl.BlockSpec((1, d), const),
                  pl.BlockSpec((1, d), const)],
        out_specs=[pl.BlockSpec((tm, d), row), pl.BlockSpec((tm, d), row)],
        out_shape=[jax.ShapeDtypeStruct((m, d), F32), jax.ShapeDtypeStruct((m, d), BF16)],
        scratch_shapes=[pltpu.VMEM((tm, d), F32)],
        compiler_params=_params(("arbitrary", "arbitrary")),
        name="ffn_down",
    )(act, w, x, g_post.reshape(1, d).astype(F32), g_next.reshape(1, d).astype(F32))


def kernel(x_prompt, x_sample, cache_k, cache_v, state_ret, state_pool, state_conv, page_table,
           rel_bias_table, norm_mix_pre, w_in, lambda_q1, lambda_k1, lambda_q2, lambda_k2, subln_a,
           pool_w, pool_scale, w_out, norm_mix_post, norm_ffn_pre, ffn_w_up, ffn_conv_w, ffn_conv_b,
           ffn_w_down, norm_ffn_post):
    bp, seq, d = x_prompt.shape
    bs, lq, _ = x_sample.shape
    depth = w_in.shape[0]
    dff2 = ffn_w_up.shape[2]
    past_len = page_table.shape[1] * cache_k.shape[2]

    xp = x_prompt.reshape(bp * seq, d)
    xs = x_sample.reshape(bs * lq, d)
    hp = _norm_call(xp, norm_mix_pre[0])
    hs = _norm_call(xs, norm_mix_pre[0])
    zeros_hist_p = jnp.zeros((bp, POOL_HIST, W_C), F32)
    zeros_ret_p = jnp.zeros((bp, H_B, DK_B, DV_B), F32)
    zeros_conv_p = jnp.zeros((bp, CONV_W - 1, dff2), F32)

    w_in_b = w_in.astype(BF16)
    w_out_b = w_out.astype(BF16)
    w_up_b = ffn_w_up.astype(BF16)
    w_down_b = ffn_w_down.astype(BF16)

    outs = {k: [] for k in ('kp', 'vp', 'ks', 'vs', 'rp', 'rs', 'pp', 'ps', 'cp', 'cs')}
    for l in range(depth):
        lam_init = 0.8 - 0.6 * math.exp(-0.3 * l)
        lam = (jnp.exp(jnp.sum(lambda_q1[l].astype(F32) * lambda_k1[l].astype(F32)))
               - jnp.exp(jnp.sum(lambda_q2[l].astype(F32) * lambda_k2[l].astype(F32))) + lam_init)
        g_next = norm_mix_pre[l + 1] if l + 1 < depth else norm_mix_pre[l]

        proj_p = _matmul_call(hp, w_in_b, l)
        oa_p = _attn_prompt_call(proj_p, bp, seq, rel_bias_table, lam, subln_a[l], 1.0 - lam_init)
        bc_p, ret_p = _mixer_call(proj_p, bp, seq, 0, zeros_ret_p, zeros_hist_p, pool_w[l], pool_scale[l], BF16)
        x1_p, h2_p = _outproj_call(oa_p, bc_p, w_out_b, l, xp, norm_mix_post[l], norm_ffn_pre[l])
        act_p, conv_p = _ffn_up_prompt_call(h2_p, w_up_b, l, ffn_conv_w[l], ffn_conv_b[l], zeros_conv_p, bp, seq)
        xp, hp = _ffn_down_call(act_p, w_down_b, l, x1_p, norm_ffn_post[l], g_next)
        proj_p3 = proj_p.reshape(bp, seq, -1)
        outs['kp'].append(proj_p3[:, :, COL_KA:COL_KA + W_A].reshape(bp, seq, H_A, DV_A))
        outs['vp'].append(proj_p3[:, :, COL_VA:COL_VA + W_A].reshape(bp, seq, H_A, DV_A))
        outs['rp'].append(ret_p)
        outs['pp'].append(proj_p3[:, seq - POOL_BUF:, COL_XC:COL_XC + W_C])
        outs['cp'].append(conv_p)

        proj_s = _matmul_call(hs, w_in_b, l)
        oa_s = _attn_sample_call(proj_s, cache_k, cache_v, l, page_table, rel_bias_table, lam, subln_a[l],
                                 1.0 - lam_init, lq)
        hist_s = jnp.concatenate([jnp.zeros((bs, POOL_HIST - POOL_BUF, W_C), F32),
                                  state_pool[l].astype(F32)], axis=1)
        bc_s, ret_s = _mixer_call(proj_s, bs, lq, past_len, state_ret[l], hist_s, pool_w[l], pool_scale[l], F32)
        x1_s, h2_s = _outproj_call(oa_s, bc_s, w_out_b, l, xs, norm_mix_post[l], norm_ffn_pre[l])
        act_s, conv_s = _ffn_up_sample_call(h2_s, w_up_b, l, ffn_conv_w[l], ffn_conv_b[l],
                                            state_conv[l].astype(F32), bs, lq)
        xs, hs = _ffn_down_call(act_s, w_down_b, l, x1_s, norm_ffn_post[l], g_next)
        proj_s3 = proj_s.reshape(bs, lq, -1)
        outs['ks'].append(proj_s3[:, :, COL_KA:COL_KA + W_A].reshape(bs, lq, H_A, DV_A))
        outs['vs'].append(proj_s3[:, :, COL_VA:COL_VA + W_A].reshape(bs, lq, H_A, DV_A))
        outs['rs'].append(ret_s)
        xe_s = jnp.concatenate([state_pool[l].astype(F32), proj_s3[:, :, COL_XC:COL_XC + W_C]], axis=1)
        outs['ps'].append(xe_s[:, -POOL_BUF:])
        outs['cs'].append(conv_s)

    st = lambda k: jnp.stack(outs[k])
    return (xp.reshape(bp, seq, d), xs.reshape(bs, lq, d),
            st('kp'), st('vp'), st('ks'), st('vs'), st('rp'), st('rs'),
            st('pp'), st('ps'), st('cp'), st('cs'))
```

```python
import functools
import math

import numpy as np
import jax
import jax.numpy as jnp
from jax import lax
from jax.experimental import pallas as pl
from jax.experimental.pallas import tpu as pltpu

F32 = jnp.float32
BF16 = jnp.bfloat16

H_A = 6
DH_A = 64
DV_A = 128
W_A = H_A * DV_A
ATT_SCALE = DH_A ** -0.5
NEG_INF = -1e30
H_B = 6
DK_B = 64
DV_B = 128
W_B = H_B * DV_B
ROPE_BASE = 10000.0
N_POOL_GROUPS = 4
POOL_GC = 128
POOL_WINDOWS = (2, 4, 8, 16)
W_C = N_POOL_GROUPS * POOL_GC
POOL_BUF = 15
POOL_HIST = 16
REL_BUCKETS = 32
REL_MAX_DIST = 128
CONV_W = 3
EPS = 1e-6
COL_QA, COL_KA, COL_VA = 0, W_A, 2 * W_A
COL_QB = 3 * W_A
COL_KB = COL_QB + H_B * DK_B
COL_VB = COL_KB + H_B * DK_B
COL_GB = COL_VB + W_B
COL_XC = COL_GB + W_B

LANES = 128
VMEM_LIMIT = 56 * 1024 * 1024


def _params(sem, vmem=VMEM_LIMIT):
    return pltpu.CompilerParams(dimension_semantics=sem, vmem_limit_bytes=vmem)


def _rms(x, g=None):
    y = x * lax.rsqrt(jnp.mean(x * x, axis=-1, keepdims=True) + EPS)
    return y if g is None else y * g


def _pick(n, pref):
    if n <= pref:
        return n
    t = pref
    while n % t:
        t //= 2
    return t


def _norm_kernel(x_ref, g_ref, o_ref):
    o_ref[...] = _rms(x_ref[...], g_ref[...]).astype(o_ref.dtype)


def _norm_call(x, g):
    m, d = x.shape
    tm = _pick(m, 512)
    return pl.pallas_call(
        _norm_kernel,
        grid=(m // tm,),
        in_specs=[pl.BlockSpec((tm, d), lambda i: (i, 0)),
                  pl.BlockSpec((1, d), lambda i: (0, 0))],
        out_specs=pl.BlockSpec((tm, d), lambda i: (i, 0)),
        out_shape=jax.ShapeDtypeStruct((m, d), BF16),
        compiler_params=_params(("arbitrary",)),
        name="rmsnorm",
    )(x, g.reshape(1, d))


def _matmul_kernel(x_ref, w_ref, o_ref):
    o_ref[...] = jnp.dot(x_ref[...], w_ref[...], preferred_element_type=F32)


def _matmul_call(x, w, layer):
    m, k = x.shape
    n = w.shape[2]
    tm = _pick(m, 1024)
    tn = _pick(n, 1024)
    return pl.pallas_call(
        _matmul_kernel,
        grid=(n // tn, m // tm),
        in_specs=[pl.BlockSpec((tm, k), lambda j, i: (i, 0)),
                  pl.BlockSpec((None, k, tn), lambda j, i: (layer, 0, j))],
        out_specs=pl.BlockSpec((tm, tn), lambda j, i: (i, j)),
        out_shape=jax.ShapeDtypeStruct((m, n), F32),
        compiler_params=_params(("arbitrary", "arbitrary")),
        name="in_proj",
    )(x, w)


def _t5_bucket(dist):
    n = jnp.maximum(dist, 0)
    exact = REL_BUCKETS // 2
    nf = jnp.maximum(n, 1).astype(F32)
    large = exact + (jnp.log(nf / exact) / math.log(REL_MAX_DIST / exact)
                     * (REL_BUCKETS - exact)).astype(jnp.int32)
    return jnp.where(n < exact, n, jnp.minimum(large, REL_BUCKETS - 1))


def _far_bucket_is_constant(min_dist):
    exact = REL_BUCKETS // 2
    v = np.float32(min_dist) / np.float32(exact)
    large = exact + int(np.float32(np.log(v)) / np.float32(math.log(REL_MAX_DIST / exact))
                        * (REL_BUCKETS - exact))
    return min_dist >= exact and large >= REL_BUCKETS


def _attn_prompt_kernel(sc_ref, q_ref, k_ref, v_ref, d0_ref, d1_ref, g_ref, o_ref,
                        kb, vb, qs, s_a, s_b, acc, mrow, *, tile, out_scale):
    h = pl.program_id(1)
    qi = pl.program_id(2)

    @pl.when(qi == 0)
    def _():
        kb[...] = k_ref[...].astype(BF16)
        vb[:, :DV_A] = v_ref[...].astype(BF16)
        vb[:, DV_A:] = jnp.ones((vb.shape[0], DV_A), BF16)

    q = q_ref[...] * ATT_SCALE
    lane = lax.broadcasted_iota(jnp.int32, q.shape, 1)
    qs[0:tile, :] = jnp.where(lane < DH_A, q, 0.0).astype(BF16)
    qs[tile:, :] = jnp.where(lane >= DH_A, q, 0.0).astype(BF16)
    acc[...] = jnp.zeros(acc.shape, F32)
    mrow[...] = jnp.full(mrow.shape, NEG_INF, F32)
    cfar = sc_ref[1 + h]
    reps = tile // LANES

    def scores(kj, s_ref):
        start = pl.multiple_of(kj * tile, tile)
        s_ref[...] = lax.dot_general(qs[...], kb[pl.ds(start, tile), :], (((1,), (1,)), ((), ())),
                                     preferred_element_type=F32)

    def softmax_pv(kj, s_ref, bias_ref):
        start = pl.multiple_of(kj * tile, tile)
        vs = vb[pl.ds(start, tile), :]
        s = s_ref[...]
        if bias_ref is None:
            m_cur = jnp.max(s, axis=-1, keepdims=True) + cfar
        else:
            s = s + jnp.concatenate([bias_ref[...], bias_ref[...]], axis=0)
            m_cur = jnp.max(s, axis=-1, keepdims=True)
        m_prev = mrow[...]
        m_new = jnp.maximum(m_prev, m_cur)
        alpha = jnp.exp(m_prev - m_new)
        shift = m_new - cfar if bias_ref is None else m_new
        p = jnp.exp(s - jnp.tile(shift, (1, reps))).astype(BF16)
        acc[...] = jnp.tile(alpha, (1, 2)) * acc[...] + jnp.dot(p, vs, preferred_element_type=F32)
        mrow[...] = m_new

    n_far = qi - 1
    n_pairs = jnp.maximum(n_far, 0) // 2

    @pl.when(qi >= 1)
    def _():
        scores(0, s_a)

    def pair_body(jj, carry):
        j = 2 * jj
        scores(j + 1, s_b)
        softmax_pv(j, s_a, None)
        scores(j + 2, s_a)
        softmax_pv(j + 1, s_b, None)
        return carry

    lax.fori_loop(0, n_pairs, pair_body, 0)

    @pl.when(jnp.logical_and(qi >= 1, n_far % 2 == 1))
    def _():
        scores(qi - 1, s_b)
        softmax_pv(qi - 2, s_a, None)
        scores(qi, s_a)
        softmax_pv(qi - 1, s_b, d1_ref)
        softmax_pv(qi, s_a, d0_ref)

    @pl.when(jnp.logical_and(qi >= 1, n_far % 2 == 0))
    def _():
        scores(qi, s_b)
        softmax_pv(qi - 1, s_a, d1_ref)
        softmax_pv(qi, s_b, d0_ref)

    @pl.when(qi == 0)
    def _():
        scores(0, s_a)
        softmax_pv(0, s_a, d0_ref)

    lam = sc_ref[0]
    a1 = acc[0:tile, :]
    a2 = acc[tile:, :]
    o = a1[:, :DV_A] / a1[:, DV_A:] - lam * (a2[:, :DV_A] / a2[:, DV_A:])
    o_ref[...] = (_rms(o, g_ref[...]) * out_scale).astype(o_ref.dtype)


def _toeplitz(u, t):
    hh = u.shape[0]
    return jnp.tile(u, (1, t))[:, :t * (2 * t - 1)].reshape(hh, t, 2 * t - 1)[:, :, :t]


def _attn_prompt_call(proj, bsz, seq, table, lam, subln, out_scale):
    tile = _pick(seq, 512)
    nq = seq // tile
    assert tile % LANES == 0 and _far_bucket_is_constant(tile + 1)
    bv = table[_t5_bucket(jnp.arange(2 * tile))].astype(F32).T
    neg = jnp.full((H_A, tile), NEG_INF, F32)
    u0 = jnp.concatenate([bv[:, :1], neg, bv[:, 1:tile][:, ::-1]], axis=1)
    u1 = jnp.concatenate([bv[:, tile:tile + 1], bv[:, 1:tile][:, ::-1], neg[:, :1],
                          bv[:, tile + 1:][:, ::-1]], axis=1)
    d0 = _toeplitz(u0, tile)
    d1 = _toeplitz(u1, tile)
    scal = jnp.concatenate([jnp.reshape(lam, (1,)).astype(F32),
                            table[REL_BUCKETS - 1].astype(F32)])
    kq, kk, kv = COL_QA // DV_A, COL_KA // DV_A, COL_VA // DV_A
    kern = functools.partial(_attn_prompt_kernel, tile=tile, out_scale=out_scale)
    return pl.pallas_call(
        kern,
        grid=(bsz, H_A, nq),
        in_specs=[
            pl.BlockSpec(memory_space=pltpu.SMEM),
            pl.BlockSpec((tile, DV_A), lambda b, h, i: (b * nq + i, kq + h)),
            pl.BlockSpec((seq, DV_A), lambda b, h, i: (b, kk + h)),
            pl.BlockSpec((seq, DV_A), lambda b, h, i: (b, kv + h)),
            pl.BlockSpec((None, tile, tile), lambda b, h, i: (h, 0, 0)),
            pl.BlockSpec((None, tile, tile), lambda b, h, i: (h, 0, 0)),
            pl.BlockSpec((1, DV_A), lambda b, h, i: (0, 0)),
        ],
        out_specs=pl.BlockSpec((tile, DV_A), lambda b, h, i: (b * nq + i, h)),
        out_shape=jax.ShapeDtypeStruct((bsz * seq, W_A), BF16),
        scratch_shapes=[pltpu.VMEM((seq, DV_A), BF16),
                        pltpu.VMEM((seq, 2 * DV_A), BF16),
                        pltpu.VMEM((2 * tile, DV_A), BF16),
                        pltpu.VMEM((2 * tile, tile), F32),
                        pltpu.VMEM((2 * tile, tile), F32),
                        pltpu.VMEM((2 * tile, 2 * DV_A), F32),
                        pltpu.VMEM((2 * tile, LANES), F32)],
        compiler_params=_params(("arbitrary", "arbitrary", "arbitrary")),
        name="attn_prompt",
    )(scal, proj, proj, proj, d0, d1, subln.reshape(1, DV_A).astype(F32))


def _attn_sample_kernel(pt_ref, sc_ref, qbd_ref, kn_ref, vn_ref, bl_ref, bf_ref, bn_ref, g_ref,
                        ck_ref, cv_ref, o_ref, st, oacc, vnew, ring, sems,
                        *, layer, n_seq, n_pages, group, n_slots, page, lq, out_scale):
    b = pl.program_id(0)
    past = n_pages * page
    items_per_seq = 2 * n_pages
    n_items = n_seq * items_per_seq
    n_groups = n_pages // group

    def page_copy(cache_ref, page_idx, slot):
        return pltpu.make_async_copy(cache_ref.at[layer, page_idx], ring.at[slot], sems.at[slot])

    def issue(item, slot, queue):
        seq = item // items_per_seq
        it = item % items_per_seq

        @pl.when(jnp.logical_and(item < n_items, it < n_pages))
        def _():
            page_copy(ck_ref, pt_ref[seq * n_pages + it], slot).start(priority=queue)

        @pl.when(jnp.logical_and(item < n_items, it >= n_pages))
        def _():
            page_copy(cv_ref, pt_ref[seq * n_pages + it - n_pages], slot).start(priority=queue)

    @pl.when(b == 0)
    def _():
        for r in range(n_slots):
            issue(jnp.int32(r), r, r % 2)

    item0 = b * items_per_seq

    def group_pages(base):
        return jnp.concatenate(
            [jnp.concatenate([ring[base + r, h].astype(BF16) for h in range(H_A)], axis=1)
             for r in range(group)], axis=0)

    def k_group(grp, carry):
        base = (grp % (n_slots // group)) * group
        for r in range(group):
            page_copy(ck_ref, pt_ref[b * n_pages + grp * group + r], base + r).wait()
        row = pl.multiple_of(grp * group * page, group * page)
        st[pl.ds(row, group * page), :] = jnp.dot(group_pages(base), qbd_ref[...],
                                                   preferred_element_type=F32)
        for r in range(group):
            issue(item0 + grp * group + r + n_slots, base + r, r % 2)
        return carry

    lax.fori_loop(0, n_groups, k_group, 0)

    st[past:past + page, :] = jnp.full((page, LANES), NEG_INF, F32)
    st[past:past + lq, :] = jnp.dot(kn_ref[...].astype(BF16), qbd_ref[...],
                                    preferred_element_type=F32) + bn_ref[...]
    st[past - page:past, :] = st[past - page:past, :] + bl_ref[...]
    bfar = bf_ref[...]
    n_far = (past - page) // page

    def max_body(i, m):
        row = pl.multiple_of(i * page, page)
        return jnp.maximum(m, jnp.max(st[pl.ds(row, page), :], axis=0, keepdims=True))

    m_far = lax.fori_loop(0, n_far, max_body, jnp.full((1, LANES), NEG_INF, F32))
    m_near = jnp.max(st[past - page:past + page, :], axis=0, keepdims=True)
    m = jnp.maximum(m_far + bfar, m_near)

    def exp_body(i, l):
        row = pl.multiple_of(i * page, page)
        p = jnp.exp(st[pl.ds(row, page), :] - (m - bfar))
        st[pl.ds(row, page), :] = p
        return l + jnp.sum(p, axis=0, keepdims=True)

    l_far = lax.fori_loop(0, n_far, exp_body, jnp.zeros((1, LANES), F32))
    p_near = jnp.exp(st[past - page:past + page, :] - m)
    st[past - page:past + page, :] = p_near
    l_lane = l_far + jnp.sum(p_near, axis=0, keepdims=True)
    oacc[...] = jnp.zeros(oacc.shape, F32)
    vnew[...] = jnp.zeros(vnew.shape, BF16)
    vnew[0:lq, :] = vn_ref[...].astype(BF16)

    def v_group(grp, carry):
        base = ((n_groups + grp) % (n_slots // group)) * group
        for r in range(group):
            page_copy(cv_ref, pt_ref[b * n_pages + grp * group + r], base + r).wait()
        row = pl.multiple_of(grp * group * page, group * page)
        pt_t = st[pl.ds(row, group * page), :].T.astype(BF16)
        oacc[...] += jnp.dot(pt_t, group_pages(base), preferred_element_type=F32)
        for r in range(group):
            issue(item0 + n_pages + grp * group + r + n_slots, base + r, r % 2)
        return carry

    lax.fori_loop(0, n_groups, v_group, 0)

    pn_t = st[past:past + page, :].T.astype(BF16)
    rr = lax.broadcasted_iota(jnp.int32, (LANES, LANES), 0)
    cc = lax.broadcasted_iota(jnp.int32, (LANES, LANES), 1)
    l_rows = jnp.sum(jnp.where(rr == cc, jnp.broadcast_to(l_lane, (LANES, LANES)), 0.0),
                     axis=1, keepdims=True)
    lam = sc_ref[0]
    half = LANES // 2
    tot_all = oacc[...] + jnp.dot(pn_t, vnew[...], preferred_element_type=F32)
    for h in range(H_A):
        r1 = h * lq
        r2 = half + h * lq
        tot = tot_all[:, h * DV_A:(h + 1) * DV_A]
        o1 = tot[r1:r1 + lq] / l_rows[r1:r1 + lq]
        o2 = tot[r2:r2 + lq] / l_rows[r2:r2 + lq]
        o = o1 - lam * o2
        o_ref[:, h * DV_A:(h + 1) * DV_A] = _rms(o, g_ref[...]) * out_scale


def _lane_bias(t, lq, half):
    rows = t.shape[1]
    x = jnp.transpose(t, (1, 0, 2)).reshape(rows, H_A * lq)
    x = jnp.concatenate([x, jnp.zeros((rows, half - H_A * lq), F32)], axis=1)
    return jnp.concatenate([x, x], axis=1)


def _attn_sample_call(proj, cache_k, cache_v, layer, page_table, table, lam, subln, out_scale, lq):
    bs = page_table.shape[0]
    n_pages = page_table.shape[1]
    page = cache_k.shape[2]
    assert page == LANES and lq == 8 and H_A * lq <= LANES // 2
    assert _far_bucket_is_constant(page + 1)
    group = _pick(n_pages, 8)
    n_slots = 4 * group if (2 * n_pages) % (4 * group) == 0 else 2 * group
    assert n_pages % group == 0 and (2 * n_pages) % n_slots == 0
    past = n_pages * page
    half = LANES // 2
    ck = jnp.transpose(cache_k, (0, 1, 3, 2, 4))
    cv = jnp.transpose(cache_v, (0, 1, 3, 2, 4))

    q = proj[:, COL_QA:COL_QA + W_A] * ATT_SCALE
    qt = jnp.transpose(q.reshape(bs, lq, W_A), (0, 2, 1))
    row = np.arange(W_A)[:, None]
    lane = np.arange(LANES)[None, :]
    keep = ((row // DV_A) == (lane % half) // lq) & (((row % DV_A) // DH_A) == lane // half)
    qbd = jnp.where(jnp.asarray(keep)[None], jnp.tile(qt, (1, 1, LANES // lq)), 0.0).astype(BF16)

    bv = table[_t5_bucket(jnp.arange(2 * page))].astype(F32).T
    u_last = jnp.concatenate([bv[:, page:], bv[:, :page]], axis=1)
    b_last = _lane_bias(_toeplitz(u_last, page)[:, :, :lq], lq, half)
    neg = jnp.full((H_A, page), NEG_INF, F32)
    u_new = jnp.concatenate([bv[:, :page], neg], axis=1)
    b_new = _lane_bias(_toeplitz(u_new, page)[:, :lq, :lq], lq, half)
    lane_h = np.minimum((np.arange(LANES) % half) // lq, H_A - 1)
    valid = jnp.asarray((np.arange(LANES) % half) < H_A * lq)
    b_far = jnp.where(valid, table[REL_BUCKETS - 1].astype(F32)[lane_h], 0.0).reshape(1, LANES)
    scal = jnp.reshape(lam, (1,)).astype(F32)
    pt_flat = page_table.reshape(-1).astype(jnp.int32)

    kcol, vcol = COL_KA // W_A, COL_VA // W_A
    in_specs = [
        pl.BlockSpec(memory_space=pltpu.SMEM),
        pl.BlockSpec((None, W_A, LANES), lambda b, pt: (b, 0, 0)),
        pl.BlockSpec((lq, W_A), lambda b, pt: (b, kcol)),
        pl.BlockSpec((lq, W_A), lambda b, pt: (b, vcol)),
        pl.BlockSpec((page, LANES), lambda b, pt: (0, 0)),
        pl.BlockSpec((1, LANES), lambda b, pt: (0, 0)),
        pl.BlockSpec((lq, LANES), lambda b, pt: (0, 0)),
        pl.BlockSpec((1, DV_A), lambda b, pt: (0, 0)),
        pl.BlockSpec(memory_space=pl.ANY),
        pl.BlockSpec(memory_space=pl.ANY),
    ]
    kern = functools.partial(_attn_sample_kernel, layer=layer, n_seq=bs, n_pages=n_pages, group=group,
                             n_slots=n_slots, page=page, lq=lq, out_scale=out_scale)
    grid_spec = pltpu.PrefetchScalarGridSpec(
        num_scalar_prefetch=1,
        grid=(bs,),
        in_specs=in_specs,
        out_specs=pl.BlockSpec((None, lq, W_A), lambda b, pt: (b, 0, 0)),
        scratch_shapes=[pltpu.VMEM((past + page, LANES), F32),
                        pltpu.VMEM((LANES, W_A), F32),
                        pltpu.VMEM((page, W_A), BF16),
                        pltpu.VMEM((n_slots, H_A, page, DV_A), F32),
                        pltpu.SemaphoreType.DMA((n_slots,))],
    )
    out = pl.pallas_call(
        kern,
        grid_spec=grid_spec,
        out_shape=jax.ShapeDtypeStruct((bs, lq, W_A), F32),
        compiler_params=_params(("arbitrary",)),
        name="attn_sample",
    )(pt_flat, scal, qbd, proj, proj, b_last, b_far, b_new, subln.reshape(1, DV_A).astype(F32), ck, cv)
    return out.reshape(bs * lq, W_A)


def _log_gamma():
    return np.log(1.0 - 2.0 ** (-5.0 - np.arange(H_B, dtype=np.float64)))


def _retention_constants(chunk_math, rows):
    lg = _log_gamma()
    i = np.arange(rows, dtype=np.float64)
    diff = i[:, None] - i[None, :]
    dmat = np.where(diff >= 0, np.exp(np.maximum(diff, 0.0)[None] * lg[:, None, None]), 0.0)
    xi = np.exp((i[:, None] + 1.0) * lg[None, :])
    zeta = np.exp((chunk_math - 1.0 - i)[:, None] * lg[None, :])
    zeta = np.where(i[:, None] < chunk_math, zeta, 0.0)
    cdec = np.exp(chunk_math * lg)
    xi_l = np.repeat(xi, DV_B, axis=1)
    zeta_l = np.repeat(zeta, DK_B, axis=1) * (DK_B ** -0.5)
    cdec_rows = np.repeat(cdec.reshape(H_B // 2, 2), DK_B, axis=1).reshape(H_B // 2, 2 * DK_B, 1)
    return (jnp.asarray(dmat, F32), jnp.asarray(xi_l, F32), jnp.asarray(zeta_l, F32),
            jnp.asarray(cdec_rows, F32))


def _rope_tables(pos):
    half = DK_B // 2
    inv = ROPE_BASE ** (-jnp.arange(half, dtype=F32) / half)
    ang = pos.astype(F32)[:, None] * inv[None, :]
    cos = jnp.cos(ang)
    sin = jnp.sin(ang)
    cos_l = jnp.tile(jnp.concatenate([cos, cos], axis=1), (1, H_B))
    sin_l = jnp.tile(jnp.concatenate([-sin, sin], axis=1), (1, H_B))
    return cos_l, sin_l


def _mixer_kernel(qb_ref, kb_ref, vb_ref, gb_ref, xc_ref, cos_ref, sin_ref, dmat_ref, xi_ref, zeta_ref,
                  cdec_ref, s0_ref, hist_ref, pw_ref, ps_ref, o_ref, sout_ref, state, ext,
                  *, rows_in, rows, pos0):
    ci = pl.program_id(1)
    n_chunks = pl.num_programs(1)

    @pl.when(ci == 0)
    def _():
        state[...] = s0_ref[...]
        ext[0:POOL_HIST, :] = hist_ref[...]

    def pad_rows(x):
        if rows == rows_in:
            return x
        return jnp.concatenate([x, jnp.zeros((rows - rows_in, x.shape[1]), x.dtype)], axis=0)

    lane = lax.broadcasted_iota(jnp.int32, (rows, LANES), 1)
    first_half = (lane % DK_B) < (DK_B // 2)
    low_head = lane < DK_B

    def rope(x, cos, sin):
        swapped = jnp.where(first_half, pltpu.roll(x, LANES - DK_B // 2, 1), pltpu.roll(x, DK_B // 2, 1))
        return x * cos + swapped * sin

    cos = pad_rows(cos_ref[...])
    sin = pad_rows(sin_ref[...])
    qb = pad_rows(qb_ref[...])
    kb = pad_rows(kb_ref[...])
    vb = pad_rows(vb_ref[...])
    gb = pad_rows(gb_ref[...])
    zeta = zeta_ref[...]
    xi = xi_ref[...]
    for pr in range(H_B // 2):
        sl = slice(pr * LANES, (pr + 1) * LANES)
        q2 = rope(qb[:, sl], cos[:, sl], sin[:, sl])
        k2 = rope(kb[:, sl], cos[:, sl], sin[:, sl])
        kz2 = k2 * zeta[:, sl]
        k2 = k2 * (DK_B ** -0.5)
        s_pair = state[pr]
        s_pair_b = s_pair.astype(BF16)
        upd = jnp.zeros((LANES, DV_B), F32)
        k2b = k2.astype(BF16)
        for hh in range(2):
            h = 2 * pr + hh
            mask = low_head if hh == 0 else jnp.logical_not(low_head)
            qm = jnp.where(mask, q2, 0.0).astype(BF16)
            vh = vb[:, h * DV_B:(h + 1) * DV_B].astype(BF16)
            a = lax.dot_general(qm, k2b, (((1,), (1,)), ((), ())), preferred_element_type=F32)
            a = (a * dmat_ref[h]).astype(BF16)
            o = jnp.dot(a, vh, preferred_element_type=F32)
            o = o + jnp.dot(qm, s_pair_b, preferred_element_type=F32) * xi[:, h * DV_B:(h + 1) * DV_B]
            kzm_t = jnp.where(mask, kz2, 0.0).T.astype(BF16)
            upd = upd + jnp.dot(kzm_t, vh, preferred_element_type=F32)
            g = gb[:, h * DV_B:(h + 1) * DV_B]
            y = _rms(o) * (g / (1.0 + jnp.exp(-g)))
            o_ref[:, h * DV_B:(h + 1) * DV_B] = y[0:rows_in].astype(o_ref.dtype)
        state[pr] = s_pair * cdec_ref[pr] + upd

    @pl.when(ci == n_chunks - 1)
    def _():
        for pr in range(H_B // 2):
            sp = state[pr]
            sout_ref[2 * pr] = sp[0:DK_B]
            sout_ref[2 * pr + 1] = sp[DK_B:]

    xc = xc_ref[...]
    ext[POOL_HIST:POOL_HIST + rows_in, :] = xc
    pos = pos0 + ci * rows_in + lax.broadcasted_iota(jnp.int32, (rows_in, 1), 0)
    for gi, win in enumerate(POOL_WINDOWS):
        sl = slice(gi * POOL_GC, (gi + 1) * POOL_GC)
        tot = xc[:, sl]
        for d in range(1, win):
            tot = tot + ext[POOL_HIST - d:POOL_HIST - d + rows_in, sl]
        cnt = jnp.minimum(pos + 1, win).astype(F32)
        pooled = (tot / cnt - xc[:, sl]).astype(BF16)
        y = jnp.dot(pooled, pw_ref[gi], preferred_element_type=F32) * ps_ref[:, sl]
        o_ref[:, W_B + gi * POOL_GC:W_B + (gi + 1) * POOL_GC] = y.astype(o_ref.dtype)
    ext[0:POOL_HIST, :] = ext[rows_in:rows_in + POOL_HIST, :]


def _mixer_call(proj, bsz, seq, pos0, s0, hist, pool_w, pool_scale, out_dtype):
    rows_in = _pick(seq, 256)
    rows = max(rows_in, LANES)
    n_chunks = seq // rows_in
    assert rows_in % 8 == 0 and (rows_in == rows or n_chunks == 1)
    dmat, xi_l, zeta_l, cdec_rows = _retention_constants(rows_in, rows)
    cos_l, sin_l = _rope_tables(pos0 + jnp.arange(seq))
    s0p = s0.astype(F32).reshape(bsz, H_B // 2, 2 * DK_B, DV_B)
    wq, wv, wc = H_B * DK_B, W_B, W_C
    kern = functools.partial(_mixer_kernel, rows_in=rows_in, rows=rows, pos0=pos0)
    row_map = lambda col: (lambda b, c: (b * n_chunks + c, col))
    const2 = lambda b, c: (0, 0)
    const3 = lambda b, c: (0, 0, 0)
    out, sout = pl.pallas_call(
        kern,
        grid=(bsz, n_chunks),
        in_specs=[
            pl.BlockSpec((rows_in, wq), row_map(COL_QB // wq)),
            pl.BlockSpec((rows_in, wq), row_map(COL_KB // wq)),
            pl.BlockSpec((rows_in, wv), row_map(COL_VB // wv)),
            pl.BlockSpec((rows_in, wv), row_map(COL_GB // wv)),
            pl.BlockSpec((rows_in, wc), row_map(COL_XC // wc)),
            pl.BlockSpec((rows_in, wq), lambda b, c: (c, 0)),
            pl.BlockSpec((rows_in, wq), lambda b, c: (c, 0)),
            pl.BlockSpec((H_B, rows, rows), const3),
            pl.BlockSpec((rows, W_B), const2),
            pl.BlockSpec((rows, wq), const2),
            pl.BlockSpec((H_B // 2, 2 * DK_B, 1), const3),
            pl.BlockSpec((None, H_B // 2, 2 * DK_B, DV_B), lambda b, c: (b, 0, 0, 0)),
            pl.BlockSpec((None, POOL_HIST, W_C), lambda b, c: (b, 0, 0)),
            pl.BlockSpec((N_POOL_GROUPS, POOL_GC, POOL_GC), const3),
            pl.BlockSpec((1, W_C), const2),
        ],
        out_specs=[pl.BlockSpec((rows_in, W_B + W_C), lambda b, c: (b * n_chunks + c, 0)),
                   pl.BlockSpec((None, H_B, DK_B, DV_B), lambda b, c: (b, 0, 0, 0))],
        out_shape=[jax.ShapeDtypeStruct((bsz * seq, W_B + W_C), out_dtype),
                   jax.ShapeDtypeStruct((bsz, H_B, DK_B, DV_B), F32)],
        scratch_shapes=[pltpu.VMEM((H_B // 2, 2 * DK_B, DV_B), F32),
                        pltpu.VMEM((POOL_HIST + rows_in, W_C), F32)],
        compiler_params=_params(("arbitrary", "arbitrary")),
        name="mixer_ret_pool",
    )(proj, proj, proj, proj, proj, cos_l, sin_l, dmat, xi_l, zeta_l, cdec_rows, s0p, hist,
      pool_w.astype(BF16), pool_scale.reshape(1, W_C).astype(F32))
    return out, sout


def _outproj_kernel(oa_ref, bc_ref, w_ref, x_ref, gpost_ref, gnext_ref, x1_ref, h_ref):
    wa = oa_ref.shape[1]
    mix = jnp.dot(oa_ref[...].astype(BF16), w_ref[0:wa, :], preferred_element_type=F32)
    mix = mix + jnp.dot(bc_ref[...].astype(BF16), w_ref[wa:, :], preferred_element_type=F32)
    x1 = x_ref[...] + _rms(mix, gpost_ref[...])
    x1_ref[...] = x1
    h_ref[...] = _rms(x1, gnext_ref[...]).astype(h_ref.dtype)


def _outproj_call(oa, bc, w, layer, x, g_post, g_next):
    m, d = x.shape
    tm = _pick(m, 512)
    row = lambda i: (i, 0)
    const = lambda i: (0, 0)
    return pl.pallas_call(
        _outproj_kernel,
        grid=(m // tm,),
        in_specs=[pl.BlockSpec((tm, oa.shape[1]), row),
                  pl.BlockSpec((tm, bc.shape[1]), row),
                  pl.BlockSpec((None,) + w.shape[1:], lambda i: (layer, 0, 0)),
                  pl.BlockSpec((tm, d), row),
                  pl.BlockSpec((1, d), const),
                  pl.BlockSpec((1, d), const)],
        out_specs=[pl.BlockSpec((tm, d), row), pl.BlockSpec((tm, d), row)],
        out_shape=[jax.ShapeDtypeStruct((m, d), F32), jax.ShapeDtypeStruct((m, d), BF16)],
        compiler_params=_params(("arbitrary",)),
        name="out_proj",
    )(oa, bc, w, x, g_post.reshape(1, d).astype(F32), g_next.reshape(1, d).astype(F32))


def _gelu_tanh(g):
    return 0.5 * g * (1.0 + jnp.tanh(math.sqrt(2.0 / math.pi) * (g + 0.044715 * (g * g * g))))


def _conv_taps(u, tail, cw_ref, cb_ref, cols):
    row8 = lax.broadcasted_iota(jnp.int32, tail.shape, 0)
    r1 = pltpu.roll(u, 1, 0)
    r2 = pltpu.roll(u, 2, 0)
    head1 = jnp.where(row8 < 1, pltpu.roll(tail, 1, 0), r1[0:8])
    head2 = jnp.where(row8 < 2, pltpu.roll(tail, 2, 0), r2[0:8])
    um1 = jnp.concatenate([head1, r1[8:]], axis=0)
    um2 = jnp.concatenate([head2, r2[8:]], axis=0)
    return (um2 * cw_ref[0:1, cols] + um1 * cw_ref[1:2, cols] + u * cw_ref[2:3, cols]) + cb_ref[:, cols]


def _ffn_up_prompt_kernel(h_ref, wa_ref, wg_ref, cwa_ref, cwg_ref, cba_ref, cbg_ref, bufa_ref, bufg_ref,
                          act_ref, lasta_ref, lastg_ref, ta, tg, *, tm, tn, cc, tiles_per_seq):
    i = pl.program_id(1)

    @pl.when(i % tiles_per_seq == 0)
    def _():
        ta[8 - (CONV_W - 1):, :] = bufa_ref[...]
        tg[8 - (CONV_W - 1):, :] = bufg_ref[...]

    hh = h_ref[...]
    for c0 in range(0, tn, cc):
        cols = slice(c0, c0 + cc)
        ua = jnp.dot(hh, wa_ref[:, cols], preferred_element_type=F32)
        ug = jnp.dot(hh, wg_ref[:, cols], preferred_element_type=F32)
        a = _conv_taps(ua, ta[:, cols], cwa_ref, cba_ref, cols)
        g = _conv_taps(ug, tg[:, cols], cwg_ref, cbg_ref, cols)
        act_ref[:, cols] = (a * _gelu_tanh(g)).astype(act_ref.dtype)
        ta[:, cols] = ua[tm - 8:, :]
        tg[:, cols] = ug[tm - 8:, :]
        lasta_ref[:, cols] = ua[tm - (CONV_W - 1):, :]
        lastg_ref[:, cols] = ug[tm - (CONV_W - 1):, :]


def _ffn_up_prompt_call(h, w_up, layer, conv_w, conv_b, conv_buf, bsz, seq):
    m, d = h.shape
    dff = w_up.shape[2] // 2
    tm = _pick(seq, 1024)
    tn = _pick(dff, 512)
    cc = _pick(tn, 256)
    nj = dff // tn
    tiles_per_seq = seq // tm
    kern = functools.partial(_ffn_up_prompt_kernel, tm=tm, tn=tn, cc=cc, tiles_per_seq=tiles_per_seq)
    cb = conv_b.reshape(1, 2 * dff).astype(F32)
    seq_map_a = lambda j, i: (i // tiles_per_seq, 0, j)
    seq_map_g = lambda j, i: (i // tiles_per_seq, 0, j + nj)
    act, last_a, last_g = pl.pallas_call(
        kern,
        grid=(nj, m // tm),
        in_specs=[pl.BlockSpec((tm, d), lambda j, i: (i, 0)),
                  pl.BlockSpec((None, d, tn), lambda j, i: (layer, 0, j)),
                  pl.BlockSpec((None, d, tn), lambda j, i: (layer, 0, j + nj)),
                  pl.BlockSpec((CONV_W, tn), lambda j, i: (0, j)),
                  pl.BlockSpec((CONV_W, tn), lambda j, i: (0, j + nj)),
                  pl.BlockSpec((1, tn), lambda j, i: (0, j)),
                  pl.BlockSpec((1, tn), lambda j, i: (0, j + nj)),
                  pl.BlockSpec((None, CONV_W - 1, tn), seq_map_a),
                  pl.BlockSpec((None, CONV_W - 1, tn), seq_map_g)],
        out_specs=[pl.BlockSpec((tm, tn), lambda j, i: (i, j)),
                   pl.BlockSpec((None, CONV_W - 1, tn), lambda j, i: (i // tiles_per_seq, 0, j)),
                   pl.BlockSpec((None, CONV_W - 1, tn), lambda j, i: (i // tiles_per_seq, 0, j))],
        out_shape=[jax.ShapeDtypeStruct((m, dff), BF16),
                   jax.ShapeDtypeStruct((bsz, CONV_W - 1, dff), F32),
                   jax.ShapeDtypeStruct((bsz, CONV_W - 1, dff), F32)],
        scratch_shapes=[pltpu.VMEM((8, tn), F32), pltpu.VMEM((8, tn), F32)],
        compiler_params=_params(("arbitrary", "arbitrary")),
        name="ffn_up_prompt",
    )(h, w_up, w_up, conv_w, conv_w, cb, cb, conv_buf, conv_buf)
    return act, jnp.concatenate([last_a, last_g], axis=-1)


def _ffn_up_sample_kernel(h_ref, wa_ref, wg_ref, cwa_ref, cwg_ref, cba_ref, cbg_ref,
                          b0a_ref, b0g_ref, b1a_ref, b1g_ref, act_ref, ua_ref, ug_ref, sa, sg, *, m, lq):
    pad = 8
    hh = h_ref[...]
    t = lax.broadcasted_iota(jnp.int32, (m, 1), 0) % lq

    def branch(w_ref, cw_ref, cb_ref, b0_ref, b1_ref, u_ref, s):
        u = jnp.dot(hh, w_ref[...], preferred_element_type=F32)
        u_ref[...] = u
        s[0:pad, :] = jnp.zeros((pad, s.shape[1]), F32)
        s[pad:pad + m, :] = u
        um1 = jnp.where(t == 0, b1_ref[...], s[pad - 1:pad - 1 + m, :])
        um2 = jnp.where(t == 0, b0_ref[...], jnp.where(t == 1, b1_ref[...], s[pad - 2:pad - 2 + m, :]))
        return um2 * cw_ref[0:1, :] + um1 * cw_ref[1:2, :] + u * cw_ref[2:3, :] + cb_ref[...]

    a = branch(wa_ref, cwa_ref, cba_ref, b0a_ref, b1a_ref, ua_ref, sa)
    g = branch(wg_ref, cwg_ref, cbg_ref, b0g_ref, b1g_ref, ug_ref, sg)
    act_ref[...] = (a * _gelu_tanh(g)).astype(act_ref.dtype)


def _ffn_up_sample_call(h, w_up, layer, conv_w, conv_b, conv_buf, bsz, lq):
    m, d = h.shape
    dff = w_up.shape[2] // 2
    tn = _pick(dff, 512)
    nj = dff // tn
    assert lq >= CONV_W - 1
    kern = functools.partial(_ffn_up_sample_kernel, m=m, lq=lq)
    cb = conv_b.reshape(1, 2 * dff).astype(F32)
    b0 = jnp.repeat(conv_buf[:, 0], lq, axis=0)
    b1 = jnp.repeat(conv_buf[:, 1], lq, axis=0)
    col_a = lambda j: (0, j)
    col_g = lambda j: (0, j + nj)
    act, u_a, u_g = pl.pallas_call(
        kern,
        grid=(nj,),
        in_specs=[pl.BlockSpec((m, d), lambda j: (0, 0)),
                  pl.BlockSpec((None, d, tn), lambda j: (layer, 0, j)),
                  pl.BlockSpec((None, d, tn), lambda j: (layer, 0, j + nj)),
                  pl.BlockSpec((CONV_W, tn), col_a), pl.BlockSpec((CONV_W, tn), col_g),
                  pl.BlockSpec((1, tn), col_a), pl.BlockSpec((1, tn), col_g),
                  pl.BlockSpec((m, tn), col_a), pl.BlockSpec((m, tn), col_g),
                  pl.BlockSpec((m, tn), col_a), pl.BlockSpec((m, tn), col_g)],
        out_specs=[pl.BlockSpec((m, tn), col_a), pl.BlockSpec((m, tn), col_a), pl.BlockSpec((m, tn), col_a)],
        out_shape=[jax.ShapeDtypeStruct((m, dff), BF16),
                   jax.ShapeDtypeStruct((m, dff), F32),
                   jax.ShapeDtypeStruct((m, dff), F32)],
        scratch_shapes=[pltpu.VMEM((8 + m, tn), F32), pltpu.VMEM((8 + m, tn), F32)],
        compiler_params=_params(("arbitrary",)),
        name="ffn_up_sample",
    )(h, w_up, w_up, conv_w, conv_w, cb, cb, b0, b0, b1, b1)
    u = jnp.concatenate([u_a, u_g], axis=-1).reshape(bsz, lq, 2 * dff)
    return act, u[:, lq - (CONV_W - 1):]


def _ffn_down_kernel(act_ref, w_ref, x_ref, gpost_ref, gnext_ref, x2_ref, h_ref, acc):
    k = pl.program_id(1)

    @pl.when(k == 0)
    def _():
        acc[...] = jnp.zeros(acc.shape, F32)

    acc[...] += jnp.dot(act_ref[...], w_ref[...], preferred_element_type=F32)

    @pl.when(k == pl.num_programs(1) - 1)
    def _():
        x2 = x_ref[...] + _rms(acc[...], gpost_ref[...])
        x2_ref[...] = x2
        h_ref[...] = _rms(x2, gnext_ref[...]).astype(h_ref.dtype)


def _ffn_down_call(act, w, layer, x, g_post, g_next):
    m, d = x.shape
    dff = act.shape[1]
    tm = _pick(m, 512)
    tk = dff // 4 if (dff % 4 == 0 and (dff // 4) % LANES == 0) else dff
    row = lambda i, k: (i, 0)
    const = lambda i, k: (0, 0)
    return pl.pallas_call(
        _ffn_down_kernel,
        grid=(m // tm, dff // tk),
        in_specs=[pl.BlockSpec((tm, tk), lambda i, k: (i, k)),
                  pl.BlockSpec((None, tk, d), lambda i, k: (layer, k, 0)),
                  pl.BlockSpec((tm, d), row),
                  pl.BlockSpec((1, d), const),
                  pl.BlockSpec((1, d), const)],
        out_specs=[pl.BlockSpec((tm, d), row), pl.BlockSpec((tm, d), row)],
        out_shape=[jax.ShapeDtypeStruct((m, d), F32), jax.ShapeDtypeStruct((m, d), BF16)],
        scratch_shapes=[pltpu.VMEM((tm, d), F32)],
        compiler_params=_params(("arbitrary", "arbitrary")),
        name="ffn_down",
    )(act, w, x, g_post.reshape(1, d).astype(F32), g_next.reshape(1, d).astype(F32))


def kernel(x_prompt, x_sample, cache_k, cache_v, state_ret, state_pool, state_conv, page_table,
           rel_bias_table, norm_mix_pre, w_in, lambda_q1, lambda_k1, lambda_q2, lambda_k2, subln_a,
           pool_w, pool_scale, w_out, norm_mix_post, norm_ffn_pre, ffn_w_up, ffn_conv_w, ffn_conv_b,
           ffn_w_down, norm_ffn_post):
    bp, seq, d = x_prompt.shape
    bs, lq, _ = x_sample.shape
    depth = w_in.shape[0]
    dff2 = ffn_w_up.shape[2]
    past_len = page_table.shape[1] * cache_k.shape[2]

    xp = x_prompt.reshape(bp * seq, d)
    xs = x_sample.reshape(bs * lq, d)
    hp = _norm_call(xp, norm_mix_pre[0])
    hs = _norm_call(xs, norm_mix_pre[0])
    zeros_hist_p = jnp.zeros((bp, POOL_HIST, W_C), F32)
    zeros_ret_p = jnp.zeros((bp, H_B, DK_B, DV_B), F32)
    zeros_conv_p = jnp.zeros((bp, CONV_W - 1, dff2), F32)

    w_in_b = w_in.astype(BF16)
    w_out_b = w_out.astype(BF16)
    w_up_b = ffn_w_up.astype(BF16)
    w_down_b = ffn_w_down.astype(BF16)

    outs = {k: [] for k in ('kp', 'vp', 'ks', 'vs', 'rp', 'rs', 'pp', 'ps', 'cp', 'cs')}
    for l in range(depth):
        lam_init = 0.8 - 0.6 * math.exp(-0.3 * l)
        lam = (jnp.exp(jnp.sum(lambda_q1[l].astype(F32) * lambda_k1[l].astype(F32)))
               - jnp.exp(jnp.sum(lambda_q2[l].astype(F32) * lambda_k2[l].astype(F32))) + lam_init)
        g_next = norm_mix_pre[l + 1] if l + 1 < depth else norm_mix_pre[l]

        proj_p = _matmul_call(hp, w_in_b, l)
        oa_p = _attn_prompt_call(proj_p, bp, seq, rel_bias_table, lam, subln_a[l], 1.0 - lam_init)
        bc_p, ret_p = _mixer_call(proj_p, bp, seq, 0, zeros_ret_p, zeros_hist_p, pool_w[l], pool_scale[l], BF16)
        x1_p, h2_p = _outproj_call(oa_p, bc_p, w_out_b, l, xp, norm_mix_post[l], norm_ffn_pre[l])
        act_p, conv_p = _ffn_up_prompt_call(h2_p, w_up_b, l, ffn_conv_w[l], ffn_conv_b[l], zeros_conv_p, bp, seq)
        xp, hp = _ffn_down_call(act_p, w_down_b, l, x1_p, norm_ffn_post[l], g_next)
        proj_p3 = proj_p.reshape(bp, seq, -1)
        outs['kp'].append(proj_p3[:, :, COL_KA:COL_KA + W_A].reshape(bp, seq, H_A, DV_A))
        outs['vp'].append(proj_p3[:, :, COL_VA:COL_VA + W_A].reshape(bp, seq, H_A, DV_A))
        outs['rp'].append(ret_p)
        outs['pp'].append(proj_p3[:, seq - POOL_BUF:, COL_XC:COL_XC + W_C])
        outs['cp'].append(conv_p)

        proj_s = _matmul_call(hs, w_in_b, l)
        oa_s = _attn_sample_call(proj_s, cache_k, cache_v, l, page_table, rel_bias_table, lam, subln_a[l],
                                 1.0 - lam_init, lq)
        hist_s = jnp.concatenate([jnp.zeros((bs, POOL_HIST - POOL_BUF, W_C), F32),
                                  state_pool[l].astype(F32)], axis=1)
        bc_s, ret_s = _mixer_call(proj_s, bs, lq, past_len, state_ret[l], hist_s, pool_w[l], pool_scale[l], F32)
        x1_s, h2_s = _outproj_call(oa_s, bc_s, w_out_b, l, xs, norm_mix_post[l], norm_ffn_pre[l])
        act_s, conv_s = _ffn_up_sample_call(h2_s, w_up_b, l, ffn_conv_w[l], ffn_conv_b[l],
                                            state_conv[l].astype(F32), bs, lq)
        xs, hs = _ffn_down_call(act_s, w_down_b, l, x1_s, norm_ffn_post[l], g_next)
        proj_s3 = proj_s.reshape(bs, lq, -1)
        outs['ks'].append(proj_s3[:, :, COL_KA:COL_KA + W_A].reshape(bs, lq, H_A, DV_A))
        outs['vs'].append(proj_s3[:, :, COL_VA:COL_VA + W_A].reshape(bs, lq, H_A, DV_A))
        outs['rs'].append(ret_s)
        xe_s = jnp.concatenate([state_pool[l].astype(F32), proj_s3[:, :, COL_XC:COL_XC + W_C]], axis=1)
        outs['ps'].append(xe_s[:, -POOL_BUF:])
        outs['cs'].append(conv_s)

    st = lambda k: jnp.stack(outs[k])
    return (xp.reshape(bp, seq, d), xs.reshape(bs, lq, d),
            st('kp'), st('vp'), st('ks'), st('vs'), st('rp'), st('rs'),
            st('pp'), st('ps'), st('cp'), st('cs'))
```

```python
import functools
import math

import numpy as np
import jax
import jax.numpy as jnp
from jax import lax
from jax.experimental import pallas as pl
from jax.experimental.pallas import tpu as pltpu

F32 = jnp.float32
BF16 = jnp.bfloat16

H_A = 6
DH_A = 64
DV_A = 128
W_A = H_A * DV_A
ATT_SCALE = DH_A ** -0.5
NEG_INF = -1e30
H_B = 6
DK_B = 64
DV_B = 128
W_B = H_B * DV_B
ROPE_BASE = 10000.0
N_POOL_GROUPS = 4
POOL_GC = 128
POOL_WINDOWS = (2, 4, 8, 16)
W_C = N_POOL_GROUPS * POOL_GC
POOL_BUF = 15
POOL_HIST = 16
REL_BUCKETS = 32
REL_MAX_DIST = 128
CONV_W = 3
EPS = 1e-6
COL_QA, COL_KA, COL_VA = 0, W_A, 2 * W_A
COL_QB = 3 * W_A
COL_KB = COL_QB + H_B * DK_B
COL_VB = COL_KB + H_B * DK_B
COL_GB = COL_VB + W_B
COL_XC = COL_GB + W_B

LANES = 128
VMEM_LIMIT = 56 * 1024 * 1024


def _params(sem, vmem=VMEM_LIMIT):
    return pltpu.CompilerParams(dimension_semantics=sem, vmem_limit_bytes=vmem)


def _rms(x, g=None):
    y = x * lax.rsqrt(jnp.mean(x * x, axis=-1, keepdims=True) + EPS)
    return y if g is None else y * g


def _pick(n, pref):
    if n <= pref:
        return n
    t = pref
    while n % t:
        t //= 2
    return t


def _norm_kernel(x_ref, g_ref, o_ref):
    o_ref[...] = _rms(x_ref[...], g_ref[...]).astype(o_ref.dtype)


def _norm_call(x, g):
    m, d = x.shape
    tm = _pick(m, 512)
    return pl.pallas_call(
        _norm_kernel,
        grid=(m // tm,),
        in_specs=[pl.BlockSpec((tm, d), lambda i: (i, 0)),
                  pl.BlockSpec((1, d), lambda i: (0, 0))],
        out_specs=pl.BlockSpec((tm, d), lambda i: (i, 0)),
        out_shape=jax.ShapeDtypeStruct((m, d), BF16),
        compiler_params=_params(("arbitrary",)),
        name="rmsnorm",
    )(x, g.reshape(1, d))


def _matmul_kernel(x_ref, w_ref, o_ref, wb):
    @pl.when(pl.program_id(1) == 0)
    def _():
        wb[...] = w_ref[...].astype(BF16)

    o_ref[...] = jnp.dot(x_ref[...], wb[...], preferred_element_type=F32)


def _matmul_call(x, w, layer):
    m, k = x.shape
    n = w.shape[2]
    tm = _pick(m, 1024)
    tn = _pick(n, 1024)
    return pl.pallas_call(
        _matmul_kernel,
        grid=(n // tn, m // tm),
        in_specs=[pl.BlockSpec((tm, k), lambda j, i: (i, 0)),
                  pl.BlockSpec((None, k, tn), lambda j, i: (layer, 0, j))],
        out_specs=pl.BlockSpec((tm, tn), lambda j, i: (i, j)),
        out_shape=jax.ShapeDtypeStruct((m, n), F32),
        scratch_shapes=[pltpu.VMEM((k, tn), BF16)],
        compiler_params=_params(("arbitrary", "arbitrary")),
        name="in_proj",
    )(x, w)


def _t5_bucket(dist):
    n = jnp.maximum(dist, 0)
    exact = REL_BUCKETS // 2
    nf = jnp.maximum(n, 1).astype(F32)
    large = exact + (jnp.log(nf / exact) / math.log(REL_MAX_DIST / exact)
                     * (REL_BUCKETS - exact)).astype(jnp.int32)
    return jnp.where(n < exact, n, jnp.minimum(large, REL_BUCKETS - 1))


def _far_bucket_is_constant(min_dist):
    exact = REL_BUCKETS // 2
    v = np.float32(min_dist) / np.float32(exact)
    large = exact + int(np.float32(np.log(v)) / np.float32(math.log(REL_MAX_DIST / exact))
                        * (REL_BUCKETS - exact))
    return min_dist >= exact and large >= REL_BUCKETS


def _attn_prompt_kernel(sc_ref, q_ref, k_ref, v_ref, d0_ref, d1_ref, g_ref, o_ref,
                        kb, vb, qs, s_a, s_b, acc, mrow, *, tile, out_scale):
    h = pl.program_id(1)
    qi = pl.program_id(2)

    @pl.when(qi == 0)
    def _():
        kb[...] = k_ref[...].astype(BF16)
        vb[:, :DV_A] = v_ref[...].astype(BF16)
        vb[:, DV_A:] = jnp.ones((vb.shape[0], DV_A), BF16)

    q = q_ref[...] * ATT_SCALE
    lane = lax.broadcasted_iota(jnp.int32, q.shape, 1)
    qs[0:tile, :] = jnp.where(lane < DH_A, q, 0.0).astype(BF16)
    qs[tile:, :] = jnp.where(lane >= DH_A, q, 0.0).astype(BF16)
    acc[...] = jnp.zeros(acc.shape, F32)
    mrow[...] = jnp.full(mrow.shape, NEG_INF, F32)
    cfar = sc_ref[1 + h]
    reps = tile // LANES

    def scores(kj, s_ref):
        start = pl.multiple_of(kj * tile, tile)
        s_ref[...] = lax.dot_general(qs[...], kb[pl.ds(start, tile), :], (((1,), (1,)), ((), ())),
                                     preferred_element_type=F32)

    def softmax_pv(kj, s_ref, bias_ref):
        start = pl.multiple_of(kj * tile, tile)
        vs = vb[pl.ds(start, tile), :]
        s = s_ref[...]
        if bias_ref is None:
            m_cur = jnp.max(s, axis=-1, keepdims=True) + cfar
        else:
            s = s + jnp.concatenate([bias_ref[...], bias_ref[...]], axis=0)
            m_cur = jnp.max(s, axis=-1, keepdims=True)
        m_prev = mrow[...]
        m_new = jnp.maximum(m_prev, m_cur)
        alpha = jnp.exp(m_prev - m_new)
        shift = m_new - cfar if bias_ref is None else m_new
        p = jnp.exp(s - jnp.tile(shift, (1, reps))).astype(BF16)
        acc[...] = jnp.tile(alpha, (1, 2)) * acc[...] + jnp.dot(p, vs, preferred_element_type=F32)
        mrow[...] = m_new

    n_far = qi - 1
    n_pairs = jnp.maximum(n_far, 0) // 2

    @pl.when(qi >= 1)
    def _():
        scores(0, s_a)

    def pair_body(jj, carry):
        j = 2 * jj
        scores(j + 1, s_b)
        softmax_pv(j, s_a, None)
        scores(j + 2, s_a)
        softmax_pv(j + 1, s_b, None)
        return carry

    lax.fori_loop(0, n_pairs, pair_body, 0)

    @pl.when(jnp.logical_and(qi >= 1, n_far % 2 == 1))
    def _():
        scores(qi - 1, s_b)
        softmax_pv(qi - 2, s_a, None)
        scores(qi, s_a)
        softmax_pv(qi - 1, s_b, d1_ref)
        softmax_pv(qi, s_a, d0_ref)

    @pl.when(jnp.logical_and(qi >= 1, n_far % 2 == 0))
    def _():
        scores(qi, s_b)
        softmax_pv(qi - 1, s_a, d1_ref)
        softmax_pv(qi, s_b, d0_ref)

    @pl.when(qi == 0)
    def _():
        scores(0, s_a)
        softmax_pv(0, s_a, d0_ref)

    lam = sc_ref[0]
    a1 = acc[0:tile, :]
    a2 = acc[tile:, :]
    o = a1[:, :DV_A] / a1[:, DV_A:] - lam * (a2[:, :DV_A] / a2[:, DV_A:])
    o_ref[...] = (_rms(o, g_ref[...]) * out_scale).astype(o_ref.dtype)


def _toeplitz(u, t):
    hh = u.shape[0]
    return jnp.tile(u, (1, t))[:, :t * (2 * t - 1)].reshape(hh, t, 2 * t - 1)[:, :, :t]


def _attn_prompt_call(proj, bsz, seq, table, lam, subln, out_scale):
    tile = _pick(seq, 512)
    nq = seq // tile
    assert tile % LANES == 0 and _far_bucket_is_constant(tile + 1)
    bv = table[_t5_bucket(jnp.arange(2 * tile))].astype(F32).T
    neg = jnp.full((H_A, tile), NEG_INF, F32)
    u0 = jnp.concatenate([bv[:, :1], neg, bv[:, 1:tile][:, ::-1]], axis=1)
    u1 = jnp.concatenate([bv[:, tile:tile + 1], bv[:, 1:tile][:, ::-1], neg[:, :1],
                          bv[:, tile + 1:][:, ::-1]], axis=1)
    d0 = _toeplitz(u0, tile)
    d1 = _toeplitz(u1, tile)
    scal = jnp.concatenate([jnp.reshape(lam, (1,)).astype(F32),
                            table[REL_BUCKETS - 1].astype(F32)])
    kq, kk, kv = COL_QA // DV_A, COL_KA // DV_A, COL_VA // DV_A
    kern = functools.partial(_attn_prompt_kernel, tile=tile, out_scale=out_scale)
    return pl.pallas_call(
        kern,
        grid=(bsz, H_A, nq),
        in_specs=[
            pl.BlockSpec(memory_space=pltpu.SMEM),
            pl.BlockSpec((tile, DV_A), lambda b, h, i: (b * nq + i, kq + h)),
            pl.BlockSpec((seq, DV_A), lambda b, h, i: (b, kk + h)),
            pl.BlockSpec((seq, DV_A), lambda b, h, i: (b, kv + h)),
            pl.BlockSpec((None, tile, tile), lambda b, h, i: (h, 0, 0)),
            pl.BlockSpec((None, tile, tile), lambda b, h, i: (h, 0, 0)),
            pl.BlockSpec((1, DV_A), lambda b, h, i: (0, 0)),
        ],
        out_specs=pl.BlockSpec((tile, DV_A), lambda b, h, i: (b * nq + i, h)),
        out_shape=jax.ShapeDtypeStruct((bsz * seq, W_A), BF16),
        scratch_shapes=[pltpu.VMEM((seq, DV_A), BF16),
                        pltpu.VMEM((seq, 2 * DV_A), BF16),
                        pltpu.VMEM((2 * tile, DV_A), BF16),
                        pltpu.VMEM((2 * tile, tile), F32),
                        pltpu.VMEM((2 * tile, tile), F32),
                        pltpu.VMEM((2 * tile, 2 * DV_A), F32),
                        pltpu.VMEM((2 * tile, LANES), F32)],
        compiler_params=_params(("arbitrary", "arbitrary", "arbitrary")),
        name="attn_prompt",
    )(scal, proj, proj, proj, d0, d1, subln.reshape(1, DV_A).astype(F32))


def _attn_sample_kernel(pt_ref, sc_ref, qbd_ref, kn_ref, vn_ref, bl_ref, bf_ref, bn_ref, g_ref,
                        ck_ref, cv_ref, o_ref, st, oacc, vnew, ring, sems,
                        *, layer, n_seq, n_pages, group, n_slots, page, lq, out_scale):
    b = pl.program_id(0)
    past = n_pages * page
    items_per_seq = 2 * n_pages
    n_items = n_seq * items_per_seq
    n_groups = n_pages // group

    def page_copy(cache_ref, page_idx, slot):
        return pltpu.make_async_copy(cache_ref.at[layer, page_idx], ring.at[slot], sems.at[slot])

    def issue(item, slot, queue):
        seq = item // items_per_seq
        it = item % items_per_seq

        @pl.when(jnp.logical_and(item < n_items, it < n_pages))
        def _():
            page_copy(ck_ref, pt_ref[seq * n_pages + it], slot).start(priority=queue)

        @pl.when(jnp.logical_and(item < n_items, it >= n_pages))
        def _():
            page_copy(cv_ref, pt_ref[seq * n_pages + it - n_pages], slot).start(priority=queue)

    @pl.when(b == 0)
    def _():
        for r in range(n_slots):
            issue(jnp.int32(r), r, r % 2)

    item0 = b * items_per_seq

    def group_pages(base):
        return jnp.concatenate(
            [jnp.concatenate([ring[base + r, h].astype(BF16) for h in range(H_A)], axis=1)
             for r in range(group)], axis=0)

    def k_group(grp, carry):
        base = (grp % (n_slots // group)) * group
        for r in range(group):
            page_copy(ck_ref, pt_ref[b * n_pages + grp * group + r], base + r).wait()
        row = pl.multiple_of(grp * group * page, group * page)
        st[pl.ds(row, group * page), :] = jnp.dot(group_pages(base), qbd_ref[...],
                                                   preferred_element_type=F32)
        for r in range(group):
            issue(item0 + grp * group + r + n_slots, base + r, r % 2)
        return carry

    lax.fori_loop(0, n_groups, k_group, 0)

    st[past:past + page, :] = jnp.full((page, LANES), NEG_INF, F32)
    st[past:past + lq, :] = jnp.dot(kn_ref[...].astype(BF16), qbd_ref[...],
                                    preferred_element_type=F32) + bn_ref[...]
    st[past - page:past, :] = st[past - page:past, :] + bl_ref[...]
    bfar = bf_ref[...]
    n_far = (past - page) // page

    def max_body(i, m):
        row = pl.multiple_of(i * page, page)
        return jnp.maximum(m, jnp.max(st[pl.ds(row, page), :], axis=0, keepdims=True))

    unroll = max(u for u in range(1, 10) if n_far % u == 0) if n_far > 0 else 1
    m_far = lax.fori_loop(0, n_far, max_body, jnp.full((1, LANES), NEG_INF, F32), unroll=unroll)
    m_near = jnp.max(st[past - page:past + page, :], axis=0, keepdims=True)
    m = jnp.maximum(m_far + bfar, m_near)

    def exp_body(i, l):
        row = pl.multiple_of(i * page, page)
        p = jnp.exp(st[pl.ds(row, page), :] - (m - bfar))
        st[pl.ds(row, page), :] = p
        return l + jnp.sum(p, axis=0, keepdims=True)

    l_far = lax.fori_loop(0, n_far, exp_body, jnp.zeros((1, LANES), F32), unroll=unroll)
    p_near = jnp.exp(st[past - page:past + page, :] - m)
    st[past - page:past + page, :] = p_near
    l_lane = l_far + jnp.sum(p_near, axis=0, keepdims=True)
    oacc[...] = jnp.zeros(oacc.shape, F32)
    vnew[...] = jnp.zeros(vnew.shape, BF16)
    vnew[0:lq, :] = vn_ref[...].astype(BF16)

    def v_group(grp, carry):
        base = ((n_groups + grp) % (n_slots // group)) * group
        for r in range(group):
            page_copy(cv_ref, pt_ref[b * n_pages + grp * group + r], base + r).wait()
        row = pl.multiple_of(grp * group * page, group * page)
        pt_t = st[pl.ds(row, group * page), :].T.astype(BF16)
        oacc[...] += jnp.dot(pt_t, group_pages(base), preferred_element_type=F32)
        for r in range(group):
            issue(item0 + n_pages + grp * group + r + n_slots, base + r, r % 2)
        return carry

    lax.fori_loop(0, n_groups, v_group, 0)

    pn_t = st[past:past + page, :].T.astype(BF16)
    rr = lax.broadcasted_iota(jnp.int32, (LANES, LANES), 0)
    cc = lax.broadcasted_iota(jnp.int32, (LANES, LANES), 1)
    l_rows = jnp.sum(jnp.where(rr == cc, jnp.broadcast_to(l_lane, (LANES, LANES)), 0.0),
                     axis=1, keepdims=True)
    lam = sc_ref[0]
    half = LANES // 2
    tot_all = oacc[...] + jnp.dot(pn_t, vnew[...], preferred_element_type=F32)
    for h in range(H_A):
        r1 = h * lq
        r2 = half + h * lq
        tot = tot_all[:, h * DV_A:(h + 1) * DV_A]
        o1 = tot[r1:r1 + lq] / l_rows[r1:r1 + lq]
        o2 = tot[r2:r2 + lq] / l_rows[r2:r2 + lq]
        o = o1 - lam * o2
        o_ref[:, h * DV_A:(h + 1) * DV_A] = _rms(o, g_ref[...]) * out_scale


def _lane_bias(t, lq, half):
    rows = t.shape[1]
    x = jnp.transpose(t, (1, 0, 2)).reshape(rows, H_A * lq)
    x = jnp.concatenate([x, jnp.zeros((rows, half - H_A * lq), F32)], axis=1)
    return jnp.concatenate([x, x], axis=1)


def _attn_sample_call(proj, cache_k, cache_v, layer, page_table, table, lam, subln, out_scale, lq):
    bs = page_table.shape[0]
    n_pages = page_table.shape[1]
    page = cache_k.shape[2]
    assert page == LANES and lq == 8 and H_A * lq <= LANES // 2
    assert _far_bucket_is_constant(page + 1)
    group = _pick(n_pages, 8)
    n_slots = 4 * group if (2 * n_pages) % (4 * group) == 0 else 2 * group
    assert n_pages % group == 0 and (2 * n_pages) % n_slots == 0
    past = n_pages * page
    half = LANES // 2
    ck = jnp.transpose(cache_k, (0, 1, 3, 2, 4))
    cv = jnp.transpose(cache_v, (0, 1, 3, 2, 4))

    q = proj[:, COL_QA:COL_QA + W_A] * ATT_SCALE
    qt = jnp.transpose(q.reshape(bs, lq, W_A), (0, 2, 1))
    row = np.arange(W_A)[:, None]
    lane = np.arange(LANES)[None, :]
    keep = ((row // DV_A) == (lane % half) // lq) & (((row % DV_A) // DH_A) == lane // half)
    qbd = jnp.where(jnp.asarray(keep)[None], jnp.tile(qt, (1, 1, LANES // lq)), 0.0).astype(BF16)

    bv = table[_t5_bucket(jnp.arange(2 * page))].astype(F32).T
    u_last = jnp.concatenate([bv[:, page:], bv[:, :page]], axis=1)
    b_last = _lane_bias(_toeplitz(u_last, page)[:, :, :lq], lq, half)
    neg = jnp.full((H_A, page), NEG_INF, F32)
    u_new = jnp.concatenate([bv[:, :page], neg], axis=1)
    b_new = _lane_bias(_toeplitz(u_new, page)[:, :lq, :lq], lq, half)
    lane_h = np.minimum((np.arange(LANES) % half) // lq, H_A - 1)
    valid = jnp.asarray((np.arange(LANES) % half) < H_A * lq)
    b_far = jnp.where(valid, table[REL_BUCKETS - 1].astype(F32)[lane_h], 0.0).reshape(1, LANES)
    scal = jnp.reshape(lam, (1,)).astype(F32)
    pt_flat = page_table.reshape(-1).astype(jnp.int32)

    kcol, vcol = COL_KA // W_A, COL_VA // W_A
    in_specs = [
        pl.BlockSpec(memory_space=pltpu.SMEM),
        pl.BlockSpec((None, W_A, LANES), lambda b, pt: (b, 0, 0)),
        pl.BlockSpec((lq, W_A), lambda b, pt: (b, kcol)),
        pl.BlockSpec((lq, W_A), lambda b, pt: (b, vcol)),
        pl.BlockSpec((page, LANES), lambda b, pt: (0, 0)),
        pl.BlockSpec((1, LANES), lambda b, pt: (0, 0)),
        pl.BlockSpec((lq, LANES), lambda b, pt: (0, 0)),
        pl.BlockSpec((1, DV_A), lambda b, pt: (0, 0)),
        pl.BlockSpec(memory_space=pl.ANY),
        pl.BlockSpec(memory_space=pl.ANY),
    ]
    kern = functools.partial(_attn_sample_kernel, layer=layer, n_seq=bs, n_pages=n_pages, group=group,
                             n_slots=n_slots, page=page, lq=lq, out_scale=out_scale)
    grid_spec = pltpu.PrefetchScalarGridSpec(
        num_scalar_prefetch=1,
        grid=(bs,),
        in_specs=in_specs,
        out_specs=pl.BlockSpec((None, lq, W_A), lambda b, pt: (b, 0, 0)),
        scratch_shapes=[pltpu.VMEM((past + page, LANES), F32),
                        pltpu.VMEM((LANES, W_A), F32),
                        pltpu.VMEM((page, W_A), BF16),
                        pltpu.VMEM((n_slots, H_A, page, DV_A), F32),
                        pltpu.SemaphoreType.DMA((n_slots,))],
    )
    out = pl.pallas_call(
        kern,
        grid_spec=grid_spec,
        out_shape=jax.ShapeDtypeStruct((bs, lq, W_A), F32),
        compiler_params=_params(("arbitrary",)),
        name="attn_sample",
    )(pt_flat, scal, qbd, proj, proj, b_last, b_far, b_new, subln.reshape(1, DV_A).astype(F32), ck, cv)
    return out.reshape(bs * lq, W_A)


def _log_gamma():
    return np.log(1.0 - 2.0 ** (-5.0 - np.arange(H_B, dtype=np.float64)))


def _retention_constants(chunk_math, rows):
    lg = _log_gamma()
    i = np.arange(rows, dtype=np.float64)
    diff = i[:, None] - i[None, :]
    dmat = np.where(diff >= 0, np.exp(np.maximum(diff, 0.0)[None] * lg[:, None, None]), 0.0)
    xi = np.exp((i[:, None] + 1.0) * lg[None, :])
    zeta = np.exp((chunk_math - 1.0 - i)[:, None] * lg[None, :])
    zeta = np.where(i[:, None] < chunk_math, zeta, 0.0)
    cdec = np.exp(chunk_math * lg)
    xi_l = np.repeat(xi, DV_B, axis=1)
    zeta_l = np.repeat(zeta, DK_B, axis=1) * (DK_B ** -0.5)
    cdec_rows = np.repeat(cdec.reshape(H_B // 2, 2), DK_B, axis=1).reshape(H_B // 2, 2 * DK_B, 1)
    return (jnp.asarray(dmat, F32), jnp.asarray(xi_l, F32), jnp.asarray(zeta_l, F32),
            jnp.asarray(cdec_rows, F32))


def _rope_tables(pos):
    half = DK_B // 2
    inv = ROPE_BASE ** (-jnp.arange(half, dtype=F32) / half)
    ang = pos.astype(F32)[:, None] * inv[None, :]
    cos = jnp.cos(ang)
    sin = jnp.sin(ang)
    cos_l = jnp.tile(jnp.concatenate([cos, cos], axis=1), (1, H_B))
    sin_l = jnp.tile(jnp.concatenate([-sin, sin], axis=1), (1, H_B))
    return cos_l, sin_l


def _mixer_kernel(qb_ref, kb_ref, vb_ref, gb_ref, xc_ref, cos_ref, sin_ref, dmat_ref, xi_ref, zeta_ref,
                  cdec_ref, s0_ref, hist_ref, pw_ref, ps_ref, o_ref, sout_ref, state, ext,
                  *, rows_in, rows, pos0):
    ci = pl.program_id(1)
    n_chunks = pl.num_programs(1)

    @pl.when(ci == 0)
    def _():
        state[...] = s0_ref[...]
        ext[0:POOL_HIST, :] = hist_ref[...]

    def pad_rows(x):
        if rows == rows_in:
            return x
        return jnp.concatenate([x, jnp.zeros((rows - rows_in, x.shape[1]), x.dtype)], axis=0)

    lane = lax.broadcasted_iota(jnp.int32, (rows, LANES), 1)
    first_half = (lane % DK_B) < (DK_B // 2)
    low_head = lane < DK_B

    def rope(x, cos, sin):
        swapped = jnp.where(first_half, pltpu.roll(x, LANES - DK_B // 2, 1), pltpu.roll(x, DK_B // 2, 1))
        return x * cos + swapped * sin

    cos = pad_rows(cos_ref[...])
    sin = pad_rows(sin_ref[...])
    qb = pad_rows(qb_ref[...])
    kb = pad_rows(kb_ref[...])
    vb = pad_rows(vb_ref[...])
    gb = pad_rows(gb_ref[...])
    zeta = zeta_ref[...]
    xi = xi_ref[...]
    for pr in range(H_B // 2):
        sl = slice(pr * LANES, (pr + 1) * LANES)
        q2 = rope(qb[:, sl], cos[:, sl], sin[:, sl])
        k2 = rope(kb[:, sl], cos[:, sl], sin[:, sl])
        kz2 = k2 * zeta[:, sl]
        k2 = k2 * (DK_B ** -0.5)
        s_pair = state[pr]
        s_pair_b = s_pair.astype(BF16)
        upd = jnp.zeros((LANES, DV_B), F32)
        k2b = k2.astype(BF16)
        for hh in range(2):
            h = 2 * pr + hh
            mask = low_head if hh == 0 else jnp.logical_not(low_head)
            qm = jnp.where(mask, q2, 0.0).astype(BF16)
            vh = vb[:, h * DV_B:(h + 1) * DV_B].astype(BF16)
            a = lax.dot_general(qm, k2b, (((1,), (1,)), ((), ())), preferred_element_type=F32)
            a = (a * dmat_ref[h]).astype(BF16)
            o = jnp.dot(a, vh, preferred_element_type=F32)
            o = o + jnp.dot(qm, s_pair_b, preferred_element_type=F32) * xi[:, h * DV_B:(h + 1) * DV_B]
            kzm_t = jnp.where(mask, kz2, 0.0).T.astype(BF16)
            upd = upd + jnp.dot(kzm_t, vh, preferred_element_type=F32)
            g = gb[:, h * DV_B:(h + 1) * DV_B]
            y = _rms(o) * (g / (1.0 + jnp.exp(-g)))
            o_ref[:, h * DV_B:(h + 1) * DV_B] = y[0:rows_in].astype(o_ref.dtype)
        state[pr] = s_pair * cdec_ref[pr] + upd

    @pl.when(ci == n_chunks - 1)
    def _():
        for pr in range(H_B // 2):
            sp = state[pr]
            sout_ref[2 * pr] = sp[0:DK_B]
            sout_ref[2 * pr + 1] = sp[DK_B:]

    xc = xc_ref[...]
    ext[POOL_HIST:POOL_HIST + rows_in, :] = xc
    pos = pos0 + ci * rows_in + lax.broadcasted_iota(jnp.int32, (rows_in, 1), 0)
    for gi, win in enumerate(POOL_WINDOWS):
        sl = slice(gi * POOL_GC, (gi + 1) * POOL_GC)
        tot = xc[:, sl]
        for d in range(1, win):
            tot = tot + ext[POOL_HIST - d:POOL_HIST - d + rows_in, sl]
        cnt = jnp.minimum(pos + 1, win).astype(F32)
        pooled = (tot / cnt - xc[:, sl]).astype(BF16)
        y = jnp.dot(pooled, pw_ref[gi], preferred_element_type=F32) * ps_ref[:, sl]
        o_ref[:, W_B + gi * POOL_GC:W_B + (gi + 1) * POOL_GC] = y.astype(o_ref.dtype)
    ext[0:POOL_HIST, :] = ext[rows_in:rows_in + POOL_HIST, :]


def _mixer_call(proj, bsz, seq, pos0, s0, hist, pool_w, pool_scale, out_dtype):
    rows_in = _pick(seq, 256)
    rows = max(rows_in, LANES)
    n_chunks = seq // rows_in
    assert rows_in % 8 == 0 and (rows_in == rows or n_chunks == 1)
    dmat, xi_l, zeta_l, cdec_rows = _retention_constants(rows_in, rows)
    cos_l, sin_l = _rope_tables(pos0 + jnp.arange(seq))
    s0p = s0.astype(F32).reshape(bsz, H_B // 2, 2 * DK_B, DV_B)
    wq, wv, wc = H_B * DK_B, W_B, W_C
    kern = functools.partial(_mixer_kernel, rows_in=rows_in, rows=rows, pos0=pos0)
    row_map = lambda col: (lambda b, c: (b * n_chunks + c, col))
    const2 = lambda b, c: (0, 0)
    const3 = lambda b, c: (0, 0, 0)
    out, sout = pl.pallas_call(
        kern,
        grid=(bsz, n_chunks),
        in_specs=[
            pl.BlockSpec((rows_in, wq), row_map(COL_QB // wq)),
            pl.BlockSpec((rows_in, wq), row_map(COL_KB // wq)),
            pl.BlockSpec((rows_in, wv), row_map(COL_VB // wv)),
            pl.BlockSpec((rows_in, wv), row_map(COL_GB // wv)),
            pl.BlockSpec((rows_in, wc), row_map(COL_XC // wc)),
            pl.BlockSpec((rows_in, wq), lambda b, c: (c, 0)),
            pl.BlockSpec((rows_in, wq), lambda b, c: (c, 0)),
            pl.BlockSpec((H_B, rows, rows), const3),
            pl.BlockSpec((rows, W_B), const2),
            pl.BlockSpec((rows, wq), const2),
            pl.BlockSpec((H_B // 2, 2 * DK_B, 1), const3),
            pl.BlockSpec((None, H_B // 2, 2 * DK_B, DV_B), lambda b, c: (b, 0, 0, 0)),
            pl.BlockSpec((None, POOL_HIST, W_C), lambda b, c: (b, 0, 0)),
            pl.BlockSpec((N_POOL_GROUPS, POOL_GC, POOL_GC), const3),
            pl.BlockSpec((1, W_C), const2),
        ],
        out_specs=[pl.BlockSpec((rows_in, W_B + W_C), lambda b, c: (b * n_chunks + c, 0)),
                   pl.BlockSpec((None, H_B, DK_B, DV_B), lambda b, c: (b, 0, 0, 0))],
        out_shape=[jax.ShapeDtypeStruct((bsz * seq, W_B + W_C), out_dtype),
                   jax.ShapeDtypeStruct((bsz, H_B, DK_B, DV_B), F32)],
        scratch_shapes=[pltpu.VMEM((H_B // 2, 2 * DK_B, DV_B), F32),
                        pltpu.VMEM((POOL_HIST + rows_in, W_C), F32)],
        compiler_params=_params(("arbitrary", "arbitrary")),
        name="mixer_ret_pool",
    )(proj, proj, proj, proj, proj, cos_l, sin_l, dmat, xi_l, zeta_l, cdec_rows, s0p, hist,
      pool_w.astype(BF16), pool_scale.reshape(1, W_C).astype(F32))
    return out, sout


def _outproj_kernel(oa_ref, bc_ref, w_ref, x_ref, gpost_ref, gnext_ref, x1_ref, h_ref):
    wa = oa_ref.shape[1]
    mix = jnp.dot(oa_ref[...].astype(BF16), w_ref[0:wa, :], preferred_element_type=F32)
    mix = mix + jnp.dot(bc_ref[...].astype(BF16), w_ref[wa:, :], preferred_element_type=F32)
    x1 = x_ref[...] + _rms(mix, gpost_ref[...])
    x1_ref[...] = x1
    h_ref[...] = _rms(x1, gnext_ref[...]).astype(h_ref.dtype)


def _outproj_call(oa, bc, w, layer, x, g_post, g_next):
    m, d = x.shape
    tm = _pick(m, 512)
    row = lambda i: (i, 0)
    const = lambda i: (0, 0)
    return pl.pallas_call(
        _outproj_kernel,
        grid=(m // tm,),
        in_specs=[pl.BlockSpec((tm, oa.shape[1]), row),
                  pl.BlockSpec((tm, bc.shape[1]), row),
                  pl.BlockSpec((None,) + w.shape[1:], lambda i: (layer, 0, 0)),
                  pl.BlockSpec((tm, d), row),
                  pl.BlockSpec((1, d), const),
                  pl.BlockSpec((1, d), const)],
        out_specs=[pl.BlockSpec((tm, d), row), pl.BlockSpec((tm, d), row)],
        out_shape=[jax.ShapeDtypeStruct((m, d), F32), jax.ShapeDtypeStruct((m, d), BF16)],
        compiler_params=_params(("arbitrary",)),
        name="out_proj",
    )(oa, bc, w, x, g_post.reshape(1, d).astype(F32), g_next.reshape(1, d).astype(F32))


def _gelu_tanh(g):
    return 0.5 * g * (1.0 + jnp.tanh(math.sqrt(2.0 / math.pi) * (g + 0.044715 * (g * g * g))))


def _conv_taps(u, tail, cw_ref, cb_ref, cols):
    row8 = lax.broadcasted_iota(jnp.int32, tail.shape, 0)
    r1 = pltpu.roll(u, 1, 0)
    r2 = pltpu.roll(u, 2, 0)
    head1 = jnp.where(row8 < 1, pltpu.roll(tail, 1, 0), r1[0:8])
    head2 = jnp.where(row8 < 2, pltpu.roll(tail, 2, 0), r2[0:8])
    um1 = jnp.concatenate([head1, r1[8:]], axis=0)
    um2 = jnp.concatenate([head2, r2[8:]], axis=0)
    return (um2 * cw_ref[0:1, cols] + um1 * cw_ref[1:2, cols] + u * cw_ref[2:3, cols]) + cb_ref[:, cols]


def _ffn_up_prompt_kernel(h_ref, wa_ref, wg_ref, cwa_ref, cwg_ref, cba_ref, cbg_ref, bufa_ref, bufg_ref,
                          act_ref, lasta_ref, lastg_ref, ta, tg, wab, wgb, *, tm, tn, cc, tiles_per_seq):
    i = pl.program_id(1)

    @pl.when(i == 0)
    def _():
        wab[...] = wa_ref[...].astype(BF16)
        wgb[...] = wg_ref[...].astype(BF16)

    @pl.when(i % tiles_per_seq == 0)
    def _():
        ta[8 - (CONV_W - 1):, :] = bufa_ref[...]
        tg[8 - (CONV_W - 1):, :] = bufg_ref[...]

    hh = h_ref[...]
    for c0 in range(0, tn, cc):
        cols = slice(c0, c0 + cc)
        ua = jnp.dot(hh, wab[:, cols], preferred_element_type=F32)
        ug = jnp.dot(hh, wgb[:, cols], preferred_element_type=F32)
        a = _conv_taps(ua, ta[:, cols], cwa_ref, cba_ref, cols)
        g = _conv_taps(ug, tg[:, cols], cwg_ref, cbg_ref, cols)
        act_ref[:, cols] = (a * _gelu_tanh(g)).astype(act_ref.dtype)
        ta[:, cols] = ua[tm - 8:, :]
        tg[:, cols] = ug[tm - 8:, :]
        lasta_ref[:, cols] = ua[tm - (CONV_W - 1):, :]
        lastg_ref[:, cols] = ug[tm - (CONV_W - 1):, :]


def _ffn_up_prompt_call(h, w_up, layer, conv_w, conv_b, conv_buf, bsz, seq):
    m, d = h.shape
    dff = w_up.shape[2] // 2
    tm = _pick(seq, 1024)
    tn = _pick(dff, 512)
    cc = _pick(tn, 256)
    nj = dff // tn
    tiles_per_seq = seq // tm
    kern = functools.partial(_ffn_up_prompt_kernel, tm=tm, tn=tn, cc=cc, tiles_per_seq=tiles_per_seq)
    cb = conv_b.reshape(1, 2 * dff).astype(F32)
    seq_map_a = lambda j, i: (i // tiles_per_seq, 0, j)
    seq_map_g = lambda j, i: (i // tiles_per_seq, 0, j + nj)
    act, last_a, last_g = pl.pallas_call(
        kern,
        grid=(nj, m // tm),
        in_specs=[pl.BlockSpec((tm, d), lambda j, i: (i, 0)),
                  pl.BlockSpec((None, d, tn), lambda j, i: (layer, 0, j)),
                  pl.BlockSpec((None, d, tn), lambda j, i: (layer, 0, j + nj)),
                  pl.BlockSpec((CONV_W, tn), lambda j, i: (0, j)),
                  pl.BlockSpec((CONV_W, tn), lambda j, i: (0, j + nj)),
                  pl.BlockSpec((1, tn), lambda j, i: (0, j)),
                  pl.BlockSpec((1, tn), lambda j, i: (0, j + nj)),
                  pl.BlockSpec((None, CONV_W - 1, tn), seq_map_a),
                  pl.BlockSpec((None, CONV_W - 1, tn), seq_map_g)],
        out_specs=[pl.BlockSpec((tm, tn), lambda j, i: (i, j)),
                   pl.BlockSpec((None, CONV_W - 1, tn), lambda j, i: (i // tiles_per_seq, 0, j)),
                   pl.BlockSpec((None, CONV_W - 1, tn), lambda j, i: (i // tiles_per_seq, 0, j))],
        out_shape=[jax.ShapeDtypeStruct((m, dff), BF16),
                   jax.ShapeDtypeStruct((bsz, CONV_W - 1, dff), F32),
                   jax.ShapeDtypeStruct((bsz, CONV_W - 1, dff), F32)],
        scratch_shapes=[pltpu.VMEM((8, tn), F32), pltpu.VMEM((8, tn), F32),
                        pltpu.VMEM((d, tn), BF16), pltpu.VMEM((d, tn), BF16)],
        compiler_params=_params(("arbitrary", "arbitrary")),
        name="ffn_up_prompt",
    )(h, w_up, w_up, conv_w, conv_w, cb, cb, conv_buf, conv_buf)
    return act, jnp.concatenate([last_a, last_g], axis=-1)


def _ffn_up_sample_kernel(h_ref, wa_ref, wg_ref, cwa_ref, cwg_ref, cba_ref, cbg_ref,
                          b0a_ref, b0g_ref, b1a_ref, b1g_ref, act_ref, ua_ref, ug_ref, sa, sg, *, m, lq):
    pad = 8
    hh = h_ref[...]
    t = lax.broadcasted_iota(jnp.int32, (m, 1), 0) % lq

    def branch(w_ref, cw_ref, cb_ref, b0_ref, b1_ref, u_ref, s):
        u = jnp.dot(hh, w_ref[...].astype(BF16), preferred_element_type=F32)
        u_ref[...] = u
        s[0:pad, :] = jnp.zeros((pad, s.shape[1]), F32)
        s[pad:pad + m, :] = u
        um1 = jnp.where(t == 0, b1_ref[...], s[pad - 1:pad - 1 + m, :])
        um2 = jnp.where(t == 0, b0_ref[...], jnp.where(t == 1, b1_ref[...], s[pad - 2:pad - 2 + m, :]))
        return um2 * cw_ref[0:1, :] + um1 * cw_ref[1:2, :] + u * cw_ref[2:3, :] + cb_ref[...]

    a = branch(wa_ref, cwa_ref, cba_ref, b0a_ref, b1a_ref, ua_ref, sa)
    g = branch(wg_ref, cwg_ref, cbg_ref, b0g_ref, b1g_ref, ug_ref, sg)
    act_ref[...] = (a * _gelu_tanh(g)).astype(act_ref.dtype)


def _ffn_up_sample_call(h, w_up, layer, conv_w, conv_b, conv_buf, bsz, lq):
    m, d = h.shape
    dff = w_up.shape[2] // 2
    tn = _pick(dff, 512)
    nj = dff // tn
    assert lq >= CONV_W - 1
    kern = functools.partial(_ffn_up_sample_kernel, m=m, lq=lq)
    cb = conv_b.reshape(1, 2 * dff).astype(F32)
    b0 = jnp.repeat(conv_buf[:, 0], lq, axis=0)
    b1 = jnp.repeat(conv_buf[:, 1], lq, axis=0)
    col_a = lambda j: (0, j)
    col_g = lambda j: (0, j + nj)
    act, u_a, u_g = pl.pallas_call(
        kern,
        grid=(nj,),
        in_specs=[pl.BlockSpec((m, d), lambda j: (0, 0)),
                  pl.BlockSpec((None, d, tn), lambda j: (layer, 0, j)),
                  pl.BlockSpec((None, d, tn), lambda j: (layer, 0, j + nj)),
                  pl.BlockSpec((CONV_W, tn), col_a), pl.BlockSpec((CONV_W, tn), col_g),
                  pl.BlockSpec((1, tn), col_a), pl.BlockSpec((1, tn), col_g),
                  pl.BlockSpec((m, tn), col_a), pl.BlockSpec((m, tn), col_g),
                  pl.BlockSpec((m, tn), col_a), pl.BlockSpec((m, tn), col_g)],
        out_specs=[pl.BlockSpec((m, tn), col_a), pl.BlockSpec((m, tn), col_a), pl.BlockSpec((m, tn), col_a)],
        out_shape=[jax.ShapeDtypeStruct((m, dff), BF16),
                   jax.ShapeDtypeStruct((m, dff), F32),
                   jax.ShapeDtypeStruct((m, dff), F32)],
        scratch_shapes=[pltpu.VMEM((8 + m, tn), F32), pltpu.VMEM((8 + m, tn), F32)],
        compiler_params=_params(("arbitrary",)),
        name="ffn_up_sample",
    )(h, w_up, w_up, conv_w, conv_w, cb, cb, b0, b0, b1, b1)
    u = jnp.concatenate([u_a, u_g], axis=-1).reshape(bsz, lq, 2 * dff)
    return act, u[:, lq - (CONV_W - 1):]


def _ffn_down_kernel(act_ref, w_ref, x_ref, gpost_ref, gnext_ref, x2_ref, h_ref, acc):
    k = pl.program_id(1)

    @pl.when(k == 0)
    def _():
        acc[...] = jnp.zeros(acc.shape, F32)

    acc[...] += jnp.dot(act_ref[...], w_ref[...], preferred_element_type=F32)

    @pl.when(k == pl.num_programs(1) - 1)
    def _():
        x2 = x_ref[...] + _rms(acc[...], gpost_ref[...])
        x2_ref[...] = x2
        h_ref[...] = _rms(x2, gnext_ref[...]).astype(h_ref.dtype)


def _ffn_down_call(act, w, layer, x, g_post, g_next):
    m, d = x.shape
    dff = act.shape[1]
    tm = _pick(m, 512)
    tk = dff // 4 if (dff % 4 == 0 and (dff // 4) % LANES == 0) else dff
    row = lambda i, k: (i, 0)
    const = lambda i, k: (0, 0)
    return pl.pallas_call(
        _ffn_down_kernel,
        grid=(m // tm, dff // tk),
        in_specs=[pl.BlockSpec((tm, tk), lambda i, k: (i, k)),
                  pl.BlockSpec((None, tk, d), lambda i, k: (layer, k, 0)),
                  pl.BlockSpec((tm, d), row),
                  pl.BlockSpec((1, d), const),
                  pl.BlockSpec((1, d), const)],
        out_specs=[pl.BlockSpec((tm, d), row), pl.BlockSpec((tm, d), row)],
        out_shape=[jax.ShapeDtypeStruct((m, d), F32), jax.ShapeDtypeStruct((m, d), BF16)],
        scratch_shapes=[pltpu.VMEM((tm, d), F32)],
        compiler_params=_params(("arbitrary", "arbitrary")),
        name="ffn_down",
    )(act, w, x, g_post.reshape(1, d).astype(F32), g_next.reshape(1, d).astype(F32))


def kernel(x_prompt, x_sample, cache_k, cache_v, state_ret, state_pool, state_conv, page_table,
           rel_bias_table, norm_mix_pre, w_in, lambda_q1, lambda_k1, lambda_q2, lambda_k2, subln_a,
           pool_w, pool_scale, w_out, norm_mix_post, norm_ffn_pre, ffn_w_up, ffn_conv_w, ffn_conv_b,
           ffn_w_down, norm_ffn_post):
    bp, seq, d = x_prompt.shape
    bs, lq, _ = x_sample.shape
    depth = w_in.shape[0]
    dff2 = ffn_w_up.shape[2]
    past_len = page_table.shape[1] * cache_k.shape[2]

    xp = x_prompt.reshape(bp * seq, d)
    xs = x_sample.reshape(bs * lq, d)
    hp = _norm_call(xp, norm_mix_pre[0])
    hs = _norm_call(xs, norm_mix_pre[0])
    zeros_hist_p = jnp.zeros((bp, POOL_HIST, W_C), F32)
    zeros_ret_p = jnp.zeros((bp, H_B, DK_B, DV_B), F32)
    zeros_conv_p = jnp.zeros((bp, CONV_W - 1, dff2), F32)

    w_in_b = w_in.astype(F32)
    w_up_b = ffn_w_up.astype(F32)
    w_out_b = w_out.astype(BF16)
    w_down_b = ffn_w_down.astype(BF16)

    outs = {k: [] for k in ('kp', 'vp', 'ks', 'vs', 'rp', 'rs', 'pp', 'ps', 'cp', 'cs')}
    for l in range(depth):
        lam_init = 0.8 - 0.6 * math.exp(-0.3 * l)
        lam = (jnp.exp(jnp.sum(lambda_q1[l].astype(F32) * lambda_k1[l].astype(F32)))
               - jnp.exp(jnp.sum(lambda_q2[l].astype(F32) * lambda_k2[l].astype(F32))) + lam_init)
        g_next = norm_mix_pre[l + 1] if l + 1 < depth else norm_mix_pre[l]

        proj_p = _matmul_call(hp, w_in_b, l)
        oa_p = _attn_prompt_call(proj_p, bp, seq, rel_bias_table, lam, subln_a[l], 1.0 - lam_init)
        bc_p, ret_p = _mixer_call(proj_p, bp, seq, 0, zeros_ret_p, zeros_hist_p, pool_w[l], pool_scale[l], BF16)
        x1_p, h2_p = _outproj_call(oa_p, bc_p, w_out_b, l, xp, norm_mix_post[l], norm_ffn_pre[l])
        act_p, conv_p = _ffn_up_prompt_call(h2_p, w_up_b, l, ffn_conv_w[l], ffn_conv_b[l], zeros_conv_p, bp, seq)
        xp, hp = _ffn_down_call(act_p, w_down_b, l, x1_p, norm_ffn_post[l], g_next)
        proj_p3 = proj_p.reshape(bp, seq, -1)
        outs['kp'].append(proj_p3[:, :, COL_KA:COL_KA + W_A].reshape(bp, seq, H_A, DV_A))
        outs['vp'].append(proj_p3[:, :, COL_VA:COL_VA + W_A].reshape(bp, seq, H_A, DV_A))
        outs['rp'].append(ret_p)
        outs['pp'].append(proj_p3[:, seq - POOL_BUF:, COL_XC:COL_XC + W_C])
        outs['cp'].append(conv_p)

        proj_s = _matmul_call(hs, w_in_b, l)
        oa_s = _attn_sample_call(proj_s, cache_k, cache_v, l, page_table, rel_bias_table, lam, subln_a[l],
                                 1.0 - lam_init, lq)
        hist_s = jnp.concatenate([jnp.zeros((bs, POOL_HIST - POOL_BUF, W_C), F32),
                                  state_pool[l].astype(F32)], axis=1)
        bc_s, ret_s = _mixer_call(proj_s, bs, lq, past_len, state_ret[l], hist_s, pool_w[l], pool_scale[l], F32)
        x1_s, h2_s = _outproj_call(oa_s, bc_s, w_out_b, l, xs, norm_mix_post[l], norm_ffn_pre[l])
        act_s, conv_s = _ffn_up_sample_call(h2_s, w_up_b, l, ffn_conv_w[l], ffn_conv_b[l],
                                            state_conv[l].astype(F32), bs, lq)
        xs, hs = _ffn_down_call(act_s, w_down_b, l, x1_s, norm_ffn_post[l], g_next)
        proj_s3 = proj_s.reshape(bs, lq, -1)
        outs['ks'].append(proj_s3[:, :, COL_KA:COL_KA + W_A].reshape(bs, lq, H_A, DV_A))
        outs['vs'].append(proj_s3[:, :, COL_VA:COL_VA + W_A].reshape(bs, lq, H_A, DV_A))
        outs['rs'].append(ret_s)
        xe_s = jnp.concatenate([state_pool[l].astype(F32), proj_s3[:, :, COL_XC:COL_XC + W_C]], axis=1)
        outs['ps'].append(xe_s[:, -POOL_BUF:])
        outs['cs'].append(conv_s)

    st = lambda k: jnp.stack(outs[k])
    return (xp.reshape(bp, seq, d), xs.reshape(bs, lq, d),
            st('kp'), st('vp'), st('ks'), st('vs'), st('rp'), st('rs'),
            st('pp'), st('ps'), st('cp'), st('cs'))
```

```python
import functools
import math

import numpy as np
import jax
import jax.numpy as jnp
from jax import lax
from jax.experimental import pallas as pl
from jax.experimental.pallas import tpu as pltpu

F32 = jnp.float32
BF16 = jnp.bfloat16

H_A = 6
DH_A = 64
DV_A = 128
W_A = H_A * DV_A
ATT_SCALE = DH_A ** -0.5
NEG_INF = -1e30
H_B = 6
DK_B = 64
DV_B = 128
W_B = H_B * DV_B
ROPE_BASE = 10000.0
N_POOL_GROUPS = 4
POOL_GC = 128
POOL_WINDOWS = (2, 4, 8, 16)
W_C = N_POOL_GROUPS * POOL_GC
POOL_BUF = 15
POOL_HIST = 16
REL_BUCKETS = 32
REL_MAX_DIST = 128
CONV_W = 3
EPS = 1e-6
COL_QA, COL_KA, COL_VA = 0, W_A, 2 * W_A
COL_QB = 3 * W_A
COL_KB = COL_QB + H_B * DK_B
COL_VB = COL_KB + H_B * DK_B
COL_GB = COL_VB + W_B
COL_XC = COL_GB + W_B

LANES = 128
VMEM_LIMIT = 56 * 1024 * 1024


def _params(sem, vmem=VMEM_LIMIT):
    return pltpu.CompilerParams(dimension_semantics=sem, vmem_limit_bytes=vmem)


def _rms(x, g=None):
    y = x * lax.rsqrt(jnp.mean(x * x, axis=-1, keepdims=True) + EPS)
    return y if g is None else y * g


def _pick(n, pref):
    if n <= pref:
        return n
    t = pref
    while n % t:
        t //= 2
    return t


def _norm_kernel(x_ref, g_ref, o_ref):
    o_ref[...] = _rms(x_ref[...], g_ref[...]).astype(o_ref.dtype)


def _norm_call(x, g):
    m, d = x.shape
    tm = _pick(m, 512)
    return pl.pallas_call(
        _norm_kernel,
        grid=(m // tm,),
        in_specs=[pl.BlockSpec((tm, d), lambda i: (i, 0)),
                  pl.BlockSpec((1, d), lambda i: (0, 0))],
        out_specs=pl.BlockSpec((tm, d), lambda i: (i, 0)),
        out_shape=jax.ShapeDtypeStruct((m, d), BF16),
        compiler_params=_params(("arbitrary",)),
        name="rmsnorm",
    )(x, g.reshape(1, d))


def _matmul_kernel(x_ref, w_ref, o_ref, wb):
    @pl.when(pl.program_id(1) == 0)
    def _():
        wb[...] = w_ref[...].astype(BF16)

    o_ref[...] = jnp.dot(x_ref[...], wb[...], preferred_element_type=F32)


def _matmul_call(x, w, layer):
    m, k = x.shape
    n = w.shape[2]
    tm = _pick(m, 1024)
    tn = _pick(n, 1024)
    return pl.pallas_call(
        _matmul_kernel,
        grid=(n // tn, m // tm),
        in_specs=[pl.BlockSpec((tm, k), lambda j, i: (i, 0)),
                  pl.BlockSpec((None, k, tn), lambda j, i: (layer, 0, j))],
        out_specs=pl.BlockSpec((tm, tn), lambda j, i: (i, j)),
        out_shape=jax.ShapeDtypeStruct((m, n), F32),
        scratch_shapes=[pltpu.VMEM((k, tn), BF16)],
        compiler_params=_params(("arbitrary", "arbitrary")),
        name="in_proj",
    )(x, w)


def _t5_bucket(dist):
    n = jnp.maximum(dist, 0)
    exact = REL_BUCKETS // 2
    nf = jnp.maximum(n, 1).astype(F32)
    large = exact + (jnp.log(nf / exact) / math.log(REL_MAX_DIST / exact)
                     * (REL_BUCKETS - exact)).astype(jnp.int32)
    return jnp.where(n < exact, n, jnp.minimum(large, REL_BUCKETS - 1))


def _far_bucket_is_constant(min_dist):
    exact = REL_BUCKETS // 2
    v = np.float32(min_dist) / np.float32(exact)
    large = exact + int(np.float32(np.log(v)) / np.float32(math.log(REL_MAX_DIST / exact))
                        * (REL_BUCKETS - exact))
    return min_dist >= exact and large >= REL_BUCKETS


def _attn_prompt_kernel(sc_ref, q_ref, k_ref, v_ref, d0_ref, d1_ref, g_ref, o_ref,
                        kb, vb, qs, s_a, s_b, acc, mrow, *, tile, nq, out_scale):
    h = pl.program_id(1)
    kb[...] = k_ref[...].astype(BF16)
    vb[:, :DV_A] = v_ref[...].astype(BF16)
    vb[:, DV_A:] = jnp.ones((vb.shape[0], DV_A), BF16)
    cfar = sc_ref[1 + h]
    lam = sc_ref[0]
    reps = tile // LANES
    bufs = (s_a, s_b)

    def load_q(qi):
        q = q_ref[qi * tile:(qi + 1) * tile, :] * ATT_SCALE
        lane = lax.broadcasted_iota(jnp.int32, q.shape, 1)
        qs[0:tile, :] = jnp.where(lane < DH_A, q, 0.0).astype(BF16)
        qs[tile:, :] = jnp.where(lane >= DH_A, q, 0.0).astype(BF16)

    def scores(kj, s_ref):
        start = pl.multiple_of(kj * tile, tile)
        s_ref[...] = lax.dot_general(qs[...], kb[pl.ds(start, tile), :], (((1,), (1,)), ((), ())),
                                     preferred_element_type=F32)

    def softmax_pv(kj, s_ref, bias_ref):
        start = pl.multiple_of(kj * tile, tile)
        vs = vb[pl.ds(start, tile), :]
        s = s_ref[...]
        if bias_ref is None:
            m_cur = jnp.max(s, axis=-1, keepdims=True) + cfar
        else:
            s = s + jnp.concatenate([bias_ref[...], bias_ref[...]], axis=0)
            m_cur = jnp.max(s, axis=-1, keepdims=True)
        m_prev = mrow[...]
        m_new = jnp.maximum(m_prev, m_cur)
        alpha = jnp.exp(m_prev - m_new)
        shift = m_new - cfar if bias_ref is None else m_new
        p = jnp.exp(s - jnp.tile(shift, (1, reps))).astype(BF16)
        acc[...] = jnp.tile(alpha, (1, 2)) * acc[...] + jnp.dot(p, vs, preferred_element_type=F32)
        mrow[...] = m_new

    cur = 0
    load_q(0)
    scores(0, bufs[cur])
    for qi in range(nq):
        acc[...] = jnp.zeros(acc.shape, F32)
        mrow[...] = jnp.full(mrow.shape, NEG_INF, F32)
        n_far = max(qi - 1, 0)
        n_pairs = n_far // 2
        if n_pairs:
            buf_even, buf_odd = bufs[cur], bufs[1 - cur]

            def pair_body(jj, carry):
                j = 2 * jj
                scores(j + 1, buf_odd)
                softmax_pv(j, buf_even, None)
                scores(j + 2, buf_even)
                softmax_pv(j + 1, buf_odd, None)
                return carry

            lax.fori_loop(0, n_pairs, pair_body, 0)
        for kj in range(2 * n_pairs, qi + 1):
            bias_ref = d0_ref if kj == qi else (d1_ref if kj == qi - 1 else None)
            if kj < qi:
                scores(kj + 1, bufs[1 - cur])
            elif qi + 1 < nq:
                load_q(qi + 1)
                scores(0, bufs[1 - cur])
            softmax_pv(kj, bufs[cur], bias_ref)
            cur = 1 - cur
        a1 = acc[0:tile, :]
        a2 = acc[tile:, :]
        o = a1[:, :DV_A] / a1[:, DV_A:] - lam * (a2[:, :DV_A] / a2[:, DV_A:])
        o_ref[qi * tile:(qi + 1) * tile, :] = (_rms(o, g_ref[...]) * out_scale).astype(o_ref.dtype)


def _toeplitz(u, t):
    hh = u.shape[0]
    return jnp.tile(u, (1, t))[:, :t * (2 * t - 1)].reshape(hh, t, 2 * t - 1)[:, :, :t]


def _attn_prompt_call(proj, bsz, seq, table, lam, subln, out_scale):
    tile = _pick(seq, 512)
    nq = seq // tile
    assert tile % LANES == 0 and _far_bucket_is_constant(tile + 1)
    bv = table[_t5_bucket(jnp.arange(2 * tile))].astype(F32).T
    neg = jnp.full((H_A, tile), NEG_INF, F32)
    u0 = jnp.concatenate([bv[:, :1], neg, bv[:, 1:tile][:, ::-1]], axis=1)
    u1 = jnp.concatenate([bv[:, tile:tile + 1], bv[:, 1:tile][:, ::-1], neg[:, :1],
                          bv[:, tile + 1:][:, ::-1]], axis=1)
    d0 = _toeplitz(u0, tile)
    d1 = _toeplitz(u1, tile)
    scal = jnp.concatenate([jnp.reshape(lam, (1,)).astype(F32),
                            table[REL_BUCKETS - 1].astype(F32)])
    kq, kk, kv = COL_QA // DV_A, COL_KA // DV_A, COL_VA // DV_A
    kern = functools.partial(_attn_prompt_kernel, tile=tile, nq=nq, out_scale=out_scale)
    return pl.pallas_call(
        kern,
        grid=(bsz, H_A),
        in_specs=[
            pl.BlockSpec(memory_space=pltpu.SMEM),
            pl.BlockSpec((seq, DV_A), lambda b, h: (b, kq + h)),
            pl.BlockSpec((seq, DV_A), lambda b, h: (b, kk + h)),
            pl.BlockSpec((seq, DV_A), lambda b, h: (b, kv + h)),
            pl.BlockSpec((None, tile, tile), lambda b, h: (h, 0, 0)),
            pl.BlockSpec((None, tile, tile), lambda b, h: (h, 0, 0)),
            pl.BlockSpec((1, DV_A), lambda b, h: (0, 0)),
        ],
        out_specs=pl.BlockSpec((seq, DV_A), lambda b, h: (b, h)),
        out_shape=jax.ShapeDtypeStruct((bsz * seq, W_A), BF16),
        scratch_shapes=[pltpu.VMEM((seq, DV_A), BF16),
                        pltpu.VMEM((seq, 2 * DV_A), BF16),
                        pltpu.VMEM((2 * tile, DV_A), BF16),
                        pltpu.VMEM((2 * tile, tile), F32),
                        pltpu.VMEM((2 * tile, tile), F32),
                        pltpu.VMEM((2 * tile, 2 * DV_A), F32),
                        pltpu.VMEM((2 * tile, LANES), F32)],
        compiler_params=_params(("arbitrary", "arbitrary")),
        name="attn_prompt",
    )(scal, proj, proj, proj, d0, d1, subln.reshape(1, DV_A).astype(F32))


def _attn_sample_kernel(pt_ref, sc_ref, qbd_ref, kn_ref, vn_ref, bl_ref, bf_ref, bn_ref, g_ref,
                        ck_ref, cv_ref, o_ref, st, oacc, vnew, ring, sems,
                        *, layer, n_seq, n_pages, group, n_slots, page, lq, out_scale):
    b = pl.program_id(0)
    past = n_pages * page
    items_per_seq = 2 * n_pages
    n_items = n_seq * items_per_seq
    n_groups = n_pages // group

    def page_copy(cache_ref, page_idx, slot):
        return pltpu.make_async_copy(cache_ref.at[layer, page_idx], ring.at[slot], sems.at[slot])

    def issue(item, slot, queue):
        seq = item // items_per_seq
        it = item % items_per_seq

        @pl.when(jnp.logical_and(item < n_items, it < n_pages))
        def _():
            page_copy(ck_ref, pt_ref[seq * n_pages + it], slot).start(priority=queue)

        @pl.when(jnp.logical_and(item < n_items, it >= n_pages))
        def _():
            page_copy(cv_ref, pt_ref[seq * n_pages + it - n_pages], slot).start(priority=queue)

    @pl.when(b == 0)
    def _():
        for r in range(n_slots):
            issue(jnp.int32(r), r, r % 2)

    item0 = b * items_per_seq

    def group_pages(base):
        return jnp.concatenate(
            [jnp.concatenate([ring[base + r, h].astype(BF16) for h in range(H_A)], axis=1)
             for r in range(group)], axis=0)

    def k_group(grp, carry):
        base = (grp % (n_slots // group)) * group
        for r in range(group):
            page_copy(ck_ref, pt_ref[b * n_pages + grp * group + r], base + r).wait()
        row = pl.multiple_of(grp * group * page, group * page)
        st[pl.ds(row, group * page), :] = jnp.dot(group_pages(base), qbd_ref[...],
                                                   preferred_element_type=F32)
        for r in range(group):
            issue(item0 + grp * group + r + n_slots, base + r, r % 2)
        return carry

    lax.fori_loop(0, n_groups, k_group, 0)

    st[past:past + page, :] = jnp.full((page, LANES), NEG_INF, F32)
    st[past:past + lq, :] = jnp.dot(kn_ref[...].astype(BF16), qbd_ref[...],
                                    preferred_element_type=F32) + bn_ref[...]
    st[past - page:past, :] = st[past - page:past, :] + bl_ref[...]
    bfar = bf_ref[...]
    n_far = (past - page) // page

    def max_body(i, m):
        row = pl.multiple_of(i * page, page)
        return jnp.maximum(m, jnp.max(st[pl.ds(row, page), :], axis=0, keepdims=True))

    unroll = max(u for u in range(1, 10) if n_far % u == 0) if n_far > 0 else 1
    m_far = lax.fori_loop(0, n_far, max_body, jnp.full((1, LANES), NEG_INF, F32), unroll=unroll)
    m_near = jnp.max(st[past - page:past + page, :], axis=0, keepdims=True)
    m = jnp.maximum(m_far + bfar, m_near)

    def exp_body(i, l):
        row = pl.multiple_of(i * page, page)
        p = jnp.exp(st[pl.ds(row, page), :] - (m - bfar))
        st[pl.ds(row, page), :] = p
        return l + jnp.sum(p, axis=0, keepdims=True)

    l_far = lax.fori_loop(0, n_far, exp_body, jnp.zeros((1, LANES), F32), unroll=unroll)
    p_near = jnp.exp(st[past - page:past + page, :] - m)
    st[past - page:past + page, :] = p_near
    l_lane = l_far + jnp.sum(p_near, axis=0, keepdims=True)
    oacc[...] = jnp.zeros(oacc.shape, F32)
    vnew[...] = jnp.zeros(vnew.shape, BF16)
    vnew[0:lq, :] = vn_ref[...].astype(BF16)

    def v_group(grp, carry):
        base = ((n_groups + grp) % (n_slots // group)) * group
        for r in range(group):
            page_copy(cv_ref, pt_ref[b * n_pages + grp * group + r], base + r).wait()
        row = pl.multiple_of(grp * group * page, group * page)
        pt_t = st[pl.ds(row, group * page), :].T.astype(BF16)
        oacc[...] += jnp.dot(pt_t, group_pages(base), preferred_element_type=F32)
        for r in range(group):
            issue(item0 + n_pages + grp * group + r + n_slots, base + r, r % 2)
        return carry

    lax.fori_loop(0, n_groups, v_group, 0)

    pn_t = st[past:past + page, :].T.astype(BF16)
    rr = lax.broadcasted_iota(jnp.int32, (LANES, LANES), 0)
    cc = lax.broadcasted_iota(jnp.int32, (LANES, LANES), 1)
    l_rows = jnp.sum(jnp.where(rr == cc, jnp.broadcast_to(l_lane, (LANES, LANES)), 0.0),
                     axis=1, keepdims=True)
    lam = sc_ref[0]
    half = LANES // 2
    tot_all = oacc[...] + jnp.dot(pn_t, vnew[...], preferred_element_type=F32)
    for h in range(H_A):
        r1 = h * lq
        r2 = half + h * lq
        tot = tot_all[:, h * DV_A:(h + 1) * DV_A]
        o1 = tot[r1:r1 + lq] / l_rows[r1:r1 + lq]
        o2 = tot[r2:r2 + lq] / l_rows[r2:r2 + lq]
        o = o1 - lam * o2
        o_ref[:, h * DV_A:(h + 1) * DV_A] = _rms(o, g_ref[...]) * out_scale


def _lane_bias(t, lq, half):
    rows = t.shape[1]
    x = jnp.transpose(t, (1, 0, 2)).reshape(rows, H_A * lq)
    x = jnp.concatenate([x, jnp.zeros((rows, half - H_A * lq), F32)], axis=1)
    return jnp.concatenate([x, x], axis=1)


def _attn_sample_call(proj, cache_k, cache_v, layer, page_table, table, lam, subln, out_scale, lq):
    bs = page_table.shape[0]
    n_pages = page_table.shape[1]
    page = cache_k.shape[2]
    assert page == LANES and lq == 8 and H_A * lq <= LANES // 2
    assert _far_bucket_is_constant(page + 1)
    group = _pick(n_pages, 8)
    n_slots = 4 * group if (2 * n_pages) % (4 * group) == 0 else 2 * group
    assert n_pages % group == 0 and (2 * n_pages) % n_slots == 0
    past = n_pages * page
    half = LANES // 2
    ck = jnp.transpose(cache_k, (0, 1, 3, 2, 4))
    cv = jnp.transpose(cache_v, (0, 1, 3, 2, 4))

    q = proj[:, COL_QA:COL_QA + W_A] * ATT_SCALE
    qt = jnp.transpose(q.reshape(bs, lq, W_A), (0, 2, 1))
    row = np.arange(W_A)[:, None]
    lane = np.arange(LANES)[None, :]
    keep = ((row // DV_A) == (lane % half) // lq) & (((row % DV_A) // DH_A) == lane // half)
    qbd = jnp.where(jnp.asarray(keep)[None], jnp.tile(qt, (1, 1, LANES // lq)), 0.0).astype(BF16)

    bv = table[_t5_bucket(jnp.arange(2 * page))].astype(F32).T
    u_last = jnp.concatenate([bv[:, page:], bv[:, :page]], axis=1)
    b_last = _lane_bias(_toeplitz(u_last, page)[:, :, :lq], lq, half)
    neg = jnp.full((H_A, page), NEG_INF, F32)
    u_new = jnp.concatenate([bv[:, :page], neg], axis=1)
    b_new = _lane_bias(_toeplitz(u_new, page)[:, :lq, :lq], lq, half)
    lane_h = np.minimum((np.arange(LANES) % half) // lq, H_A - 1)
    valid = jnp.asarray((np.arange(LANES) % half) < H_A * lq)
    b_far = jnp.where(valid, table[REL_BUCKETS - 1].astype(F32)[lane_h], 0.0).reshape(1, LANES)
    scal = jnp.reshape(lam, (1,)).astype(F32)
    pt_flat = page_table.reshape(-1).astype(jnp.int32)

    kcol, vcol = COL_KA // W_A, COL_VA // W_A
    in_specs = [
        pl.BlockSpec(memory_space=pltpu.SMEM),
        pl.BlockSpec((None, W_A, LANES), lambda b, pt: (b, 0, 0)),
        pl.BlockSpec((lq, W_A), lambda b, pt: (b, kcol)),
        pl.BlockSpec((lq, W_A), lambda b, pt: (b, vcol)),
        pl.BlockSpec((page, LANES), lambda b, pt: (0, 0)),
        pl.BlockSpec((1, LANES), lambda b, pt: (0, 0)),
        pl.BlockSpec((lq, LANES), lambda b, pt: (0, 0)),
        pl.BlockSpec((1, DV_A), lambda b, pt: (0, 0)),
        pl.BlockSpec(memory_space=pl.ANY),
        pl.BlockSpec(memory_space=pl.ANY),
    ]
    kern = functools.partial(_attn_sample_kernel, layer=layer, n_seq=bs, n_pages=n_pages, group=group,
                             n_slots=n_slots, page=page, lq=lq, out_scale=out_scale)
    grid_spec = pltpu.PrefetchScalarGridSpec(
        num_scalar_prefetch=1,
        grid=(bs,),
        in_specs=in_specs,
        out_specs=pl.BlockSpec((None, lq, W_A), lambda b, pt: (b, 0, 0)),
        scratch_shapes=[pltpu.VMEM((past + page, LANES), F32),
                        pltpu.VMEM((LANES, W_A), F32),
                        pltpu.VMEM((page, W_A), BF16),
                        pltpu.VMEM((n_slots, H_A, page, DV_A), F32),
                        pltpu.SemaphoreType.DMA((n_slots,))],
    )
    out = pl.pallas_call(
        kern,
        grid_spec=grid_spec,
        out_shape=jax.ShapeDtypeStruct((bs, lq, W_A), F32),
        compiler_params=_params(("arbitrary",)),
        name="attn_sample",
    )(pt_flat, scal, qbd, proj, proj, b_last, b_far, b_new, subln.reshape(1, DV_A).astype(F32), ck, cv)
    return out.reshape(bs * lq, W_A)


def _log_gamma():
    return np.log(1.0 - 2.0 ** (-5.0 - np.arange(H_B, dtype=np.float64)))


def _retention_constants(chunk_math, rows):
    lg = _log_gamma()
    i = np.arange(rows, dtype=np.float64)
    diff = i[:, None] - i[None, :]
    dmat = np.where(diff >= 0, np.exp(np.maximum(diff, 0.0)[None] * lg[:, None, None]), 0.0)
    xi = np.exp((i[:, None] + 1.0) * lg[None, :])
    zeta = np.exp((chunk_math - 1.0 - i)[:, None] * lg[None, :])
    zeta = np.where(i[:, None] < chunk_math, zeta, 0.0)
    cdec = np.exp(chunk_math * lg)
    xi_l = np.repeat(xi, DV_B, axis=1)
    zeta_l = np.repeat(zeta, DK_B, axis=1) * (DK_B ** -0.5)
    cdec_rows = np.repeat(cdec.reshape(H_B // 2, 2), DK_B, axis=1).reshape(H_B // 2, 2 * DK_B, 1)
    return (jnp.asarray(dmat, F32), jnp.asarray(xi_l, F32), jnp.asarray(zeta_l, F32),
            jnp.asarray(cdec_rows, F32))


def _rope_tables(pos):
    half = DK_B // 2
    inv = ROPE_BASE ** (-jnp.arange(half, dtype=F32) / half)
    ang = pos.astype(F32)[:, None] * inv[None, :]
    cos = jnp.cos(ang)
    sin = jnp.sin(ang)
    cos_l = jnp.tile(jnp.concatenate([cos, cos], axis=1), (1, H_B))
    sin_l = jnp.tile(jnp.concatenate([-sin, sin], axis=1), (1, H_B))
    return cos_l, sin_l


def _mixer_kernel(qb_ref, kb_ref, vb_ref, gb_ref, xc_ref, cos_ref, sin_ref, dmat_ref, xi_ref, zeta_ref,
                  cdec_ref, s0_ref, hist_ref, pw_ref, ps_ref, o_ref, sout_ref, state, ext,
                  *, rows_in, rows, pos0):
    ci = pl.program_id(1)
    n_chunks = pl.num_programs(1)

    @pl.when(ci == 0)
    def _():
        state[...] = s0_ref[...]
        ext[0:POOL_HIST, :] = hist_ref[...]

    def pad_rows(x):
        if rows == rows_in:
            return x
        return jnp.concatenate([x, jnp.zeros((rows - rows_in, x.shape[1]), x.dtype)], axis=0)

    lane = lax.broadcasted_iota(jnp.int32, (rows, LANES), 1)
    first_half = (lane % DK_B) < (DK_B // 2)
    low_head = lane < DK_B

    def rope(x, cos, sin):
        swapped = jnp.where(first_half, pltpu.roll(x, LANES - DK_B // 2, 1), pltpu.roll(x, DK_B // 2, 1))
        return x * cos + swapped * sin

    cos = pad_rows(cos_ref[...])
    sin = pad_rows(sin_ref[...])
    qb = pad_rows(qb_ref[...])
    kb = pad_rows(kb_ref[...])
    vb = pad_rows(vb_ref[...])
    gb = pad_rows(gb_ref[...])
    zeta = zeta_ref[...]
    xi = xi_ref[...]
    for pr in range(H_B // 2):
        sl = slice(pr * LANES, (pr + 1) * LANES)
        q2 = rope(qb[:, sl], cos[:, sl], sin[:, sl])
        k2 = rope(kb[:, sl], cos[:, sl], sin[:, sl])
        kz2 = k2 * zeta[:, sl]
        k2 = k2 * (DK_B ** -0.5)
        s_pair = state[pr]
        s_pair_b = s_pair.astype(BF16)
        upd = jnp.zeros((LANES, DV_B), F32)
        k2b = k2.astype(BF16)
        for hh in range(2):
            h = 2 * pr + hh
            mask = low_head if hh == 0 else jnp.logical_not(low_head)
            qm = jnp.where(mask, q2, 0.0).astype(BF16)
            vh = vb[:, h * DV_B:(h + 1) * DV_B].astype(BF16)
            a = lax.dot_general(qm, k2b, (((1,), (1,)), ((), ())), preferred_element_type=F32)
            a = (a * dmat_ref[h]).astype(BF16)
            o = jnp.dot(a, vh, preferred_element_type=F32)
            o = o + jnp.dot(qm, s_pair_b, preferred_element_type=F32) * xi[:, h * DV_B:(h + 1) * DV_B]
            kzm_t = jnp.where(mask, kz2, 0.0).T.astype(BF16)
            upd = upd + jnp.dot(kzm_t, vh, preferred_element_type=F32)
            g = gb[:, h * DV_B:(h + 1) * DV_B]
            y = _rms(o) * (g / (1.0 + jnp.exp(-g)))
            o_ref[:, h * DV_B:(h + 1) * DV_B] = y[0:rows_in].astype(o_ref.dtype)
        state[pr] = s_pair * cdec_ref[pr] + upd

    @pl.when(ci == n_chunks - 1)
    def _():
        for pr in range(H_B // 2):
            sp = state[pr]
            sout_ref[2 * pr] = sp[0:DK_B]
            sout_ref[2 * pr + 1] = sp[DK_B:]

    xc = xc_ref[...]
    ext[POOL_HIST:POOL_HIST + rows_in, :] = xc
    pos = pos0 + ci * rows_in + lax.broadcasted_iota(jnp.int32, (rows_in, 1), 0)
    for gi, win in enumerate(POOL_WINDOWS):
        sl = slice(gi * POOL_GC, (gi + 1) * POOL_GC)
        tot = xc[:, sl]
        for d in range(1, win):
            tot = tot + ext[POOL_HIST - d:POOL_HIST - d + rows_in, sl]
        cnt = jnp.minimum(pos + 1, win).astype(F32)
        pooled = (tot / cnt - xc[:, sl]).astype(BF16)
        y = jnp.dot(pooled, pw_ref[gi], preferred_element_type=F32) * ps_ref[:, sl]
        o_ref[:, W_B + gi * POOL_GC:W_B + (gi + 1) * POOL_GC] = y.astype(o_ref.dtype)
    ext[0:POOL_HIST, :] = ext[rows_in:rows_in + POOL_HIST, :]


def _mixer_call(proj, bsz, seq, pos0, s0, hist, pool_w, pool_scale, out_dtype):
    rows_in = _pick(seq, 256)
    rows = max(rows_in, LANES)
    n_chunks = seq // rows_in
    assert rows_in % 8 == 0 and (rows_in == rows or n_chunks == 1)
    dmat, xi_l, zeta_l, cdec_rows = _retention_constants(rows_in, rows)
    cos_l, sin_l = _rope_tables(pos0 + jnp.arange(seq))
    s0p = s0.astype(F32).reshape(bsz, H_B // 2, 2 * DK_B, DV_B)
    wq, wv, wc = H_B * DK_B, W_B, W_C
    kern = functools.partial(_mixer_kernel, rows_in=rows_in, rows=rows, pos0=pos0)
    row_map = lambda col: (lambda b, c: (b * n_chunks + c, col))
    const2 = lambda b, c: (0, 0)
    const3 = lambda b, c: (0, 0, 0)
    out, sout = pl.pallas_call(
        kern,
        grid=(bsz, n_chunks),
        in_specs=[
            pl.BlockSpec((rows_in, wq), row_map(COL_QB // wq)),
            pl.BlockSpec((rows_in, wq), row_map(COL_KB // wq)),
            pl.BlockSpec((rows_in, wv), row_map(COL_VB // wv)),
            pl.BlockSpec((rows_in, wv), row_map(COL_GB // wv)),
            pl.BlockSpec((rows_in, wc), row_map(COL_XC // wc)),
            pl.BlockSpec((rows_in, wq), lambda b, c: (c, 0)),
            pl.BlockSpec((rows_in, wq), lambda b, c: (c, 0)),
            pl.BlockSpec((H_B, rows, rows), const3),
            pl.BlockSpec((rows, W_B), const2),
            pl.BlockSpec((rows, wq), const2),
            pl.BlockSpec((H_B // 2, 2 * DK_B, 1), const3),
            pl.BlockSpec((None, H_B // 2, 2 * DK_B, DV_B), lambda b, c: (b, 0, 0, 0)),
            pl.BlockSpec((None, POOL_HIST, W_C), lambda b, c: (b, 0, 0)),
            pl.BlockSpec((N_POOL_GROUPS, POOL_GC, POOL_GC), const3),
            pl.BlockSpec((1, W_C), const2),
        ],
        out_specs=[pl.BlockSpec((rows_in, W_B + W_C), lambda b, c: (b * n_chunks + c, 0)),
                   pl.BlockSpec((None, H_B, DK_B, DV_B), lambda b, c: (b, 0, 0, 0))],
        out_shape=[jax.ShapeDtypeStruct((bsz * seq, W_B + W_C), out_dtype),
                   jax.ShapeDtypeStruct((bsz, H_B, DK_B, DV_B), F32)],
        scratch_shapes=[pltpu.VMEM((H_B // 2, 2 * DK_B, DV_B), F32),
                        pltpu.VMEM((POOL_HIST + rows_in, W_C), F32)],
        compiler_params=_params(("arbitrary", "arbitrary")),
        name="mixer_ret_pool",
    )(proj, proj, proj, proj, proj, cos_l, sin_l, dmat, xi_l, zeta_l, cdec_rows, s0p, hist,
      pool_w.astype(BF16), pool_scale.reshape(1, W_C).astype(F32))
    return out, sout


def _outproj_kernel(oa_ref, bc_ref, w_ref, x_ref, gpost_ref, gnext_ref, x1_ref, h_ref):
    wa = oa_ref.shape[1]
    mix = jnp.dot(oa_ref[...].astype(BF16), w_ref[0:wa, :], preferred_element_type=F32)
    mix = mix + jnp.dot(bc_ref[...].astype(BF16), w_ref[wa:, :], preferred_element_type=F32)
    x1 = x_ref[...] + _rms(mix, gpost_ref[...])
    x1_ref[...] = x1
    h_ref[...] = _rms(x1, gnext_ref[...]).astype(h_ref.dtype)


def _outproj_call(oa, bc, w, layer, x, g_post, g_next):
    m, d = x.shape
    tm = _pick(m, 512)
    row = lambda i: (i, 0)
    const = lambda i: (0, 0)
    return pl.pallas_call(
        _outproj_kernel,
        grid=(m // tm,),
        in_specs=[pl.BlockSpec((tm, oa.shape[1]), row),
                  pl.BlockSpec((tm, bc.shape[1]), row),
                  pl.BlockSpec((None,) + w.shape[1:], lambda i: (layer, 0, 0)),
                  pl.BlockSpec((tm, d), row),
                  pl.BlockSpec((1, d), const),
                  pl.BlockSpec((1, d), const)],
        out_specs=[pl.BlockSpec((tm, d), row), pl.BlockSpec((tm, d), row)],
        out_shape=[jax.ShapeDtypeStruct((m, d), F32), jax.ShapeDtypeStruct((m, d), BF16)],
        compiler_params=_params(("arbitrary",)),
        name="out_proj",
    )(oa, bc, w, x, g_post.reshape(1, d).astype(F32), g_next.reshape(1, d).astype(F32))


def _gelu_tanh(g):
    return 0.5 * g * (1.0 + jnp.tanh(math.sqrt(2.0 / math.pi) * (g + 0.044715 * (g * g * g))))


def _conv_taps(u, tail, cw_ref, cb_ref, cols):
    row8 = lax.broadcasted_iota(jnp.int32, tail.shape, 0)
    r1 = pltpu.roll(u, 1, 0)
    r2 = pltpu.roll(u, 2, 0)
    head1 = jnp.where(row8 < 1, pltpu.roll(tail, 1, 0), r1[0:8])
    head2 = jnp.where(row8 < 2, pltpu.roll(tail, 2, 0), r2[0:8])
    um1 = jnp.concatenate([head1, r1[8:]], axis=0)
    um2 = jnp.concatenate([head2, r2[8:]], axis=0)
    return (um2 * cw_ref[0:1, cols] + um1 * cw_ref[1:2, cols] + u * cw_ref[2:3, cols]) + cb_ref[:, cols]


def _ffn_up_prompt_kernel(h_ref, wa_ref, wg_ref, cwa_ref, cwg_ref, cba_ref, cbg_ref, bufa_ref, bufg_ref,
                          act_ref, lasta_ref, lastg_ref, ta, tg, wab, wgb, *, tm, tn, cc, tiles_per_seq):
    i = pl.program_id(1)

    @pl.when(i == 0)
    def _():
        wab[...] = wa_ref[...].astype(BF16)
        wgb[...] = wg_ref[...].astype(BF16)

    @pl.when(i % tiles_per_seq == 0)
    def _():
        ta[8 - (CONV_W - 1):, :] = bufa_ref[...]
        tg[8 - (CONV_W - 1):, :] = bufg_ref[...]

    hh = h_ref[...]
    for c0 in range(0, tn, cc):
        cols = slice(c0, c0 + cc)
        ua = jnp.dot(hh, wab[:, cols], preferred_element_type=F32)
        ug = jnp.dot(hh, wgb[:, cols], preferred_element_type=F32)
        a = _conv_taps(ua, ta[:, cols], cwa_ref, cba_ref, cols)
        g = _conv_taps(ug, tg[:, cols], cwg_ref, cbg_ref, cols)
        act_ref[:, cols] = (a * _gelu_tanh(g)).astype(act_ref.dtype)
        ta[:, cols] = ua[tm - 8:, :]
        tg[:, cols] = ug[tm - 8:, :]
        lasta_ref[:, cols] = ua[tm - (CONV_W - 1):, :]
        lastg_ref[:, cols] = ug[tm - (CONV_W - 1):, :]


def _ffn_up_prompt_call(h, w_up, layer, conv_w, conv_b, conv_buf, bsz, seq):
    m, d = h.shape
    dff = w_up.shape[2] // 2
    tm = _pick(seq, 1024)
    tn = _pick(dff, 512)
    cc = _pick(tn, 256)
    nj = dff // tn
    tiles_per_seq = seq // tm
    kern = functools.partial(_ffn_up_prompt_kernel, tm=tm, tn=tn, cc=cc, tiles_per_seq=tiles_per_seq)
    cb = conv_b.reshape(1, 2 * dff).astype(F32)
    seq_map_a = lambda j, i: (i // tiles_per_seq, 0, j)
    seq_map_g = lambda j, i: (i // tiles_per_seq, 0, j + nj)
    act, last_a, last_g = pl.pallas_call(
        kern,
        grid=(nj, m // tm),
        in_specs=[pl.BlockSpec((tm, d), lambda j, i: (i, 0)),
                  pl.BlockSpec((None, d, tn), lambda j, i: (layer, 0, j)),
                  pl.BlockSpec((None, d, tn), lambda j, i: (layer, 0, j + nj)),
                  pl.BlockSpec((CONV_W, tn), lambda j, i: (0, j)),
                  pl.BlockSpec((CONV_W, tn), lambda j, i: (0, j + nj)),
                  pl.BlockSpec((1, tn), lambda j, i: (0, j)),
                  pl.BlockSpec((1, tn), lambda j, i: (0, j + nj)),
                  pl.BlockSpec((None, CONV_W - 1, tn), seq_map_a),
                  pl.BlockSpec((None, CONV_W - 1, tn), seq_map_g)],
        out_specs=[pl.BlockSpec((tm, tn), lambda j, i: (i, j)),
                   pl.BlockSpec((None, CONV_W - 1, tn), lambda j, i: (i // tiles_per_seq, 0, j)),
                   pl.BlockSpec((None, CONV_W - 1, tn), lambda j, i: (i // tiles_per_seq, 0, j))],
        out_shape=[jax.ShapeDtypeStruct((m, dff), BF16),
                   jax.ShapeDtypeStruct((bsz, CONV_W - 1, dff), F32),
                   jax.ShapeDtypeStruct((bsz, CONV_W - 1, dff), F32)],
        scratch_shapes=[pltpu.VMEM((8, tn), F32), pltpu.VMEM((8, tn), F32),
                        pltpu.VMEM((d, tn), BF16), pltpu.VMEM((d, tn), BF16)],
        compiler_params=_params(("arbitrary", "arbitrary")),
        name="ffn_up_prompt",
    )(h, w_up, w_up, conv_w, conv_w, cb, cb, conv_buf, conv_buf)
    return act, jnp.concatenate([last_a, last_g], axis=-1)


def _ffn_up_sample_kernel(h_ref, wa_ref, wg_ref, cwa_ref, cwg_ref, cba_ref, cbg_ref,
                          b0a_ref, b0g_ref, b1a_ref, b1g_ref, act_ref, ua_ref, ug_ref, sa, sg, *, m, lq):
    pad = 8
    hh = h_ref[...]
    t = lax.broadcasted_iota(jnp.int32, (m, 1), 0) % lq

    def branch(w_ref, cw_ref, cb_ref, b0_ref, b1_ref, u_ref, s):
        u = jnp.dot(hh, w_ref[...].astype(BF16), preferred_element_type=F32)
        u_ref[...] = u
        s[0:pad, :] = jnp.zeros((pad, s.shape[1]), F32)
        s[pad:pad + m, :] = u
        um1 = jnp.where(t == 0, b1_ref[...], s[pad - 1:pad - 1 + m, :])
        um2 = jnp.where(t == 0, b0_ref[...], jnp.where(t == 1, b1_ref[...], s[pad - 2:pad - 2 + m, :]))
        return um2 * cw_ref[0:1, :] + um1 * cw_ref[1:2, :] + u * cw_ref[2:3, :] + cb_ref[...]

    a = branch(wa_ref, cwa_ref, cba_ref, b0a_ref, b1a_ref, ua_ref, sa)
    g = branch(wg_ref, cwg_ref, cbg_ref, b0g_ref, b1g_ref, ug_ref, sg)
    act_ref[...] = (a * _gelu_tanh(g)).astype(act_ref.dtype)


def _ffn_up_sample_call(h, w_up, layer, conv_w, conv_b, conv_buf, bsz, lq):
    m, d = h.shape
    dff = w_up.shape[2] // 2
    tn = _pick(dff, 512)
    nj = dff // tn
    assert lq >= CONV_W - 1
    kern = functools.partial(_ffn_up_sample_kernel, m=m, lq=lq)
    cb = conv_b.reshape(1, 2 * dff).astype(F32)
    b0 = jnp.repeat(conv_buf[:, 0], lq, axis=0)
    b1 = jnp.repeat(conv_buf[:, 1], lq, axis=0)
    col_a = lambda j: (0, j)
    col_g = lambda j: (0, j + nj)
    act, u_a, u_g = pl.pallas_call(
        kern,
        grid=(nj,),
        in_specs=[pl.BlockSpec((m, d), lambda j: (0, 0)),
                  pl.BlockSpec((None, d, tn), lambda j: (layer, 0, j)),
                  pl.BlockSpec((None, d, tn), lambda j: (layer, 0, j + nj)),
                  pl.BlockSpec((CONV_W, tn), col_a), pl.BlockSpec((CONV_W, tn), col_g),
                  pl.BlockSpec((1, tn), col_a), pl.BlockSpec((1, tn), col_g),
                  pl.BlockSpec((m, tn), col_a), pl.BlockSpec((m, tn), col_g),
                  pl.BlockSpec((m, tn), col_a), pl.BlockSpec((m, tn), col_g)],
        out_specs=[pl.BlockSpec((m, tn), col_a), pl.BlockSpec((m, tn), col_a), pl.BlockSpec((m, tn), col_a)],
        out_shape=[jax.ShapeDtypeStruct((m, dff), BF16),
                   jax.ShapeDtypeStruct((m, dff), F32),
                   jax.ShapeDtypeStruct((m, dff), F32)],
        scratch_shapes=[pltpu.VMEM((8 + m, tn), F32), pltpu.VMEM((8 + m, tn), F32)],
        compiler_params=_params(("arbitrary",)),
        name="ffn_up_sample",
    )(h, w_up, w_up, conv_w, conv_w, cb, cb, b0, b0, b1, b1)
    u = jnp.concatenate([u_a, u_g], axis=-1).reshape(bsz, lq, 2 * dff)
    return act, u[:, lq - (CONV_W - 1):]


def _ffn_down_kernel(act_ref, w_ref, x_ref, gpost_ref, gnext_ref, x2_ref, h_ref, acc):
    k = pl.program_id(1)

    @pl.when(k == 0)
    def _():
        acc[...] = jnp.zeros(acc.shape, F32)

    acc[...] += jnp.dot(act_ref[...], w_ref[...], preferred_element_type=F32)

    @pl.when(k == pl.num_programs(1) - 1)
    def _():
        x2 = x_ref[...] + _rms(acc[...], gpost_ref[...])
        x2_ref[...] = x2
        h_ref[...] = _rms(x2, gnext_ref[...]).astype(h_ref.dtype)


def _ffn_down_call(act, w, layer, x, g_post, g_next):
    m, d = x.shape
    dff = act.shape[1]
    tm = _pick(m, 512)
    tk = dff // 4 if (dff % 4 == 0 and (dff // 4) % LANES == 0) else dff
    row = lambda i, k: (i, 0)
    const = lambda i, k: (0, 0)
    return pl.pallas_call(
        _ffn_down_kernel,
        grid=(m // tm, dff // tk),
        in_specs=[pl.BlockSpec((tm, tk), lambda i, k: (i, k)),
                  pl.BlockSpec((None, tk, d), lambda i, k: (layer, k, 0)),
                  pl.BlockSpec((tm, d), row),
                  pl.BlockSpec((1, d), const),
                  pl.BlockSpec((1, d), const)],
        out_specs=[pl.BlockSpec((tm, d), row), pl.BlockSpec((tm, d), row)],
        out_shape=[jax.ShapeDtypeStruct((m, d), F32), jax.ShapeDtypeStruct((m, d), BF16)],
        scratch_shapes=[pltpu.VMEM((tm, d), F32)],
        compiler_params=_params(("arbitrary", "arbitrary")),
        name="ffn_down",
    )(act, w, x, g_post.reshape(1, d).astype(F32), g_next.reshape(1, d).astype(F32))


def kernel(x_prompt, x_sample, cache_k, cache_v, state_ret, state_pool, state_conv, page_table,
           rel_bias_table, norm_mix_pre, w_in, lambda_q1, lambda_k1, lambda_q2, lambda_k2, subln_a,
           pool_w, pool_scale, w_out, norm_mix_post, norm_ffn_pre, ffn_w_up, ffn_conv_w, ffn_conv_b,
           ffn_w_down, norm_ffn_post):
    bp, seq, d = x_prompt.shape
    bs, lq, _ = x_sample.shape
    depth = w_in.shape[0]
    dff2 = ffn_w_up.shape[2]
    past_len = page_table.shape[1] * cache_k.shape[2]

    xp = x_prompt.reshape(bp * seq, d)
    xs = x_sample.reshape(bs * lq, d)
    hp = _norm_call(xp, norm_mix_pre[0])
    hs = _norm_call(xs, norm_mix_pre[0])
    zeros_hist_p = jnp.zeros((bp, POOL_HIST, W_C), F32)
    zeros_ret_p = jnp.zeros((bp, H_B, DK_B, DV_B), F32)
    zeros_conv_p = jnp.zeros((bp, CONV_W - 1, dff2), F32)

    w_in_b = w_in.astype(F32)
    w_up_b = ffn_w_up.astype(F32)
    w_out_b = w_out.astype(BF16)
    w_down_b = ffn_w_down.astype(BF16)

    outs = {k: [] for k in ('kp', 'vp', 'ks', 'vs', 'rp', 'rs', 'pp', 'ps', 'cp', 'cs')}
    for l in range(depth):
        lam_init = 0.8 - 0.6 * math.exp(-0.3 * l)
        lam = (jnp.exp(jnp.sum(lambda_q1[l].astype(F32) * lambda_k1[l].astype(F32)))
               - jnp.exp(jnp.sum(lambda_q2[l].astype(F32) * lambda_k2[l].astype(F32))) + lam_init)
        g_next = norm_mix_pre[l + 1] if l + 1 < depth else norm_mix_pre[l]

        proj_p = _matmul_call(hp, w_in_b, l)
        oa_p = _attn_prompt_call(proj_p, bp, seq, rel_bias_table, lam, subln_a[l], 1.0 - lam_init)
        bc_p, ret_p = _mixer_call(proj_p, bp, seq, 0, zeros_ret_p, zeros_hist_p, pool_w[l], pool_scale[l], BF16)
        x1_p, h2_p = _outproj_call(oa_p, bc_p, w_out_b, l, xp, norm_mix_post[l], norm_ffn_pre[l])
        act_p, conv_p = _ffn_up_prompt_call(h2_p, w_up_b, l, ffn_conv_w[l], ffn_conv_b[l], zeros_conv_p, bp, seq)
        xp, hp = _ffn_down_call(act_p, w_down_b, l, x1_p, norm_ffn_post[l], g_next)
        proj_p3 = proj_p.reshape(bp, seq, -1)
        outs['kp'].append(proj_p3[:, :, COL_KA:COL_KA + W_A].reshape(bp, seq, H_A, DV_A))
        outs['vp'].append(proj_p3[:, :, COL_VA:COL_VA + W_A].reshape(bp, seq, H_A, DV_A))
        outs['rp'].append(ret_p)
        outs['pp'].append(proj_p3[:, seq - POOL_BUF:, COL_XC:COL_XC + W_C])
        outs['cp'].append(conv_p)

        proj_s = _matmul_call(hs, w_in_b, l)
        oa_s = _attn_sample_call(proj_s, cache_k, cache_v, l, page_table, rel_bias_table, lam, subln_a[l],
                                 1.0 - lam_init, lq)
        hist_s = jnp.concatenate([jnp.zeros((bs, POOL_HIST - POOL_BUF, W_C), F32),
                                  state_pool[l].astype(F32)], axis=1)
        bc_s, ret_s = _mixer_call(proj_s, bs, lq, past_len, state_ret[l], hist_s, pool_w[l], pool_scale[l], F32)
        x1_s, h2_s = _outproj_call(oa_s, bc_s, w_out_b, l, xs, norm_mix_post[l], norm_ffn_pre[l])
        act_s, conv_s = _ffn_up_sample_call(h2_s, w_up_b, l, ffn_conv_w[l], ffn_conv_b[l],
                                            state_conv[l].astype(F32), bs, lq)
        xs, hs = _ffn_down_call(act_s, w_down_b, l, x1_s, norm_ffn_post[l], g_next)
        proj_s3 = proj_s.reshape(bs, lq, -1)
        outs['ks'].append(proj_s3[:, :, COL_KA:COL_KA + W_A].reshape(bs, lq, H_A, DV_A))
        outs['vs'].append(proj_s3[:, :, COL_VA:COL_VA + W_A].reshape(bs, lq, H_A, DV_A))
        outs['rs'].append(ret_s)
        xe_s = jnp.concatenate([state_pool[l].astype(F32), proj_s3[:, :, COL_XC:COL_XC + W_C]], axis=1)
        outs['ps'].append(xe_s[:, -POOL_BUF:])
        outs['cs'].append(conv_s)

    st = lambda k: jnp.stack(outs[k])
    return (xp.reshape(bp, seq, d), xs.reshape(bs, lq, d),
            st('kp'), st('vp'), st('ks'), st('vs'), st('rp'), st('rs'),
            st('pp'), st('ps'), st('cp'), st('cs'))
```

```python
import functools
import math

import numpy as np
import jax
import jax.numpy as jnp
from jax import lax
from jax.experimental import pallas as pl
from jax.experimental.pallas import tpu as pltpu

F32 = jnp.float32
BF16 = jnp.bfloat16

H_A = 6
DH_A = 64
DV_A = 128
W_A = H_A * DV_A
ATT_SCALE = DH_A ** -0.5
NEG_INF = -1e30
H_B = 6
DK_B = 64
DV_B = 128
W_B = H_B * DV_B
ROPE_BASE = 10000.0
N_POOL_GROUPS = 4
POOL_GC = 128
POOL_WINDOWS = (2, 4, 8, 16)
W_C = N_POOL_GROUPS * POOL_GC
POOL_BUF = 15
POOL_HIST = 16
REL_BUCKETS = 32
REL_MAX_DIST = 128
CONV_W = 3
EPS = 1e-6
COL_QA, COL_KA, COL_VA = 0, W_A, 2 * W_A
COL_QB = 3 * W_A
COL_KB = COL_QB + H_B * DK_B
COL_VB = COL_KB + H_B * DK_B
COL_GB = COL_VB + W_B
COL_XC = COL_GB + W_B

LANES = 128
VMEM_LIMIT = 56 * 1024 * 1024


def _params(sem, vmem=VMEM_LIMIT):
    return pltpu.CompilerParams(dimension_semantics=sem, vmem_limit_bytes=vmem)


def _rms(x, g=None):
    y = x * lax.rsqrt(jnp.mean(x * x, axis=-1, keepdims=True) + EPS)
    return y if g is None else y * g


def _pick(n, pref):
    if n <= pref:
        return n
    t = pref
    while n % t:
        t //= 2
    return t


def _norm_kernel(x_ref, g_ref, o_ref):
    o_ref[...] = _rms(x_ref[...], g_ref[...]).astype(o_ref.dtype)


def _norm_call(x, g):
    m, d = x.shape
    tm = _pick(m, 512)
    return pl.pallas_call(
        _norm_kernel,
        grid=(m // tm,),
        in_specs=[pl.BlockSpec((tm, d), lambda i: (i, 0)),
                  pl.BlockSpec((1, d), lambda i: (0, 0))],
        out_specs=pl.BlockSpec((tm, d), lambda i: (i, 0)),
        out_shape=jax.ShapeDtypeStruct((m, d), BF16),
        compiler_params=_params(("arbitrary",)),
        name="rmsnorm",
    )(x, g.reshape(1, d))


def _matmul_kernel(x_ref, w_ref, o_ref, wb):
    @pl.when(pl.program_id(1) == 0)
    def _():
        wb[...] = w_ref[...].astype(BF16)

    o_ref[...] = jnp.dot(x_ref[...], wb[...], preferred_element_type=F32)


def _matmul_call(x, w, layer):
    m, k = x.shape
    n = w.shape[2]
    tm = _pick(m, 1024)
    tn = _pick(n, 1024)
    return pl.pallas_call(
        _matmul_kernel,
        grid=(n // tn, m // tm),
        in_specs=[pl.BlockSpec((tm, k), lambda j, i: (i, 0)),
                  pl.BlockSpec((None, k, tn), lambda j, i: (layer, 0, j))],
        out_specs=pl.BlockSpec((tm, tn), lambda j, i: (i, j)),
        out_shape=jax.ShapeDtypeStruct((m, n), F32),
        scratch_shapes=[pltpu.VMEM((k, tn), BF16)],
        compiler_params=_params(("arbitrary", "arbitrary")),
        name="in_proj",
    )(x, w)


def _t5_bucket(dist):
    n = jnp.maximum(dist, 0)
    exact = REL_BUCKETS // 2
    nf = jnp.maximum(n, 1).astype(F32)
    large = exact + (jnp.log(nf / exact) / math.log(REL_MAX_DIST / exact)
                     * (REL_BUCKETS - exact)).astype(jnp.int32)
    return jnp.where(n < exact, n, jnp.minimum(large, REL_BUCKETS - 1))


def _far_bucket_is_constant(min_dist):
    exact = REL_BUCKETS // 2
    v = np.float32(min_dist) / np.float32(exact)
    large = exact + int(np.float32(np.log(v)) / np.float32(math.log(REL_MAX_DIST / exact))
                        * (REL_BUCKETS - exact))
    return min_dist >= exact and large >= REL_BUCKETS


def _attn_prompt_kernel(sc_ref, q_ref, k_ref, v_ref, d0_ref, d1_ref, g_ref, o_ref,
                        kb, vb, qs, s_a, s_b, acc, mrow, *, tile, nq, out_scale):
    h = pl.program_id(1)
    kb[...] = k_ref[...].astype(BF16)
    vb[:, :DV_A] = v_ref[...].astype(BF16)
    vb[:, DV_A:] = jnp.ones((vb.shape[0], DV_A), BF16)
    cfar = sc_ref[1 + h]
    lam = sc_ref[0]
    reps = tile // LANES
    bufs = (s_a, s_b)

    def load_q(qi):
        q = q_ref[qi * tile:(qi + 1) * tile, :] * ATT_SCALE
        lane = lax.broadcasted_iota(jnp.int32, q.shape, 1)
        qs[0:tile, :] = jnp.where(lane < DH_A, q, 0.0).astype(BF16)
        qs[tile:, :] = jnp.where(lane >= DH_A, q, 0.0).astype(BF16)

    def scores(kj, s_ref):
        start = pl.multiple_of(kj * tile, tile)
        s_ref[...] = lax.dot_general(qs[...], kb[pl.ds(start, tile), :], (((1,), (1,)), ((), ())),
                                     preferred_element_type=F32)

    def softmax_pv(kj, s_ref, bias_ref):
        start = pl.multiple_of(kj * tile, tile)
        vs = vb[pl.ds(start, tile), :]
        s = s_ref[...]
        if bias_ref is None:
            m_cur = jnp.max(s, axis=-1, keepdims=True) + cfar
        else:
            s = s + jnp.concatenate([bias_ref[...], bias_ref[...]], axis=0)
            m_cur = jnp.max(s, axis=-1, keepdims=True)
        m_prev = mrow[...]
        m_new = jnp.maximum(m_prev, m_cur)
        alpha = jnp.exp(m_prev - m_new)
        shift = m_new - cfar if bias_ref is None else m_new
        p = jnp.exp(s - jnp.tile(shift, (1, reps))).astype(BF16)
        acc[...] = jnp.tile(alpha, (1, 2)) * acc[...] + jnp.dot(p, vs, preferred_element_type=F32)
        mrow[...] = m_new

    cur = 0
    load_q(0)
    scores(0, bufs[cur])
    for qi in range(nq):
        acc[...] = jnp.zeros(acc.shape, F32)
        mrow[...] = jnp.full(mrow.shape, NEG_INF, F32)
        n_far = max(qi - 1, 0)
        n_pairs = n_far // 2
        if n_pairs:
            buf_even, buf_odd = bufs[cur], bufs[1 - cur]

            def pair_body(jj, carry):
                j = 2 * jj
                scores(j + 1, buf_odd)
                softmax_pv(j, buf_even, None)
                scores(j + 2, buf_even)
                softmax_pv(j + 1, buf_odd, None)
                return carry

            lax.fori_loop(0, n_pairs, pair_body, 0)
        for kj in range(2 * n_pairs, qi + 1):
            bias_ref = d0_ref if kj == qi else (d1_ref if kj == qi - 1 else None)
            if kj < qi:
                scores(kj + 1, bufs[1 - cur])
            elif qi + 1 < nq:
                load_q(qi + 1)
                scores(0, bufs[1 - cur])
            softmax_pv(kj, bufs[cur], bias_ref)
            cur = 1 - cur
        a1 = acc[0:tile, :]
        a2 = acc[tile:, :]
        o = a1[:, :DV_A] / a1[:, DV_A:] - lam * (a2[:, :DV_A] / a2[:, DV_A:])
        o_ref[qi * tile:(qi + 1) * tile, :] = (_rms(o, g_ref[...]) * out_scale).astype(o_ref.dtype)


def _toeplitz(u, t):
    hh = u.shape[0]
    return jnp.tile(u, (1, t))[:, :t * (2 * t - 1)].reshape(hh, t, 2 * t - 1)[:, :, :t]


def _attn_prompt_call(proj, bsz, seq, table, lam, subln, out_scale):
    tile = _pick(seq, 512)
    nq = seq // tile
    assert tile % LANES == 0 and _far_bucket_is_constant(tile + 1)
    bv = table[_t5_bucket(jnp.arange(2 * tile))].astype(F32).T
    neg = jnp.full((H_A, tile), NEG_INF, F32)
    u0 = jnp.concatenate([bv[:, :1], neg, bv[:, 1:tile][:, ::-1]], axis=1)
    u1 = jnp.concatenate([bv[:, tile:tile + 1], bv[:, 1:tile][:, ::-1], neg[:, :1],
                          bv[:, tile + 1:][:, ::-1]], axis=1)
    d0 = _toeplitz(u0, tile)
    d1 = _toeplitz(u1, tile)
    scal = jnp.concatenate([jnp.reshape(lam, (1,)).astype(F32),
                            table[REL_BUCKETS - 1].astype(F32)])
    kq, kk, kv = COL_QA // DV_A, COL_KA // DV_A, COL_VA // DV_A
    kern = functools.partial(_attn_prompt_kernel, tile=tile, nq=nq, out_scale=out_scale)
    return pl.pallas_call(
        kern,
        grid=(bsz, H_A),
        in_specs=[
            pl.BlockSpec(memory_space=pltpu.SMEM),
            pl.BlockSpec((seq, DV_A), lambda b, h: (b, kq + h)),
            pl.BlockSpec((seq, DV_A), lambda b, h: (b, kk + h)),
            pl.BlockSpec((seq, DV_A), lambda b, h: (b, kv + h)),
            pl.BlockSpec((None, tile, tile), lambda b, h: (h, 0, 0)),
            pl.BlockSpec((None, tile, tile), lambda b, h: (h, 0, 0)),
            pl.BlockSpec((1, DV_A), lambda b, h: (0, 0)),
        ],
        out_specs=pl.BlockSpec((seq, DV_A), lambda b, h: (b, h)),
        out_shape=jax.ShapeDtypeStruct((bsz * seq, W_A), BF16),
        scratch_shapes=[pltpu.VMEM((seq, DV_A), BF16),
                        pltpu.VMEM((seq, 2 * DV_A), BF16),
                        pltpu.VMEM((2 * tile, DV_A), BF16),
                        pltpu.VMEM((2 * tile, tile), F32),
                        pltpu.VMEM((2 * tile, tile), F32),
                        pltpu.VMEM((2 * tile, 2 * DV_A), F32),
                        pltpu.VMEM((2 * tile, LANES), F32)],
        compiler_params=_params(("arbitrary", "arbitrary")),
        name="attn_prompt",
    )(scal, proj, proj, proj, d0, d1, subln.reshape(1, DV_A).astype(F32))


def _attn_sample_kernel(pt_ref, sc_ref, qbd_ref, kn_ref, vn_ref, bl_ref, bf_ref, bn_ref, g_ref,
                        ck_ref, cv_ref, o_ref, st, oacc, vnew, ring, sems,
                        *, layer, n_seq, n_pages, group, n_slots, page, lq, out_scale):
    b = pl.program_id(0)
    past = n_pages * page
    items_per_seq = 2 * n_pages
    n_items = n_seq * items_per_seq
    n_groups = n_pages // group

    def page_copy(cache_ref, page_idx, slot):
        return pltpu.make_async_copy(cache_ref.at[layer, page_idx], ring.at[slot], sems.at[slot])

    def issue(item, slot, queue):
        seq = item // items_per_seq
        it = item % items_per_seq

        @pl.when(jnp.logical_and(item < n_items, it < n_pages))
        def _():
            page_copy(ck_ref, pt_ref[seq * n_pages + it], slot).start(priority=queue)

        @pl.when(jnp.logical_and(item < n_items, it >= n_pages))
        def _():
            page_copy(cv_ref, pt_ref[seq * n_pages + it - n_pages], slot).start(priority=queue)

    @pl.when(b == 0)
    def _():
        for r in range(n_slots):
            issue(jnp.int32(r), r, r % 2)

    item0 = b * items_per_seq

    def group_pages(base):
        return jnp.concatenate(
            [jnp.concatenate([ring[base + r, h].astype(BF16) for h in range(H_A)], axis=1)
             for r in range(group)], axis=0)

    def k_group(grp, carry):
        base = (grp % (n_slots // group)) * group
        for r in range(group):
            page_copy(ck_ref, pt_ref[b * n_pages + grp * group + r], base + r).wait()
        row = pl.multiple_of(grp * group * page, group * page)
        st[pl.ds(row, group * page), :] = jnp.dot(group_pages(base), qbd_ref[...],
                                                   preferred_element_type=F32)
        for r in range(group):
            issue(item0 + grp * group + r + n_slots, base + r, r % 2)
        return carry

    lax.fori_loop(0, n_groups, k_group, 0)

    st[past:past + page, :] = jnp.full((page, LANES), NEG_INF, F32)
    st[past:past + lq, :] = jnp.dot(kn_ref[...].astype(BF16), qbd_ref[...],
                                    preferred_element_type=F32) + bn_ref[...]
    st[past - page:past, :] = st[past - page:past, :] + bl_ref[...]
    bfar = bf_ref[...]
    n_far = (past - page) // page

    def max_body(i, m):
        row = pl.multiple_of(i * page, page)
        return jnp.maximum(m, jnp.max(st[pl.ds(row, page), :], axis=0, keepdims=True))

    unroll = max(u for u in range(1, 10) if n_far % u == 0) if n_far > 0 else 1
    m_far = lax.fori_loop(0, n_far, max_body, jnp.full((1, LANES), NEG_INF, F32), unroll=unroll)
    m_near = jnp.max(st[past - page:past + page, :], axis=0, keepdims=True)
    m = jnp.maximum(m_far + bfar, m_near)

    def exp_body(i, l):
        row = pl.multiple_of(i * page, page)
        p = jnp.exp(st[pl.ds(row, page), :] - (m - bfar))
        st[pl.ds(row, page), :] = p
        return l + jnp.sum(p, axis=0, keepdims=True)

    l_far = lax.fori_loop(0, n_far, exp_body, jnp.zeros((1, LANES), F32), unroll=unroll)
    p_near = jnp.exp(st[past - page:past + page, :] - m)
    st[past - page:past + page, :] = p_near
    l_lane = l_far + jnp.sum(p_near, axis=0, keepdims=True)
    oacc[...] = jnp.zeros(oacc.shape, F32)
    vnew[...] = jnp.zeros(vnew.shape, BF16)
    vnew[0:lq, :] = vn_ref[...].astype(BF16)

    def v_group(grp, carry):
        base = ((n_groups + grp) % (n_slots // group)) * group
        for r in range(group):
            page_copy(cv_ref, pt_ref[b * n_pages + grp * group + r], base + r).wait()
        row = pl.multiple_of(grp * group * page, group * page)
        pt_t = st[pl.ds(row, group * page), :].T.astype(BF16)
        oacc[...] += jnp.dot(pt_t, group_pages(base), preferred_element_type=F32)
        for r in range(group):
            issue(item0 + n_pages + grp * group + r + n_slots, base + r, r % 2)
        return carry

    lax.fori_loop(0, n_groups, v_group, 0)

    pn_t = st[past:past + page, :].T.astype(BF16)
    rr = lax.broadcasted_iota(jnp.int32, (LANES, LANES), 0)
    cc = lax.broadcasted_iota(jnp.int32, (LANES, LANES), 1)
    l_rows = jnp.sum(jnp.where(rr == cc, jnp.broadcast_to(l_lane, (LANES, LANES)), 0.0),
                     axis=1, keepdims=True)
    lam = sc_ref[0]
    half = LANES // 2
    tot_all = oacc[...] + jnp.dot(pn_t, vnew[...], preferred_element_type=F32)
    for h in range(H_A):
        r1 = h * lq
        r2 = half + h * lq
        tot = tot_all[:, h * DV_A:(h + 1) * DV_A]
        o1 = tot[r1:r1 + lq] / l_rows[r1:r1 + lq]
        o2 = tot[r2:r2 + lq] / l_rows[r2:r2 + lq]
        o = o1 - lam * o2
        o_ref[:, h * DV_A:(h + 1) * DV_A] = _rms(o, g_ref[...]) * out_scale


def _lane_bias(t, lq, half):
    rows = t.shape[1]
    x = jnp.transpose(t, (1, 0, 2)).reshape(rows, H_A * lq)
    x = jnp.concatenate([x, jnp.zeros((rows, half - H_A * lq), F32)], axis=1)
    return jnp.concatenate([x, x], axis=1)


def _attn_sample_call(proj, cache_k, cache_v, layer, page_table, table, lam, subln, out_scale, lq):
    bs = page_table.shape[0]
    n_pages = page_table.shape[1]
    page = cache_k.shape[2]
    assert page == LANES and lq == 8 and H_A * lq <= LANES // 2
    assert _far_bucket_is_constant(page + 1)
    group = _pick(n_pages, 8)
    n_slots = 4 * group if (2 * n_pages) % (4 * group) == 0 else 2 * group
    assert n_pages % group == 0 and (2 * n_pages) % n_slots == 0
    past = n_pages * page
    half = LANES // 2
    ck = jnp.transpose(cache_k, (0, 1, 3, 2, 4))
    cv = jnp.transpose(cache_v, (0, 1, 3, 2, 4))

    q = proj[:, COL_QA:COL_QA + W_A] * ATT_SCALE
    qt = jnp.transpose(q.reshape(bs, lq, W_A), (0, 2, 1))
    row = np.arange(W_A)[:, None]
    lane = np.arange(LANES)[None, :]
    keep = ((row // DV_A) == (lane % half) // lq) & (((row % DV_A) // DH_A) == lane // half)
    qbd = jnp.where(jnp.asarray(keep)[None], jnp.tile(qt, (1, 1, LANES // lq)), 0.0).astype(BF16)

    bv = table[_t5_bucket(jnp.arange(2 * page))].astype(F32).T
    u_last = jnp.concatenate([bv[:, page:], bv[:, :page]], axis=1)
    b_last = _lane_bias(_toeplitz(u_last, page)[:, :, :lq], lq, half)
    neg = jnp.full((H_A, page), NEG_INF, F32)
    u_new = jnp.concatenate([bv[:, :page], neg], axis=1)
    b_new = _lane_bias(_toeplitz(u_new, page)[:, :lq, :lq], lq, half)
    lane_h = np.minimum((np.arange(LANES) % half) // lq, H_A - 1)
    valid = jnp.asarray((np.arange(LANES) % half) < H_A * lq)
    b_far = jnp.where(valid, table[REL_BUCKETS - 1].astype(F32)[lane_h], 0.0).reshape(1, LANES)
    scal = jnp.reshape(lam, (1,)).astype(F32)
    pt_flat = page_table.reshape(-1).astype(jnp.int32)

    kcol, vcol = COL_KA // W_A, COL_VA // W_A
    in_specs = [
        pl.BlockSpec(memory_space=pltpu.SMEM),
        pl.BlockSpec((None, W_A, LANES), lambda b, pt: (b, 0, 0)),
        pl.BlockSpec((lq, W_A), lambda b, pt: (b, kcol)),
        pl.BlockSpec((lq, W_A), lambda b, pt: (b, vcol)),
        pl.BlockSpec((page, LANES), lambda b, pt: (0, 0)),
        pl.BlockSpec((1, LANES), lambda b, pt: (0, 0)),
        pl.BlockSpec((lq, LANES), lambda b, pt: (0, 0)),
        pl.BlockSpec((1, DV_A), lambda b, pt: (0, 0)),
        pl.BlockSpec(memory_space=pl.ANY),
        pl.BlockSpec(memory_space=pl.ANY),
    ]
    kern = functools.partial(_attn_sample_kernel, layer=layer, n_seq=bs, n_pages=n_pages, group=group,
                             n_slots=n_slots, page=page, lq=lq, out_scale=out_scale)
    grid_spec = pltpu.PrefetchScalarGridSpec(
        num_scalar_prefetch=1,
        grid=(bs,),
        in_specs=in_specs,
        out_specs=pl.BlockSpec((None, lq, W_A), lambda b, pt: (b, 0, 0)),
        scratch_shapes=[pltpu.VMEM((past + page, LANES), F32),
                        pltpu.VMEM((LANES, W_A), F32),
                        pltpu.VMEM((page, W_A), BF16),
                        pltpu.VMEM((n_slots, H_A, page, DV_A), F32),
                        pltpu.SemaphoreType.DMA((n_slots,))],
    )
    out = pl.pallas_call(
        kern,
        grid_spec=grid_spec,
        out_shape=jax.ShapeDtypeStruct((bs, lq, W_A), F32),
        compiler_params=_params(("arbitrary",)),
        name="attn_sample",
    )(pt_flat, scal, qbd, proj, proj, b_last, b_far, b_new, subln.reshape(1, DV_A).astype(F32), ck, cv)
    return out.reshape(bs * lq, W_A)


def _log_gamma():
    return np.log(1.0 - 2.0 ** (-5.0 - np.arange(H_B, dtype=np.float64)))


def _retention_constants(chunk_math, rows):
    lg = _log_gamma()
    i = np.arange(rows, dtype=np.float64)
    diff = i[:, None] - i[None, :]
    dmat = np.where(diff >= 0, np.exp(np.maximum(diff, 0.0)[None] * lg[:, None, None]), 0.0)
    xi = np.exp((i[:, None] + 1.0) * lg[None, :])
    zeta = np.exp((chunk_math - 1.0 - i)[:, None] * lg[None, :])
    zeta = np.where(i[:, None] < chunk_math, zeta, 0.0)
    cdec = np.exp(chunk_math * lg)
    xi_l = np.repeat(xi, DV_B, axis=1)
    zeta_l = np.repeat(zeta, DK_B, axis=1) * (DK_B ** -0.5)
    cdec_rows = np.repeat(cdec.reshape(H_B // 2, 2), DK_B, axis=1).reshape(H_B // 2, 2 * DK_B, 1)
    return (jnp.asarray(dmat, F32), jnp.asarray(xi_l, F32), jnp.asarray(zeta_l, F32),
            jnp.asarray(cdec_rows, F32))


def _rope_tables(pos):
    half = DK_B // 2
    inv = ROPE_BASE ** (-jnp.arange(half, dtype=F32) / half)
    ang = pos.astype(F32)[:, None] * inv[None, :]
    cos = jnp.cos(ang)
    sin = jnp.sin(ang)
    cos_l = jnp.tile(jnp.concatenate([cos, cos], axis=1), (1, H_B))
    sin_l = jnp.tile(jnp.concatenate([-sin, sin], axis=1), (1, H_B))
    return cos_l, sin_l


def _mixer_kernel(qb_ref, kb_ref, vb_ref, gb_ref, xc_ref, cos_ref, sin_ref, dmat_ref, xi_ref, zeta_ref,
                  cdec_ref, s0_ref, hist_ref, pw_ref, ps_ref, o_ref, sout_ref, state, ext,
                  *, rows_in, rows, pos0):
    ci = pl.program_id(1)
    n_chunks = pl.num_programs(1)

    @pl.when(ci == 0)
    def _():
        state[...] = s0_ref[...]
        ext[0:POOL_HIST, :] = hist_ref[...]

    def pad_rows(x):
        if rows == rows_in:
            return x
        return jnp.concatenate([x, jnp.zeros((rows - rows_in, x.shape[1]), x.dtype)], axis=0)

    lane = lax.broadcasted_iota(jnp.int32, (rows, LANES), 1)
    first_half = (lane % DK_B) < (DK_B // 2)
    low_head = lane < DK_B

    def rope(x, cos, sin):
        swapped = jnp.where(first_half, pltpu.roll(x, LANES - DK_B // 2, 1), pltpu.roll(x, DK_B // 2, 1))
        return x * cos + swapped * sin

    cos = pad_rows(cos_ref[...])
    sin = pad_rows(sin_ref[...])
    qb = pad_rows(qb_ref[...])
    kb = pad_rows(kb_ref[...])
    vb = pad_rows(vb_ref[...])
    gb = pad_rows(gb_ref[...])
    zeta = zeta_ref[...]
    xi = xi_ref[...]
    for pr in range(H_B // 2):
        sl = slice(pr * LANES, (pr + 1) * LANES)
        q2 = rope(qb[:, sl], cos[:, sl], sin[:, sl])
        k2 = rope(kb[:, sl], cos[:, sl], sin[:, sl])
        kz2 = k2 * zeta[:, sl]
        k2 = k2 * (DK_B ** -0.5)
        s_pair = state[pr]
        s_pair_b = s_pair.astype(BF16)
        upd = jnp.zeros((LANES, DV_B), F32)
        k2b = k2.astype(BF16)
        for hh in range(2):
            h = 2 * pr + hh
            mask = low_head if hh == 0 else jnp.logical_not(low_head)
            qm = jnp.where(mask, q2, 0.0).astype(BF16)
            vh = vb[:, h * DV_B:(h + 1) * DV_B].astype(BF16)
            a = lax.dot_general(qm, k2b, (((1,), (1,)), ((), ())), preferred_element_type=F32)
            a = (a * dmat_ref[h]).astype(BF16)
            o = jnp.dot(a, vh, preferred_element_type=F32)
            o = o + jnp.dot(qm, s_pair_b, preferred_element_type=F32) * xi[:, h * DV_B:(h + 1) * DV_B]
            kzm_t = jnp.where(mask, kz2, 0.0).T.astype(BF16)
            upd = upd + jnp.dot(kzm_t, vh, preferred_element_type=F32)
            g = gb[:, h * DV_B:(h + 1) * DV_B]
            y = _rms(o) * (g / (1.0 + jnp.exp(-g)))
            o_ref[:, h * DV_B:(h + 1) * DV_B] = y[0:rows_in].astype(o_ref.dtype)
        state[pr] = s_pair * cdec_ref[pr] + upd

    @pl.when(ci == n_chunks - 1)
    def _():
        for pr in range(H_B // 2):
            sp = state[pr]
            sout_ref[2 * pr] = sp[0:DK_B]
            sout_ref[2 * pr + 1] = sp[DK_B:]

    xc = xc_ref[...]
    ext[POOL_HIST:POOL_HIST + rows_in, :] = xc
    pos = pos0 + ci * rows_in + lax.broadcasted_iota(jnp.int32, (rows_in, 1), 0)
    for gi, win in enumerate(POOL_WINDOWS):
        sl = slice(gi * POOL_GC, (gi + 1) * POOL_GC)
        tot = xc[:, sl]
        for d in range(1, win):
            tot = tot + ext[POOL_HIST - d:POOL_HIST - d + rows_in, sl]
        cnt = jnp.minimum(pos + 1, win).astype(F32)
        pooled = (tot / cnt - xc[:, sl]).astype(BF16)
        y = jnp.dot(pooled, pw_ref[gi], preferred_element_type=F32) * ps_ref[:, sl]
        o_ref[:, W_B + gi * POOL_GC:W_B + (gi + 1) * POOL_GC] = y.astype(o_ref.dtype)
    ext[0:POOL_HIST, :] = ext[rows_in:rows_in + POOL_HIST, :]


def _mixer_call(proj, bsz, seq, pos0, s0, hist, pool_w, pool_scale, out_dtype):
    rows_in = _pick(seq, 256)
    rows = max(rows_in, LANES)
    n_chunks = seq // rows_in
    assert rows_in % 8 == 0 and (rows_in == rows or n_chunks == 1)
    dmat, xi_l, zeta_l, cdec_rows = _retention_constants(rows_in, rows)
    cos_l, sin_l = _rope_tables(pos0 + jnp.arange(seq))
    s0p = s0.astype(F32).reshape(bsz, H_B // 2, 2 * DK_B, DV_B)
    wq, wv, wc = H_B * DK_B, W_B, W_C
    kern = functools.partial(_mixer_kernel, rows_in=rows_in, rows=rows, pos0=pos0)
    row_map = lambda col: (lambda b, c: (b * n_chunks + c, col))
    const2 = lambda b, c: (0, 0)
    const3 = lambda b, c: (0, 0, 0)
    out, sout = pl.pallas_call(
        kern,
        grid=(bsz, n_chunks),
        in_specs=[
            pl.BlockSpec((rows_in, wq), row_map(COL_QB // wq)),
            pl.BlockSpec((rows_in, wq), row_map(COL_KB // wq)),
            pl.BlockSpec((rows_in, wv), row_map(COL_VB // wv)),
            pl.BlockSpec((rows_in, wv), row_map(COL_GB // wv)),
            pl.BlockSpec((rows_in, wc), row_map(COL_XC // wc)),
            pl.BlockSpec((rows_in, wq), lambda b, c: (c, 0)),
            pl.BlockSpec((rows_in, wq), lambda b, c: (c, 0)),
            pl.BlockSpec((H_B, rows, rows), const3),
            pl.BlockSpec((rows, W_B), const2),
            pl.BlockSpec((rows, wq), const2),
            pl.BlockSpec((H_B // 2, 2 * DK_B, 1), const3),
            pl.BlockSpec((None, H_B // 2, 2 * DK_B, DV_B), lambda b, c: (b, 0, 0, 0)),
            pl.BlockSpec((None, POOL_HIST, W_C), lambda b, c: (b, 0, 0)),
            pl.BlockSpec((N_POOL_GROUPS, POOL_GC, POOL_GC), const3),
            pl.BlockSpec((1, W_C), const2),
        ],
        out_specs=[pl.BlockSpec((rows_in, W_B + W_C), lambda b, c: (b * n_chunks + c, 0)),
                   pl.BlockSpec((None, H_B, DK_B, DV_B), lambda b, c: (b, 0, 0, 0))],
        out_shape=[jax.ShapeDtypeStruct((bsz * seq, W_B + W_C), out_dtype),
                   jax.ShapeDtypeStruct((bsz, H_B, DK_B, DV_B), F32)],
        scratch_shapes=[pltpu.VMEM((H_B // 2, 2 * DK_B, DV_B), F32),
                        pltpu.VMEM((POOL_HIST + rows_in, W_C), F32)],
        compiler_params=_params(("arbitrary", "arbitrary")),
        name="mixer_ret_pool",
    )(proj, proj, proj, proj, proj, cos_l, sin_l, dmat, xi_l, zeta_l, cdec_rows, s0p, hist,
      pool_w.astype(BF16), pool_scale.reshape(1, W_C).astype(F32))
    return out, sout


def _outproj_kernel(oa_ref, bc_ref, w_ref, x_ref, gpost_ref, gnext_ref, x1_ref, h_ref):
    wa = oa_ref.shape[1]
    mix = jnp.dot(oa_ref[...].astype(BF16), w_ref[0:wa, :], preferred_element_type=F32)
    mix = mix + jnp.dot(bc_ref[...].astype(BF16), w_ref[wa:, :], preferred_element_type=F32)
    x1 = x_ref[...] + _rms(mix, gpost_ref[...])
    x1_ref[...] = x1
    h_ref[...] = _rms(x1, gnext_ref[...]).astype(h_ref.dtype)


def _outproj_call(oa, bc, w, layer, x, g_post, g_next):
    m, d = x.shape
    tm = _pick(m, 512)
    row = lambda i: (i, 0)
    const = lambda i: (0, 0)
    return pl.pallas_call(
        _outproj_kernel,
        grid=(m // tm,),
        in_specs=[pl.BlockSpec((tm, oa.shape[1]), row),
                  pl.BlockSpec((tm, bc.shape[1]), row),
                  pl.BlockSpec((None,) + w.shape[1:], lambda i: (layer, 0, 0)),
                  pl.BlockSpec((tm, d), row),
                  pl.BlockSpec((1, d), const),
                  pl.BlockSpec((1, d), const)],
        out_specs=[pl.BlockSpec((tm, d), row), pl.BlockSpec((tm, d), row)],
        out_shape=[jax.ShapeDtypeStruct((m, d), F32), jax.ShapeDtypeStruct((m, d), BF16)],
        compiler_params=_params(("arbitrary",)),
        name="out_proj",
    )(oa, bc, w, x, g_post.reshape(1, d).astype(F32), g_next.reshape(1, d).astype(F32))


def _gelu_tanh(g):
    return 0.5 * g * (1.0 + jnp.tanh(math.sqrt(2.0 / math.pi) * (g + 0.044715 * (g * g * g))))


def _conv_taps(u, tail, cw_ref, cb_ref, cols):
    row8 = lax.broadcasted_iota(jnp.int32, tail.shape, 0)
    r1 = pltpu.roll(u, 1, 0)
    r2 = pltpu.roll(u, 2, 0)
    head1 = jnp.where(row8 < 1, pltpu.roll(tail, 1, 0), r1[0:8])
    head2 = jnp.where(row8 < 2, pltpu.roll(tail, 2, 0), r2[0:8])
    um1 = jnp.concatenate([head1, r1[8:]], axis=0)
    um2 = jnp.concatenate([head2, r2[8:]], axis=0)
    return (um2 * cw_ref[0:1, cols] + um1 * cw_ref[1:2, cols] + u * cw_ref[2:3, cols]) + cb_ref[:, cols]


def _ffn_up_prompt_kernel(h_ref, wa_ref, wg_ref, cwa_ref, cwg_ref, cba_ref, cbg_ref, bufa_ref, bufg_ref,
                          act_ref, lasta_ref, lastg_ref, ta, tg, wab, wgb, *, tm, tn, cc, tiles_per_seq):
    i = pl.program_id(1)

    @pl.when(i == 0)
    def _():
        wab[...] = wa_ref[...].astype(BF16)
        wgb[...] = wg_ref[...].astype(BF16)

    @pl.when(i % tiles_per_seq == 0)
    def _():
        ta[8 - (CONV_W - 1):, :] = bufa_ref[...]
        tg[8 - (CONV_W - 1):, :] = bufg_ref[...]

    hh = h_ref[...]
    for c0 in range(0, tn, cc):
        cols = slice(c0, c0 + cc)
        ua = jnp.dot(hh, wab[:, cols], preferred_element_type=F32)
        ug = jnp.dot(hh, wgb[:, cols], preferred_element_type=F32)
        a = _conv_taps(ua, ta[:, cols], cwa_ref, cba_ref, cols)
        g = _conv_taps(ug, tg[:, cols], cwg_ref, cbg_ref, cols)
        act_ref[:, cols] = (a * _gelu_tanh(g)).astype(act_ref.dtype)
        ta[:, cols] = ua[tm - 8:, :]
        tg[:, cols] = ug[tm - 8:, :]
        lasta_ref[:, cols] = ua[tm - (CONV_W - 1):, :]
        lastg_ref[:, cols] = ug[tm - (CONV_W - 1):, :]


def _ffn_up_prompt_call(h, w_up, layer, conv_w, conv_b, conv_buf, bsz, seq):
    m, d = h.shape
    dff = w_up.shape[2] // 2
    tm = _pick(seq, 1024)
    tn = _pick(dff, 512)
    cc = _pick(tn, 256)
    nj = dff // tn
    tiles_per_seq = seq // tm
    kern = functools.partial(_ffn_up_prompt_kernel, tm=tm, tn=tn, cc=cc, tiles_per_seq=tiles_per_seq)
    cb = conv_b.reshape(1, 2 * dff).astype(F32)
    seq_map_a = lambda j, i: (i // tiles_per_seq, 0, j)
    seq_map_g = lambda j, i: (i // tiles_per_seq, 0, j + nj)
    act, last_a, last_g = pl.pallas_call(
        kern,
        grid=(nj, m // tm),
        in_specs=[pl.BlockSpec((tm, d), lambda j, i: (i, 0)),
                  pl.BlockSpec((None, d, tn), lambda j, i: (layer, 0, j)),
                  pl.BlockSpec((None, d, tn), lambda j, i: (layer, 0, j + nj)),
                  pl.BlockSpec((CONV_W, tn), lambda j, i: (0, j)),
                  pl.BlockSpec((CONV_W, tn), lambda j, i: (0, j + nj)),
                  pl.BlockSpec((1, tn), lambda j, i: (0, j)),
                  pl.BlockSpec((1, tn), lambda j, i: (0, j + nj)),
                  pl.BlockSpec((None, CONV_W - 1, tn), seq_map_a),
                  pl.BlockSpec((None, CONV_W - 1, tn), seq_map_g)],
        out_specs=[pl.BlockSpec((tm, tn), lambda j, i: (i, j)),
                   pl.BlockSpec((None, CONV_W - 1, tn), lambda j, i: (i // tiles_per_seq, 0, j)),
                   pl.BlockSpec((None, CONV_W - 1, tn), lambda j, i: (i // tiles_per_seq, 0, j))],
        out_shape=[jax.ShapeDtypeStruct((m, dff), BF16),
                   jax.ShapeDtypeStruct((bsz, CONV_W - 1, dff), F32),
                   jax.ShapeDtypeStruct((bsz, CONV_W - 1, dff), F32)],
        scratch_shapes=[pltpu.VMEM((8, tn), F32), pltpu.VMEM((8, tn), F32),
                        pltpu.VMEM((d, tn), BF16), pltpu.VMEM((d, tn), BF16)],
        compiler_params=_params(("arbitrary", "arbitrary")),
        name="ffn_up_prompt",
    )(h, w_up, w_up, conv_w, conv_w, cb, cb, conv_buf, conv_buf)
    return act, jnp.concatenate([last_a, last_g], axis=-1)


def _ffn_up_sample_kernel(h_ref, wa_ref, wg_ref, cwa_ref, cwg_ref, cba_ref, cbg_ref,
                          b0a_ref, b0g_ref, b1a_ref, b1g_ref, act_ref, ua_ref, ug_ref, sa, sg, *, m, lq):
    pad = 8
    hh = h_ref[...]
    t = lax.broadcasted_iota(jnp.int32, (m, 1), 0) % lq

    def branch(w_ref, cw_ref, cb_ref, b0_ref, b1_ref, u_ref, s):
        u = jnp.dot(hh, w_ref[...].astype(BF16), preferred_element_type=F32)
        u_ref[...] = u
        s[0:pad, :] = jnp.zeros((pad, s.shape[1]), F32)
        s[pad:pad + m, :] = u
        um1 = jnp.where(t == 0, b1_ref[...], s[pad - 1:pad - 1 + m, :])
        um2 = jnp.where(t == 0, b0_ref[...], jnp.where(t == 1, b1_ref[...], s[pad - 2:pad - 2 + m, :]))
        return um2 * cw_ref[0:1, :] + um1 * cw_ref[1:2, :] + u * cw_ref[2:3, :] + cb_ref[...]

    a = branch(wa_ref, cwa_ref, cba_ref, b0a_ref, b1a_ref, ua_ref, sa)
    g = branch(wg_ref, cwg_ref, cbg_ref, b0g_ref, b1g_ref, ug_ref, sg)
    act_ref[...] = (a * _gelu_tanh(g)).astype(act_ref.dtype)


def _ffn_up_sample_call(h, w_up, layer, conv_w, conv_b, conv_buf, bsz, lq):
    m, d = h.shape
    dff = w_up.shape[2] // 2
    tn = _pick(dff, 512)
    nj = dff // tn
    assert lq >= CONV_W - 1
    kern = functools.partial(_ffn_up_sample_kernel, m=m, lq=lq)
    cb = conv_b.reshape(1, 2 * dff).astype(F32)
    b0 = jnp.repeat(conv_buf[:, 0], lq, axis=0)
    b1 = jnp.repeat(conv_buf[:, 1], lq, axis=0)
    col_a = lambda j: (0, j)
    col_g = lambda j: (0, j + nj)
    act, u_a, u_g = pl.pallas_call(
        kern,
        grid=(nj,),
        in_specs=[pl.BlockSpec((m, d), lambda j: (0, 0)),
                  pl.BlockSpec((None, d, tn), lambda j: (layer, 0, j)),
                  pl.BlockSpec((None, d, tn), lambda j: (layer, 0, j + nj)),
                  pl.BlockSpec((CONV_W, tn), col_a), pl.BlockSpec((CONV_W, tn), col_g),
                  pl.BlockSpec((1, tn), col_a), pl.BlockSpec((1, tn), col_g),
                  pl.BlockSpec((m, tn), col_a), pl.BlockSpec((m, tn), col_g),
                  pl.BlockSpec((m, tn), col_a), pl.BlockSpec((m, tn), col_g)],
        out_specs=[pl.BlockSpec((m, tn), col_a), pl.BlockSpec((m, tn), col_a), pl.BlockSpec((m, tn), col_a)],
        out_shape=[jax.ShapeDtypeStruct((m, dff), BF16),
                   jax.ShapeDtypeStruct((m, dff), F32),
                   jax.ShapeDtypeStruct((m, dff), F32)],
        scratch_shapes=[pltpu.VMEM((8 + m, tn), F32), pltpu.VMEM((8 + m, tn), F32)],
        compiler_params=_params(("arbitrary",)),
        name="ffn_up_sample",
    )(h, w_up, w_up, conv_w, conv_w, cb, cb, b0, b0, b1, b1)
    u = jnp.concatenate([u_a, u_g], axis=-1).reshape(bsz, lq, 2 * dff)
    return act, u[:, lq - (CONV_W - 1):]


def _ffn_down_kernel(act_ref, w_ref, x_ref, gpost_ref, gnext_ref, x2_ref, h_ref):
    f = jnp.dot(act_ref[...], w_ref[...], preferred_element_type=F32)
    x2 = x_ref[...] + _rms(f, gpost_ref[...])
    x2_ref[...] = x2
    h_ref[...] = _rms(x2, gnext_ref[...]).astype(h_ref.dtype)


def _ffn_down_call(act, w, layer, x, g_post, g_next):
    m, d = x.shape
    dff = act.shape[1]
    tm = _pick(m, 256)
    row = lambda i: (i, 0)
    const = lambda i: (0, 0)
    return pl.pallas_call(
        _ffn_down_kernel,
        grid=(m // tm,),
        in_specs=[pl.BlockSpec((tm, dff), row),
                  pl.BlockSpec((None, dff, d), lambda i: (layer, 0, 0), pipeline_mode=pl.Buffered(1)),
                  pl.BlockSpec((tm, d), row),
                  pl.BlockSpec((1, d), const),
                  pl.BlockSpec((1, d), const)],
        out_specs=[pl.BlockSpec((tm, d), row), pl.BlockSpec((tm, d), row)],
        out_shape=[jax.ShapeDtypeStruct((m, d), F32), jax.ShapeDtypeStruct((m, d), BF16)],
        compiler_params=_params(("arbitrary",)),
        name="ffn_down",
    )(act, w, x, g_post.reshape(1, d).astype(F32), g_next.reshape(1, d).astype(F32))


def kernel(x_prompt, x_sample, cache_k, cache_v, state_ret, state_pool, state_conv, page_table,
           rel_bias_table, norm_mix_pre, w_in, lambda_q1, lambda_k1, lambda_q2, lambda_k2, subln_a,
           pool_w, pool_scale, w_out, norm_mix_post, norm_ffn_pre, ffn_w_up, ffn_conv_w, ffn_conv_b,
           ffn_w_down, norm_ffn_post):
    bp, seq, d = x_prompt.shape
    bs, lq, _ = x_sample.shape
    depth = w_in.shape[0]
    dff2 = ffn_w_up.shape[2]
    past_len = page_table.shape[1] * cache_k.shape[2]

    xp = x_prompt.reshape(bp * seq, d)
    xs = x_sample.reshape(bs * lq, d)
    hp = _norm_call(xp, norm_mix_pre[0])
    hs = _norm_call(xs, norm_mix_pre[0])
    zeros_hist_p = jnp.zeros((bp, POOL_HIST, W_C), F32)
    zeros_ret_p = jnp.zeros((bp, H_B, DK_B, DV_B), F32)
    zeros_conv_p = jnp.zeros((bp, CONV_W - 1, dff2), F32)

    w_in_b = w_in.astype(F32)
    w_up_b = ffn_w_up.astype(F32)
    w_out_b = w_out.astype(BF16)
    w_down_b = ffn_w_down.astype(BF16)

    outs = {k: [] for k in ('kp', 'vp', 'ks', 'vs', 'rp', 'rs', 'pp', 'ps', 'cp', 'cs')}
    for l in range(depth):
        lam_init = 0.8 - 0.6 * math.exp(-0.3 * l)
        lam = (jnp.exp(jnp.sum(lambda_q1[l].astype(F32) * lambda_k1[l].astype(F32)))
               - jnp.exp(jnp.sum(lambda_q2[l].astype(F32) * lambda_k2[l].astype(F32))) + lam_init)
        g_next = norm_mix_pre[l + 1] if l + 1 < depth else norm_mix_pre[l]

        proj_p = _matmul_call(hp, w_in_b, l)
        oa_p = _attn_prompt_call(proj_p, bp, seq, rel_bias_table, lam, subln_a[l], 1.0 - lam_init)
        bc_p, ret_p = _mixer_call(proj_p, bp, seq, 0, zeros_ret_p, zeros_hist_p, pool_w[l], pool_scale[l], BF16)
        x1_p, h2_p = _outproj_call(oa_p, bc_p, w_out_b, l, xp, norm_mix_post[l], norm_ffn_pre[l])
        act_p, conv_p = _ffn_up_prompt_call(h2_p, w_up_b, l, ffn_conv_w[l], ffn_conv_b[l], zeros_conv_p, bp, seq)
        xp, hp = _ffn_down_call(act_p, w_down_b, l, x1_p, norm_ffn_post[l], g_next)
        proj_p3 = proj_p.reshape(bp, seq, -1)
        outs['kp'].append(proj_p3[:, :, COL_KA:COL_KA + W_A].reshape(bp, seq, H_A, DV_A))
        outs['vp'].append(proj_p3[:, :, COL_VA:COL_VA + W_A].reshape(bp, seq, H_A, DV_A))
        outs['rp'].append(ret_p)
        outs['pp'].append(proj_p3[:, seq - POOL_BUF:, COL_XC:COL_XC + W_C])
        outs['cp'].append(conv_p)

        proj_s = _matmul_call(hs, w_in_b, l)
        oa_s = _attn_sample_call(proj_s, cache_k, cache_v, l, page_table, rel_bias_table, lam, subln_a[l],
                                 1.0 - lam_init, lq)
        hist_s = jnp.concatenate([jnp.zeros((bs, POOL_HIST - POOL_BUF, W_C), F32),
                                  state_pool[l].astype(F32)], axis=1)
        bc_s, ret_s = _mixer_call(proj_s, bs, lq, past_len, state_ret[l], hist_s, pool_w[l], pool_scale[l], F32)
        x1_s, h2_s = _outproj_call(oa_s, bc_s, w_out_b, l, xs, norm_mix_post[l], norm_ffn_pre[l])
        act_s, conv_s = _ffn_up_sample_call(h2_s, w_up_b, l, ffn_conv_w[l], ffn_conv_b[l],
                                            state_conv[l].astype(F32), bs, lq)
        xs, hs = _ffn_down_call(act_s, w_down_b, l, x1_s, norm_ffn_post[l], g_next)
        proj_s3 = proj_s.reshape(bs, lq, -1)
        outs['ks'].append(proj_s3[:, :, COL_KA:COL_KA + W_A].reshape(bs, lq, H_A, DV_A))
        outs['vs'].append(proj_s3[:, :, COL_VA:COL_VA + W_A].reshape(bs, lq, H_A, DV_A))
        outs['rs'].append(ret_s)
        xe_s = jnp.concatenate([state_pool[l].astype(F32), proj_s3[:, :, COL_XC:COL_XC + W_C]], axis=1)
        outs['ps'].append(xe_s[:, -POOL_BUF:])
        outs['cs'].append(conv_s)

    st = lambda k: jnp.stack(outs[k])
    return (xp.reshape(bp, seq, d), xs.reshape(bs, lq, d),
            st('kp'), st('vp'), st('ks'), st('vs'), st('rp'), st('rs'),
            st('pp'), st('ps'), st('cp'), st('cs'))
```
